```python
import math
import jax, jax.numpy as jnp
from jax import lax
import numpy as np

D_MODEL = 4096
BATCH = 8
SEQ = 4096
DEPTH = 2

N_A_LAYERS = DEPTH // 2
N_B_LAYERS = DEPTH - N_A_LAYERS
CONV_WIDTH = D_MODEL
CONV_KERNEL = 31
MLA_HEADS = 64
Q_LORA_RANK = 1024
KV_LORA_RANK = 512
QK_NOPE_DIM = 128
QK_ROPE_DIM = 64
V_HEAD_DIM = 128
MLA_WIDTH = MLA_HEADS * V_HEAD_DIM
ROPE_BASE = 10000.0
Q_BLOCK = 128
LN_EPS = 1e-5
RMS_EPS = 1e-6
DEEPNORM_ALPHA = (2.0 * DEPTH) ** 0.25
DEEPNORM_BETA = (8.0 * DEPTH) ** -0.25

kernel_name = "yoco_conformer_mla_deepnorm_adaln"


def _layernorm(x, g=None, b=None):
    xf = x.astype(jnp.float32)
    mu = jnp.mean(xf, axis=-1, keepdims=True)
    var = jnp.mean(jnp.square(xf - mu), axis=-1, keepdims=True)
    y = (xf - mu) * lax.rsqrt(var + LN_EPS)
    if g is not None:
        y = y * g.astype(jnp.float32) + b.astype(jnp.float32)
    return y.astype(x.dtype)


def _rmsnorm(x, g):
    xf = x.astype(jnp.float32)
    y = xf * lax.rsqrt(jnp.mean(jnp.square(xf), axis=-1, keepdims=True) + RMS_EPS)
    return (y * g.astype(jnp.float32)).astype(x.dtype)


def _rope(x, pos):
    half = QK_ROPE_DIM // 2
    inv_freq = ROPE_BASE ** (-jnp.arange(half, dtype=jnp.float32) / half)
    ang = pos.astype(jnp.float32)[:, None] * inv_freq[None, :]
    cos = jnp.cos(ang)[None, :, None, :]
    sin = jnp.sin(ang)[None, :, None, :]
    xf = x.astype(jnp.float32)
    x1, x2 = xf[..., :half], xf[..., half:]
    return jnp.concatenate([x1 * cos - x2 * sin, x1 * sin + x2 * cos], axis=-1).astype(x.dtype)


def _modulate(x, c, w_ada, b_ada):
    mod = jax.nn.silu(c) @ w_ada + b_ada
    shift, scale, gate = jnp.split(mod, 3, axis=-1)
    h = _layernorm(x) * (1.0 + scale[:, None, :]) + shift[:, None, :]
    return h, (1.0 + gate)[:, None, :]


def _conformer_conv(h, w_in, w_dw, b_dw, g_cn, b_cn, w_out):
    u = h @ w_in
    a, g, z = jnp.split(u, 3, axis=-1)
    v = a * jax.nn.sigmoid(g)
    v = lax.conv_general_dilated(
        v, w_dw[:, None, :], window_strides=(1,),
        padding=[(CONV_KERNEL - 1, 0)],
        dimension_numbers=("NWC", "WIO", "NWC"),
        feature_group_count=CONV_WIDTH) + b_dw
    v = jax.nn.silu(_layernorm(v, g_cn, b_cn))
    return (v * jax.nn.silu(z)) @ w_out


def _shared_kv(xs, w_kva, g_kv, w_kvb, pos):
    B, S, _ = xs.shape
    kva = xs @ w_kva
    c_kv = _rmsnorm(kva[..., :KV_LORA_RANK], g_kv)
    k_rope = _rope(kva[..., KV_LORA_RANK:][:, :, None, :], pos)[:, :, 0]
    kv = (c_kv @ w_kvb).reshape(B, S, MLA_HEADS, QK_NOPE_DIM + V_HEAD_DIM)
    return kv[..., :QK_NOPE_DIM], k_rope, kv[..., QK_NOPE_DIM:]


def _mla(h, k_nope, k_rope, v, w_in, g_q, w_qb, w_out, pos):
    B, S, _ = h.shape
    u = h @ w_in
    c_q, z = u[..., :Q_LORA_RANK], u[..., Q_LORA_RANK:]
    q = (_rmsnorm(c_q, g_q) @ w_qb).reshape(B, S, MLA_HEADS, QK_NOPE_DIM + QK_ROPE_DIM)
    q_nope = q[..., :QK_NOPE_DIM]
    q_rope = _rope(q[..., QK_NOPE_DIM:], pos)
    scale = (QK_NOPE_DIM + QK_ROPE_DIM) ** -0.5
    outs = []
    for start in range(0, S, Q_BLOCK):
        end = min(start + Q_BLOCK, S)
        s = (jnp.einsum("bqhd,bkhd->bhqk", q_nope[:, start:end], k_nope[:, :end])
             + jnp.einsum("bqhd,bkd->bhqk", q_rope[:, start:end], k_rope[:, :end]))
        s = s.astype(jnp.float32) * scale
        mask = (start + jnp.arange(end - start))[:, None] >= jnp.arange(end)[None, :]
        s = jnp.where(mask[None, None], s, -jnp.inf)
        p = jax.nn.softmax(s, axis=-1).astype(v.dtype)
        outs.append(jnp.einsum("bhqk,bkhd->bqhd", p, v[:, :end]))
    o = jnp.concatenate(outs, axis=1).reshape(B, S, MLA_WIDTH)
    return (o * jax.nn.silu(z)) @ w_out


def _fwd_setup_inputs(seed: int = 0) -> dict:
    key = jax.random.key(seed)
    ks = jax.random.split(key, 24)
    f32 = jnp.float32
    D = D_MODEL
    nrm = lambda k, shape, s: jax.random.normal(k, shape, f32) * s
    return {
        "x": nrm(ks[0], (BATCH, SEQ, D), 1.0),
        "c": nrm(ks[1], (BATCH, D), 1.0),
        "w_ada": nrm(ks[2], (DEPTH, D, 3 * D), D ** -0.5),
        "b_ada": nrm(ks[3], (DEPTH, 3 * D), 0.02),
        "ln_g": 1.0 + nrm(ks[4], (DEPTH, D), 0.02),
        "ln_b": nrm(ks[5], (DEPTH, D), 0.02),
        "a_w_in": nrm(ks[6], (N_A_LAYERS, D, 3 * CONV_WIDTH), D ** -0.5),
        "a_w_dw": nrm(ks[7], (N_A_LAYERS, CONV_KERNEL, CONV_WIDTH), CONV_KERNEL ** -0.5),
        "a_b_dw": nrm(ks[8], (N_A_LAYERS, CONV_WIDTH), 0.02),
        "a_norm_g": 1.0 + nrm(ks[9], (N_A_LAYERS, CONV_WIDTH), 0.02),
        "a_norm_b": nrm(ks[10], (N_A_LAYERS, CONV_WIDTH), 0.02),
        "a_w_out": nrm(ks[11], (N_A_LAYERS, CONV_WIDTH, D), DEEPNORM_BETA * CONV_WIDTH ** -0.5),
        "b_w_in": nrm(ks[12], (N_B_LAYERS, D, Q_LORA_RANK + MLA_WIDTH), D ** -0.5),
        "b_q_norm_g": 1.0 + nrm(ks[13], (N_B_LAYERS, Q_LORA_RANK), 0.02),
        "b_w_qb": nrm(ks[14], (N_B_LAYERS, Q_LORA_RANK, MLA_HEADS * (QK_NOPE_DIM + QK_ROPE_DIM)), Q_LORA_RANK ** -0.5),
        "b_w_out": nrm(ks[15], (N_B_LAYERS, MLA_WIDTH, D), DEEPNORM_BETA * MLA_WIDTH ** -0.5),
        "kv_w_a": nrm(ks[16], (D, KV_LORA_RANK + QK_ROPE_DIM), D ** -0.5),
        "kv_norm_g": 1.0 + nrm(ks[17], (KV_LORA_RANK,), 0.02),
        "kv_w_b": nrm(ks[18], (KV_LORA_RANK, MLA_HEADS * (QK_NOPE_DIM + V_HEAD_DIM)), KV_LORA_RANK ** -0.5),
    }


def _fwd_reference(x, c, w_ada, b_ada, ln_g, ln_b, a_w_in, a_w_dw, a_b_dw, a_norm_g, a_norm_b, a_w_out,
              b_w_in, b_q_norm_g, b_w_qb, b_w_out, kv_w_a, kv_norm_g, kv_w_b):
    S = x.shape[1]
    pos = jnp.arange(S, dtype=jnp.int32)
    k_nope = k_rope = v = None
    for layer in range(DEPTH):
        if layer == N_A_LAYERS:
            k_nope, k_rope, v = _shared_kv(x, kv_w_a, kv_norm_g, kv_w_b, pos)
        h, gate = _modulate(x, c, w_ada[layer], b_ada[layer])
        if layer < N_A_LAYERS:
            i = layer
            out = _conformer_conv(h, a_w_in[i], a_w_dw[i], a_b_dw[i], a_norm_g[i], a_norm_b[i], a_w_out[i])
        else:
            j = layer - N_A_LAYERS
            out = _mla(h, k_nope, k_rope, v, b_w_in[j], b_q_norm_g[j], b_w_qb[j], b_w_out[j], pos)
        x = _layernorm(DEEPNORM_ALPHA * x + gate * out, ln_g[layer], ln_b[layer])
    return x


import jax as _jax
import jax.numpy as _jnp

TWIN_FORMAT = 'train_step'
FWD_PARAMS = ['x', 'c', 'w_ada', 'b_ada', 'ln_g', 'ln_b', 'a_w_in', 'a_w_dw', 'a_b_dw', 'a_norm_g', 'a_norm_b', 'a_w_out', 'b_w_in', 'b_q_norm_g', 'b_w_qb', 'b_w_out', 'kv_w_a', 'kv_norm_g', 'kv_w_b']
TWIN_WEIGHTS = ['w_ada', 'b_ada', 'ln_g', 'ln_b', 'a_w_in', 'a_w_dw', 'a_b_dw', 'a_norm_g', 'a_norm_b', 'a_w_out', 'b_w_in', 'b_q_norm_g', 'b_w_qb', 'b_w_out', 'kv_w_a', 'kv_norm_g', 'kv_w_b']
TWIN_DIFF_INPUT = 'x'
TWIN_INPUTS = ['x', 'c', 'w_ada', 'b_ada', 'ln_g', 'ln_b', 'a_w_in', 'a_w_dw', 'a_b_dw', 'a_norm_g', 'a_norm_b', 'a_w_out', 'b_w_in', 'b_q_norm_g', 'b_w_qb', 'b_w_out', 'kv_w_a', 'kv_norm_g', 'kv_w_b', 'loss_target', 'm_w_ada', 'm_b_ada', 'm_ln_g', 'm_ln_b', 'm_a_w_in', 'm_a_w_dw', 'm_a_b_dw', 'm_a_norm_g', 'm_a_norm_b', 'm_a_w_out', 'm_b_w_in', 'm_b_q_norm_g', 'm_b_w_qb', 'm_b_w_out', 'm_kv_w_a', 'm_kv_norm_g', 'm_kv_w_b', 'v_w_ada', 'v_b_ada', 'v_ln_g', 'v_ln_b', 'v_a_w_in', 'v_a_w_dw', 'v_a_b_dw', 'v_a_norm_g', 'v_a_norm_b', 'v_a_w_out', 'v_b_w_in', 'v_b_q_norm_g', 'v_b_w_qb', 'v_b_w_out', 'v_kv_w_a', 'v_kv_norm_g', 'v_kv_w_b']
TWIN_OUTPUTS = ['loss', 'grad_x', 'grad_w_ada', 'grad_b_ada', 'grad_ln_g', 'grad_ln_b', 'grad_a_w_in', 'grad_a_w_dw', 'grad_a_b_dw', 'grad_a_norm_g', 'grad_a_norm_b', 'grad_a_w_out', 'grad_b_w_in', 'grad_b_q_norm_g', 'grad_b_w_qb', 'grad_b_w_out', 'grad_kv_w_a', 'grad_kv_norm_g', 'grad_kv_w_b', 'delta_w_ada', 'delta_b_ada', 'delta_ln_g', 'delta_ln_b', 'delta_a_w_in', 'delta_a_w_dw', 'delta_a_b_dw', 'delta_a_norm_g', 'delta_a_norm_b', 'delta_a_w_out', 'delta_b_w_in', 'delta_b_q_norm_g', 'delta_b_w_qb', 'delta_b_w_out', 'delta_kv_w_a', 'delta_kv_norm_g', 'delta_kv_w_b', 'new_m_w_ada', 'new_m_b_ada', 'new_m_ln_g', 'new_m_ln_b', 'new_m_a_w_in', 'new_m_a_w_dw', 'new_m_a_b_dw', 'new_m_a_norm_g', 'new_m_a_norm_b', 'new_m_a_w_out', 'new_m_b_w_in', 'new_m_b_q_norm_g', 'new_m_b_w_qb', 'new_m_b_w_out', 'new_m_kv_w_a', 'new_m_kv_norm_g', 'new_m_kv_w_b', 'new_v_w_ada', 'new_v_b_ada', 'new_v_ln_g', 'new_v_ln_b', 'new_v_a_w_in', 'new_v_a_w_dw', 'new_v_a_b_dw', 'new_v_a_norm_g', 'new_v_a_norm_b', 'new_v_a_w_out', 'new_v_b_w_in', 'new_v_b_q_norm_g', 'new_v_b_w_qb', 'new_v_b_w_out', 'new_v_kv_w_a', 'new_v_kv_norm_g', 'new_v_kv_w_b']
TWIN_LEAF_KINDS = {'loss': 'loss', 'grad_x': 'grad_x', 'grad_w_ada': 'grad_w', 'grad_b_ada': 'grad_w', 'grad_ln_g': 'grad_w', 'grad_ln_b': 'grad_w', 'grad_a_w_in': 'grad_w', 'grad_a_w_dw': 'grad_w', 'grad_a_b_dw': 'grad_w', 'grad_a_norm_g': 'grad_w', 'grad_a_norm_b': 'grad_w', 'grad_a_w_out': 'grad_w', 'grad_b_w_in': 'grad_w', 'grad_b_q_norm_g': 'grad_w', 'grad_b_w_qb': 'grad_w', 'grad_b_w_out': 'grad_w', 'grad_kv_w_a': 'grad_w', 'grad_kv_norm_g': 'grad_w', 'grad_kv_w_b': 'grad_w', 'delta_w_ada': 'delta_w', 'delta_b_ada': 'delta_w', 'delta_ln_g': 'delta_w', 'delta_ln_b': 'delta_w', 'delta_a_w_in': 'delta_w', 'delta_a_w_dw': 'delta_w', 'delta_a_b_dw': 'delta_w', 'delta_a_norm_g': 'delta_w', 'delta_a_norm_b': 'delta_w', 'delta_a_w_out': 'delta_w', 'delta_b_w_in': 'delta_w', 'delta_b_q_norm_g': 'delta_w', 'delta_b_w_qb': 'delta_w', 'delta_b_w_out': 'delta_w', 'delta_kv_w_a': 'delta_w', 'delta_kv_norm_g': 'delta_w', 'delta_kv_w_b': 'delta_w', 'new_m_w_ada': 'new_m', 'new_m_b_ada': 'new_m', 'new_m_ln_g': 'new_m', 'new_m_ln_b': 'new_m', 'new_m_a_w_in': 'new_m', 'new_m_a_w_dw': 'new_m', 'new_m_a_b_dw': 'new_m', 'new_m_a_norm_g': 'new_m', 'new_m_a_norm_b': 'new_m', 'new_m_a_w_out': 'new_m', 'new_m_b_w_in': 'new_m', 'new_m_b_q_norm_g': 'new_m', 'new_m_b_w_qb': 'new_m', 'new_m_b_w_out': 'new_m', 'new_m_kv_w_a': 'new_m', 'new_m_kv_norm_g': 'new_m', 'new_m_kv_w_b': 'new_m', 'new_v_w_ada': 'new_v', 'new_v_b_ada': 'new_v', 'new_v_ln_g': 'new_v', 'new_v_ln_b': 'new_v', 'new_v_a_w_in': 'new_v', 'new_v_a_w_dw': 'new_v', 'new_v_a_b_dw': 'new_v', 'new_v_a_norm_g': 'new_v', 'new_v_a_norm_b': 'new_v', 'new_v_a_w_out': 'new_v', 'new_v_b_w_in': 'new_v', 'new_v_b_q_norm_g': 'new_v', 'new_v_b_w_qb': 'new_v', 'new_v_b_w_out': 'new_v', 'new_v_kv_w_a': 'new_v', 'new_v_kv_norm_g': 'new_v', 'new_v_kv_w_b': 'new_v'}


def _forward(args):
    return _fwd_reference(*[args[k] for k in FWD_PARAMS])


def _output_shape():
    out = _jax.eval_shape(lambda: _forward(_fwd_setup_inputs(0)))
    return out.shape, out.dtype

N_MICROBATCH = 1
ADAM_LR = 0.001
ADAM_B1 = 0.9
ADAM_B2 = 0.999
ADAM_EPS = 1e-08
ADAM_WD = 0.01
ADAM_STEP = 10
PER_EXAMPLE_BATCH_AXIS = {'x': 0, 'c': 0, 'loss_target': 0}
SHARED_INPUTS = []
_WEIGHT_DTYPES = {'w_ada': _jnp.float32, 'b_ada': _jnp.float32, 'ln_g': _jnp.float32, 'ln_b': _jnp.float32, 'a_w_in': _jnp.float32, 'a_w_dw': _jnp.float32, 'a_b_dw': _jnp.float32, 'a_norm_g': _jnp.float32, 'a_norm_b': _jnp.float32, 'a_w_out': _jnp.float32, 'b_w_in': _jnp.float32, 'b_q_norm_g': _jnp.float32, 'b_w_qb': _jnp.float32, 'b_w_out': _jnp.float32, 'kv_w_a': _jnp.float32, 'kv_norm_g': _jnp.float32, 'kv_w_b': _jnp.float32}
MOMENT_SCALE = {'w_ada': 4.383580e-03, 'b_ada': 7.676342e-03, 'ln_g': 5.656516e+00, 'ln_b': 2.235185e-01, 'a_w_in': 8.100263e-03, 'a_w_dw': 9.526266e-03, 'a_b_dw': 1.492400e-02, 'a_norm_g': 1.212109e-02, 'a_norm_b': 1.081340e-02, 'a_w_out': 1.901369e-02, 'b_w_in': 2.851411e-03, 'b_q_norm_g': 5.336029e-03, 'b_w_qb': 1.506003e-03, 'b_w_out': 6.576661e-03, 'kv_w_a': 1.104195e-02, 'kv_norm_g': 1.144053e-02, 'kv_w_b': 1.968445e-03}


def _to_microbatches(a, axis):
    t = _jnp.moveaxis(a, axis, 0)
    t = t.reshape((N_MICROBATCH, t.shape[0] // N_MICROBATCH) + t.shape[1:])
    return _jnp.moveaxis(t, 1, axis + 1)


def setup_inputs(seed: int = 0) -> dict:
    inp = _fwd_setup_inputs(seed)
    key = _jax.random.fold_in(_jax.random.key(seed), 7919)
    shape, _ = _output_shape()
    out = dict(inp)
    out["loss_target"] = _jax.random.normal(_jax.random.fold_in(key, 0), shape, _jnp.float32)
    for i, name in enumerate(TWIN_WEIGHTS):
        w = inp[name].astype(_jnp.float32)
        if MOMENT_SCALE is None:
            s = _jnp.sqrt(_jnp.mean(_jnp.square(w)) + 1e-30)
        else:
            s = MOMENT_SCALE[name]
        km, kv = _jax.random.split(_jax.random.fold_in(key, i + 1))
        out[name] = w
        out["m_" + name] = s * _jax.random.normal(km, w.shape, _jnp.float32)
        out["v_" + name] = (s * s) * _jax.random.uniform(kv, w.shape, _jnp.float32, 0.5, 1.5)
    if N_MICROBATCH > 1:
        for name, axis in PER_EXAMPLE_BATCH_AXIS.items():
            out[name] = _to_microbatches(out[name], axis)
    return {'x': out['x'], 'c': out['c'], 'w_ada': out['w_ada'], 'b_ada': out['b_ada'], 'ln_g': out['ln_g'], 'ln_b': out['ln_b'], 'a_w_in': out['a_w_in'], 'a_w_dw': out['a_w_dw'], 'a_b_dw': out['a_b_dw'], 'a_norm_g': out['a_norm_g'], 'a_norm_b': out['a_norm_b'], 'a_w_out': out['a_w_out'], 'b_w_in': out['b_w_in'], 'b_q_norm_g': out['b_q_norm_g'], 'b_w_qb': out['b_w_qb'], 'b_w_out': out['b_w_out'], 'kv_w_a': out['kv_w_a'], 'kv_norm_g': out['kv_norm_g'], 'kv_w_b': out['kv_w_b'], 'loss_target': out['loss_target'], 'm_w_ada': out['m_w_ada'], 'm_b_ada': out['m_b_ada'], 'm_ln_g': out['m_ln_g'], 'm_ln_b': out['m_ln_b'], 'm_a_w_in': out['m_a_w_in'], 'm_a_w_dw': out['m_a_w_dw'], 'm_a_b_dw': out['m_a_b_dw'], 'm_a_norm_g': out['m_a_norm_g'], 'm_a_norm_b': out['m_a_norm_b'], 'm_a_w_out': out['m_a_w_out'], 'm_b_w_in': out['m_b_w_in'], 'm_b_q_norm_g': out['m_b_q_norm_g'], 'm_b_w_qb': out['m_b_w_qb'], 'm_b_w_out': out['m_b_w_out'], 'm_kv_w_a': out['m_kv_w_a'], 'm_kv_norm_g': out['m_kv_norm_g'], 'm_kv_w_b': out['m_kv_w_b'], 'v_w_ada': out['v_w_ada'], 'v_b_ada': out['v_b_ada'], 'v_ln_g': out['v_ln_g'], 'v_ln_b': out['v_ln_b'], 'v_a_w_in': out['v_a_w_in'], 'v_a_w_dw': out['v_a_w_dw'], 'v_a_b_dw': out['v_a_b_dw'], 'v_a_norm_g': out['v_a_norm_g'], 'v_a_norm_b': out['v_a_norm_b'], 'v_a_w_out': out['v_a_w_out'], 'v_b_w_in': out['v_b_w_in'], 'v_b_q_norm_g': out['v_b_q_norm_g'], 'v_b_w_qb': out['v_b_w_qb'], 'v_b_w_out': out['v_b_w_out'], 'v_kv_w_a': out['v_kv_w_a'], 'v_kv_norm_g': out['v_kv_norm_g'], 'v_kv_w_b': out['v_kv_w_b']}


def _loss(weights, diff, rest, loss_target):
    with _jax.named_scope("forward"):
        args = {**rest, TWIN_DIFF_INPUT: diff, **{k: w.astype(_WEIGHT_DTYPES[k]) for k, w in weights.items()}}
        y = _forward(args)
    with _jax.named_scope("loss_head"):
        err = _jnp.square(y.astype(_jnp.float32) - loss_target)
        return 0.5 * _jnp.sum(_jnp.mean(err, axis=-1)) if err.ndim else 0.5 * err


def _adamw(w, g, m, v):
    m = ADAM_B1 * m + (1.0 - ADAM_B1) * g
    v = ADAM_B2 * v + (1.0 - ADAM_B2) * _jnp.square(g)
    m_hat = m / (1.0 - ADAM_B1 ** ADAM_STEP)
    v_hat = v / (1.0 - ADAM_B2 ** ADAM_STEP)
    delta = -ADAM_LR * (m_hat / (_jnp.sqrt(v_hat) + ADAM_EPS) + ADAM_WD * w)
    return delta, m, v


def reference(x, c, w_ada, b_ada, ln_g, ln_b, a_w_in, a_w_dw, a_b_dw, a_norm_g, a_norm_b, a_w_out, b_w_in, b_q_norm_g, b_w_qb, b_w_out, kv_w_a, kv_norm_g, kv_w_b, loss_target, m_w_ada, m_b_ada, m_ln_g, m_ln_b, m_a_w_in, m_a_w_dw, m_a_b_dw, m_a_norm_g, m_a_norm_b, m_a_w_out, m_b_w_in, m_b_q_norm_g, m_b_w_qb, m_b_w_out, m_kv_w_a, m_kv_norm_g, m_kv_w_b, v_w_ada, v_b_ada, v_ln_g, v_ln_b, v_a_w_in, v_a_w_dw, v_a_b_dw, v_a_norm_g, v_a_norm_b, v_a_w_out, v_b_w_in, v_b_q_norm_g, v_b_w_qb, v_b_w_out, v_kv_w_a, v_kv_norm_g, v_kv_w_b):
    given = dict(x=x, c=c, w_ada=w_ada, b_ada=b_ada, ln_g=ln_g, ln_b=ln_b, a_w_in=a_w_in, a_w_dw=a_w_dw, a_b_dw=a_b_dw, a_norm_g=a_norm_g, a_norm_b=a_norm_b, a_w_out=a_w_out, b_w_in=b_w_in, b_q_norm_g=b_q_norm_g, b_w_qb=b_w_qb, b_w_out=b_w_out, kv_w_a=kv_w_a, kv_norm_g=kv_norm_g, kv_w_b=kv_w_b, loss_target=loss_target, m_w_ada=m_w_ada, m_b_ada=m_b_ada, m_ln_g=m_ln_g, m_ln_b=m_ln_b, m_a_w_in=m_a_w_in, m_a_w_dw=m_a_w_dw, m_a_b_dw=m_a_b_dw, m_a_norm_g=m_a_norm_g, m_a_norm_b=m_a_norm_b, m_a_w_out=m_a_w_out, m_b_w_in=m_b_w_in, m_b_q_norm_g=m_b_q_norm_g, m_b_w_qb=m_b_w_qb, m_b_w_out=m_b_w_out, m_kv_w_a=m_kv_w_a, m_kv_norm_g=m_kv_norm_g, m_kv_w_b=m_kv_w_b, v_w_ada=v_w_ada, v_b_ada=v_b_ada, v_ln_g=v_ln_g, v_ln_b=v_ln_b, v_a_w_in=v_a_w_in, v_a_w_dw=v_a_w_dw, v_a_b_dw=v_a_b_dw, v_a_norm_g=v_a_norm_g, v_a_norm_b=v_a_norm_b, v_a_w_out=v_a_w_out, v_b_w_in=v_b_w_in, v_b_q_norm_g=v_b_q_norm_g, v_b_w_qb=v_b_w_qb, v_b_w_out=v_b_w_out, v_kv_w_a=v_kv_w_a, v_kv_norm_g=v_kv_norm_g, v_kv_w_b=v_kv_w_b)
    weights = {n: given[n] for n in TWIN_WEIGHTS}
    shared = {n: given[n] for n in SHARED_INPUTS}
    per_example = {n: given[n] for n in ['x', 'c']}
    grad_fn = _jax.value_and_grad(_loss, argnums=(0, 1))

    def one_microbatch(ex, loss_target):
        ex = dict(ex)
        diff = ex.pop(TWIN_DIFF_INPUT)
        return grad_fn(weights, diff, {**shared, **ex}, loss_target)

    if N_MICROBATCH == 1:
        loss, (grad_w, grad_x) = one_microbatch(per_example, given["loss_target"])
    else:
        def body(carry, xs):
            loss_sum, grad_sum = carry
            l_k, (gw_k, gx_k) = one_microbatch(xs[0], xs[1])
            with _jax.named_scope("update"):
                return (loss_sum + l_k, _jax.tree.map(_jnp.add, grad_sum, gw_k)), gx_k

        init = (_jnp.zeros((), _jnp.float32), _jax.tree.map(_jnp.zeros_like, weights))
        (loss, grad_w), grad_x = _jax.lax.scan(body, init, (per_example, given["loss_target"]))
    with _jax.named_scope("update"):
        delta_w, new_m, new_v = {}, {}, {}
        for n in TWIN_WEIGHTS:
            delta_w[n], new_m[n], new_v[n] = _adamw(weights[n], grad_w[n], given["m_" + n], given["v_" + n])
    return (loss, grad_x, *[grad_w[n] for n in TWIN_WEIGHTS], *[delta_w[n] for n in TWIN_WEIGHTS],
            *[new_m[n] for n in TWIN_WEIGHTS], *[new_v[n] for n in TWIN_WEIGHTS])
```

```python
import functools
import math

import numpy as np
import jax
import jax.numpy as jnp
from jax import lax
from jax.experimental import pallas as pl
from jax.experimental.pallas import tpu as pltpu

F32 = jnp.float32
BF16 = jnp.bfloat16
MESH = pl.DeviceIdType.MESH
ANY = pl.BlockSpec(memory_space=pl.ANY)

N_DEV = 8
N_CHIP = 4
V7X_VMEM_BYTES = 64 * 1024 * 1024
VMEM_LIMIT = V7X_VMEM_BYTES - 8 * 1024 * 1024
LANES = 128
SUBLANES = 8

DEPTH = 2
CONV_KERNEL = 31
NOPE = 128
ROPE = 64
VDIM = 128
QPAD = NOPE + LANES
ROPE_BASE = 10000.0
LN_EPS = 1e-5
RMS_EPS = 1e-6
ALPHA = (2.0 * DEPTH) ** 0.25
ATTN_SCALE = (NOPE + ROPE) ** -0.5
ADAM_LR = 0.001
ADAM_B1 = 0.9
ADAM_B2 = 0.999
ADAM_EPS = 1e-08
ADAM_WD = 0.01
ADAM_STEP = 10

CONV_TOP = 32
CONV_ROWS = 64
ATTN_BLOCK = 512
TABLE_TAIL = 16
ADAMW_TILE_ELEMS = 256 * 1024


def _pick(dim, prefs):
    for p in prefs:
        if dim % p == 0:
            return p
    return dim


def _params(sem, **kw):
    return pltpu.CompilerParams(dimension_semantics=sem, vmem_limit_bytes=VMEM_LIMIT, **kw)


_DIMS = {"nn": (((1,), (0,)), ((), ())), "nt": (((1,), (1,)), ((), ())), "tn": (((0,), (0,)), ((), ()))}


def _mm(a, b, mode, out_dtype, name, *, b_slots=False, out_slots=False):
    if mode == "tn":
        K, M = a.shape
    else:
        M, K = a.shape
    if b_slots:
        assert mode in ("nn", "nt")
        if mode == "nn":
            _, kb, n_loc = b.shape
            N, slot_w = N_DEV * n_loc, n_loc
        else:
            _, N, k_loc = b.shape
            kb, slot_w = N_DEV * k_loc, k_loc
    else:
        if mode == "nt":
            N, kb = b.shape
        else:
            kb, N = b.shape
        slot_w = N // N_DEV if out_slots else None
    assert kb == K, (name, a.shape, b.shape)
    tm = _pick(M, (1024, 512, 256, 128, 64, 32, 16))
    n_unit = slot_w if (mode == "nn" and b_slots) or out_slots else N
    k_unit = slot_w if (mode == "nt" and b_slots) else K
    tn = _pick(n_unit, (1024, 1152, 768, 640, 512, 384, 256, 128))
    tk = _pick(k_unit, (512, 640, 384, 256, 128, 64, 32, 16))
    nk = K // tk

    if mode == "tn":
        a_spec = pl.BlockSpec((tk, tm), lambda i, j, k: (k, i))
    else:
        a_spec = pl.BlockSpec((tm, tk), lambda i, j, k: (i, k))
    if b_slots and mode == "nn":
        per = slot_w // tn
        b_spec = pl.BlockSpec((None, tk, tn), lambda i, j, k: (j // per, k, j % per))
    elif b_slots and mode == "nt":
        per = slot_w // tk
        b_spec = pl.BlockSpec((None, tn, tk), lambda i, j, k: (k // per, j, k % per))
    elif mode == "nt":
        b_spec = pl.BlockSpec((tn, tk), lambda i, j, k: (j, k))
    else:
        b_spec = pl.BlockSpec((tk, tn), lambda i, j, k: (k, j))
    if out_slots:
        per_o = slot_w // tn
        o_spec = pl.BlockSpec((None, tm, tn), lambda i, j, k: (j // per_o, i, j % per_o))
        o_shape = jax.ShapeDtypeStruct((N_DEV, M, slot_w), out_dtype)
    else:
        o_spec = pl.BlockSpec((tm, tn), lambda i, j, k: (i, j))
        o_shape = jax.ShapeDtypeStruct((M, N), out_dtype)
    dims = _DIMS[mode]

    def body(a_ref, b_ref, o_ref, acc_ref):
        k = pl.program_id(2)

        @pl.when(k == 0)
        def _():
            acc_ref[...] = jnp.zeros_like(acc_ref)

        acc_ref[...] += lax.dot_general(a_ref[...].astype(BF16), b_ref[...].astype(BF16), dims,
                                        preferred_element_type=F32)

        @pl.when(k == nk - 1)
        def _():
            o_ref[...] = acc_ref[...].astype(out_dtype)

    return pl.pallas_call(
        body, name=name, grid=(M // tm, N // tn, nk), in_specs=[a_spec, b_spec], out_specs=o_spec,
        out_shape=o_shape, scratch_shapes=[pltpu.VMEM((tm, tn), F32)],
        compiler_params=_params(("parallel", "parallel", "arbitrary")),
    )(a, b)


def _rowwise(fn, rows, vecs, outs, accs, *, n_rows, tr, name, nc=1):
    nr = n_rows // tr
    n_in, n_vec, n_out, n_acc = len(rows), len(vecs), len(outs), len(accs)

    def cmap(cb0, follow):
        if follow:
            return lambda j, i: (i, cb0 + j)
        return lambda j, i: (i, cb0)

    def vmap_(cb0, follow):
        if follow:
            return lambda j, i: (0, cb0 + j)
        return lambda j, i: (0, cb0)

    in_specs = [pl.BlockSpec((tr, w), cmap(cb0, fol)) for (_, w, cb0, fol) in rows]
    in_specs += [pl.BlockSpec((1, w), vmap_(cb0, fol)) for (_, w, cb0, fol) in vecs]
    out_specs = [pl.BlockSpec((tr, bw), lambda j, i: (i, j)) for (_, _, bw) in outs]
    out_specs += [pl.BlockSpec((1, w), lambda j, i: (0, j)) for w in accs]
    out_shape = [jax.ShapeDtypeStruct((n_rows, tw), dt) for (dt, tw, _) in outs]
    out_shape += [jax.ShapeDtypeStruct((1, w * nc), F32) for w in accs]

    def body(*refs):
        ins = [r[...] for r in refs[:n_in]]
        vs = [r[...] for r in refs[n_in:n_in + n_vec]]
        o_refs = refs[n_in + n_vec:n_in + n_vec + n_out]
        a_refs = refs[n_in + n_vec + n_out:]
        o_vals, a_vals = fn(ins, vs)
        for r, v in zip(o_refs, o_vals):
            r[...] = v.astype(r.dtype)
        if n_acc:
            i = pl.program_id(1)

            @pl.when(i == 0)
            def _():
                for r in a_refs:
                    r[...] = jnp.zeros_like(r)

            for r, v in zip(a_refs, a_vals):
                r[...] += v

    res = pl.pallas_call(
        body, name=name, grid=(nc, nr), in_specs=in_specs, out_specs=out_specs, out_shape=out_shape,
        compiler_params=_params(("parallel", "arbitrary")),
    )(*[r[0] for r in rows], *[v[0] for v in vecs])
    return res[:n_out], res[n_out:]


def _colsum(x):
    return jnp.sum(x, axis=0, keepdims=True)


def _ln_stats(x):
    mu = jnp.mean(x, axis=-1, keepdims=True)
    xc = x - mu
    var = jnp.mean(xc * xc, axis=-1, keepdims=True)
    rstd = lax.rsqrt(var + LN_EPS)
    return xc * rstd, rstd


def _ln_bwd(dxhat, xhat, rstd):
    return rstd * (dxhat - jnp.mean(dxhat, axis=-1, keepdims=True)
                   - xhat * jnp.mean(dxhat * xhat, axis=-1, keepdims=True))


def _rms_stats(x):
    r = lax.rsqrt(jnp.mean(x * x, axis=-1, keepdims=True) + RMS_EPS)
    return x * r, r


def _rms_bwd(dxhat, xhat, r):
    return r * (dxhat - xhat * jnp.mean(dxhat * xhat, axis=-1, keepdims=True))


def _silu(x):
    return x * jax.nn.sigmoid(x)


def _dsilu(x):
    s = jax.nn.sigmoid(x)
    return s * (1.0 + x * (1.0 - s))


def _rope_fwd(x, cos_t, sin_a, sin_b):
    return x * cos_t + pltpu.roll(x, LANES - ROPE // 2, 1) * sin_a + pltpu.roll(x, ROPE // 2, 1) * sin_b


def _rope_bwd(dy, cos_t, sin_a, sin_b):
    return dy * cos_t + pltpu.roll(dy * sin_a, ROPE // 2, 1) + pltpu.roll(dy * sin_b, LANES - ROPE // 2, 1)


def _rope_tables(seq):
    half = ROPE // 2
    inv_freq = ROPE_BASE ** (-jnp.arange(half, dtype=F32) / half)
    ang = jnp.arange(seq, dtype=jnp.int32).astype(F32)[:, None] * inv_freq[None, :]
    cos, sin = jnp.cos(ang), jnp.sin(ang)
    z = jnp.zeros_like(cos)
    cos_t = jnp.concatenate([cos, cos, z, z], axis=1)
    sin_a = jnp.concatenate([-sin, z, z, z], axis=1)
    sin_b = jnp.concatenate([z, sin, z, z], axis=1)
    return cos_t, sin_a, sin_b


def _shifted(win, n):
    return [win] + [pltpu.roll(win, n - j, 0) for j in range(1, SUBLANES)]


def _conv_fwd(u0, w_dw, b_dw, seq, width, name):
    tc = LANES
    nct = width // tc
    rows = CONV_ROWS
    n_win = rows + CONV_TOP

    def body(a_ref, g_ref, w_ref, b_ref, o_ref, vpad):
        vpad[0:CONV_TOP, :] = jnp.zeros((CONV_TOP, tc), F32)
        vpad[CONV_TOP:, :] = a_ref[...] * jax.nn.sigmoid(g_ref[...])

        def chunk(i, carry):
            t0 = pl.multiple_of(i * rows, rows)
            rot = _shifted(vpad[pl.ds(t0, n_win), :], n_win)
            acc = jnp.zeros((rows, tc), F32) + b_ref[...]
            for k in range(CONV_KERNEL):
                s = k + CONV_TOP - (CONV_KERNEL - 1)
                q, j = divmod(s, SUBLANES)
                acc = acc + w_ref[k:k + 1, :] * rot[j][SUBLANES * q:SUBLANES * q + rows]
            o_ref[pl.ds(t0, rows), :] = acc
            return carry

        lax.fori_loop(0, seq // rows, chunk, 0)

    return pl.pallas_call(
        body, name=name, grid=(nct,),
        in_specs=[pl.BlockSpec((seq, tc), lambda j: (0, j)), pl.BlockSpec((seq, tc), lambda j: (0, nct + j)),
                  pl.BlockSpec((CONV_KERNEL, tc), lambda j: (0, j)), pl.BlockSpec((1, tc), lambda j: (0, j))],
        out_specs=pl.BlockSpec((seq, tc), lambda j: (0, j)),
        out_shape=jax.ShapeDtypeStruct((seq, width), F32),
        scratch_shapes=[pltpu.VMEM((seq + CONV_TOP, tc), F32)],
        compiler_params=_params(("parallel",)),
    )(u0, u0, w_dw, b_dw)


def _conv_bwd(u0, dv2, w_dw, seq, width, name):
    tc = LANES
    nct = width // tc
    rows = CONV_ROWS
    n_win = rows + CONV_TOP
    nsh = CONV_KERNEL * SUBLANES

    def body(a_ref, g_ref, d_ref, w_ref, da_ref, dg_ref, dw_ref, vpad, dpad, dwacc):
        vpad[0:CONV_TOP, :] = jnp.zeros((CONV_TOP, tc), F32)
        vpad[CONV_TOP:, :] = a_ref[...] * jax.nn.sigmoid(g_ref[...])
        dpad[0:seq, :] = d_ref[...]
        dpad[seq:, :] = jnp.zeros((CONV_TOP, tc), F32)
        dwacc[...] = jnp.zeros_like(dwacc)

        def chunk(i, carry):
            t0 = pl.multiple_of(i * rows, rows)
            vrot = _shifted(vpad[pl.ds(t0, n_win), :], n_win)
            drot = _shifted(dpad[pl.ds(t0, n_win), :], n_win)
            dcur = drot[0][0:rows]
            dv1 = jnp.zeros((rows, tc), F32)
            for k in range(CONV_KERNEL):
                q, j = divmod(CONV_KERNEL - 1 - k, SUBLANES)
                dv1 = dv1 + w_ref[k:k + 1, :] * drot[j][SUBLANES * q:SUBLANES * q + rows]
                q, j = divmod(k + CONV_TOP - (CONV_KERNEL - 1), SUBLANES)
                prod = dcur * vrot[j][SUBLANES * q:SUBLANES * q + rows]
                part = prod[0:SUBLANES]
                for r in range(1, rows // SUBLANES):
                    part = part + prod[SUBLANES * r:SUBLANES * (r + 1)]
                dwacc[SUBLANES * k:SUBLANES * (k + 1), :] += part
            a = a_ref[pl.ds(t0, rows), :]
            sg = jax.nn.sigmoid(g_ref[pl.ds(t0, rows), :])
            da_ref[pl.ds(t0, rows), :] = (dv1 * sg).astype(BF16)
            dg_ref[pl.ds(t0, rows), :] = (dv1 * a * sg * (1.0 - sg)).astype(BF16)
            return carry

        lax.fori_loop(0, seq // rows, chunk, 0)
        for k in range(CONV_KERNEL):
            dw_ref[k:k + 1, :] = jnp.sum(dwacc[SUBLANES * k:SUBLANES * (k + 1), :], axis=0, keepdims=True)

    return pl.pallas_call(
        body, name=name, grid=(nct,),
        in_specs=[pl.BlockSpec((seq, tc), lambda j: (0, j)), pl.BlockSpec((seq, tc), lambda j: (0, nct + j)),
                  pl.BlockSpec((seq, tc), lambda j: (0, j)), pl.BlockSpec((CONV_KERNEL, tc), lambda j: (0, j))],
        out_specs=[pl.BlockSpec((seq, tc), lambda j: (0, j)), pl.BlockSpec((seq, tc), lambda j: (0, j)),
                   pl.BlockSpec((CONV_KERNEL, tc), lambda j: (0, j))],
        out_shape=[jax.ShapeDtypeStruct((seq, width), BF16), jax.ShapeDtypeStruct((seq, width), BF16),
                   jax.ShapeDtypeStruct((CONV_KERNEL, width), F32)],
        scratch_shapes=[pltpu.VMEM((seq + CONV_TOP, tc), F32), pltpu.VMEM((seq + CONV_TOP, tc), F32),
                        pltpu.VMEM((nsh, tc), F32)],
        compiler_params=_params(("parallel",)),
    )(u0, u0, dv2, w_dw)


def _tri_tables(nb):
    qi = [q for q in range(nb) for _ in range(q + 1)]
    ki = [k for q in range(nb) for k in range(q + 1)]
    return np.asarray(qi + [qi[-1]] * TABLE_TAIL, np.int32), np.asarray(ki + [ki[-1]] * TABLE_TAIL, np.int32)


def _causal_mask(t, transposed=False):
    r = lax.broadcasted_iota(jnp.int32, (t, t), 0)
    c = lax.broadcasted_iota(jnp.int32, (t, t), 1)
    return (c >= r) if transposed else (r >= c)


def _attn_fwd(q, kv, kr, seq, heads, name):
    t = min(ATTN_BLOCK, seq)
    nb = seq // t
    qi_np, ki_np = _tri_tables(nb)

    def body(qi_ref, ki_ref, q_ref, kv_ref, kr_ref, o_ref, lse_ref, lset_ref, m_s, l_s, acc_s):
        step = pl.program_id(1)
        qi, ki = qi_ref[step], ki_ref[step]

        @pl.when(ki == 0)
        def _():
            m_s[...] = jnp.full_like(m_s, -jnp.inf)
            l_s[...] = jnp.zeros_like(l_s)
            acc_s[...] = jnp.zeros_like(acc_s)

        kvb = kv_ref[...]
        kf = jnp.concatenate([kvb[:, :NOPE], kr_ref[...]], axis=1)
        s = lax.dot_general(q_ref[...], kf, _DIMS["nt"], preferred_element_type=F32) * ATTN_SCALE
        s = jnp.where(jnp.logical_or(ki < qi, _causal_mask(t)), s, -jnp.inf)
        m_prev = m_s[...]
        m_new = jnp.maximum(m_prev, jnp.max(s, axis=1, keepdims=True))
        p = jnp.exp(s - m_new[:, :1])
        corr = jnp.exp(m_prev - m_new)
        l_s[...] = corr * l_s[...] + jnp.sum(p, axis=1, keepdims=True)
        acc_s[...] = corr * acc_s[...] + jnp.dot(p.astype(BF16), kvb[:, NOPE:], preferred_element_type=F32)
        m_s[...] = m_new

        @pl.when(ki == qi)
        def _():
            o_ref[...] = acc_s[...] / l_s[...]
            lse = m_s[...] + jnp.log(l_s[...])
            lse_ref[...] = lse
            lset_ref[...] = jnp.transpose(lse)[0:SUBLANES, :]

    grid_spec = pltpu.PrefetchScalarGridSpec(
        num_scalar_prefetch=2, grid=(heads, nb * (nb + 1) // 2),
        in_specs=[pl.BlockSpec((t, QPAD), lambda h, s, qi, ki: (qi[s], h)),
                  pl.BlockSpec((t, QPAD), lambda h, s, qi, ki: (ki[s], h)),
                  pl.BlockSpec((t, LANES), lambda h, s, qi, ki: (ki[s], 0))],
        out_specs=[pl.BlockSpec((t, VDIM), lambda h, s, qi, ki: (qi[s], h)),
                   pl.BlockSpec((t, LANES), lambda h, s, qi, ki: (qi[s], h)),
                   pl.BlockSpec((None, SUBLANES, t), lambda h, s, qi, ki: (h, 0, qi[s]))],
        scratch_shapes=[pltpu.VMEM((t, LANES), F32), pltpu.VMEM((t, LANES), F32), pltpu.VMEM((t, VDIM), F32)])
    return pl.pallas_call(
        body, name=name, grid_spec=grid_spec,
        out_shape=[jax.ShapeDtypeStruct((seq, heads * VDIM), F32), jax.ShapeDtypeStruct((seq, heads * LANES), F32),
                   jax.ShapeDtypeStruct((heads, SUBLANES, seq), F32)],
        compiler_params=_params(("parallel", "arbitrary")),
    )(jnp.asarray(qi_np), jnp.asarray(ki_np), q, kv, kr)


def _attn_delta(do, o, seq, heads, name):
    t = min(ATTN_BLOCK, seq)

    def body(do_ref, o_ref, d_ref, dt_ref):
        d = jnp.sum(do_ref[...].astype(F32) * o_ref[...], axis=1, keepdims=True)
        db = jnp.broadcast_to(d, (t, LANES))
        d_ref[...] = db
        dt_ref[...] = jnp.transpose(db)[0:SUBLANES, :]

    return pl.pallas_call(
        body, name=name, grid=(seq // t, heads),
        in_specs=[pl.BlockSpec((t, VDIM), lambda i, h: (i, h)), pl.BlockSpec((t, VDIM), lambda i, h: (i, h))],
        out_specs=[pl.BlockSpec((t, LANES), lambda i, h: (i, h)),
                   pl.BlockSpec((None, SUBLANES, t), lambda i, h: (h, 0, i))],
        out_shape=[jax.ShapeDtypeStruct((seq, heads * LANES), F32), jax.ShapeDtypeStruct((heads, SUBLANES, seq), F32)],
        compiler_params=_params(("parallel", "parallel")),
    )(do, o)


def _attn_dq(q, kv, kr, do, lse, delta, seq, heads, name):
    t = min(ATTN_BLOCK, seq)
    nb = seq // t
    qi_np, ki_np = _tri_tables(nb)

    def body(qi_ref, ki_ref, q_ref, kv_ref, kr_ref, do_ref, lse_ref, dl_ref, dq_ref, acc_s):
        step = pl.program_id(1)
        qi, ki = qi_ref[step], ki_ref[step]

        @pl.when(ki == 0)
        def _():
            acc_s[...] = jnp.zeros_like(acc_s)

        kvb = kv_ref[...]
        kf = jnp.concatenate([kvb[:, :NOPE], kr_ref[...]], axis=1)
        s = lax.dot_general(q_ref[...], kf, _DIMS["nt"], preferred_element_type=F32) * ATTN_SCALE
        p = jnp.exp(s - lse_ref[...][:, :1])
        p = jnp.where(jnp.logical_or(ki < qi, _causal_mask(t)), p, 0.0)
        dp = lax.dot_general(do_ref[...], kvb[:, NOPE:], _DIMS["nt"], preferred_element_type=F32)
        ds = p * (dp - dl_ref[...][:, :1]) * ATTN_SCALE
        acc_s[...] += jnp.dot(ds.astype(BF16), kf, preferred_element_type=F32)

        @pl.when(ki == qi)
        def _():
            dq_ref[...] = acc_s[...]

    grid_spec = pltpu.PrefetchScalarGridSpec(
        num_scalar_prefetch=2, grid=(heads, nb * (nb + 1) // 2),
        in_specs=[pl.BlockSpec((t, QPAD), lambda h, s, qi, ki: (qi[s], h)),
                  pl.BlockSpec((t, QPAD), lambda h, s, qi, ki: (ki[s], h)),
                  pl.BlockSpec((t, LANES), lambda h, s, qi, ki: (ki[s], 0)),
                  pl.BlockSpec((t, VDIM), lambda h, s, qi, ki: (qi[s], h)),
                  pl.BlockSpec((t, LANES), lambda h, s, qi, ki: (qi[s], h)),
                  pl.BlockSpec((t, LANES), lambda h, s, qi, ki: (qi[s], h))],
        out_specs=pl.BlockSpec((t, QPAD), lambda h, s, qi, ki: (qi[s], h)),
        scratch_shapes=[pltpu.VMEM((t, QPAD), F32)])
    return pl.pallas_call(
        body, name=name, grid_spec=grid_spec,
        out_shape=jax.ShapeDtypeStruct((seq, heads * QPAD), F32),
        compiler_params=_params(("parallel", "arbitrary")),
    )(jnp.asarray(qi_np), jnp.asarray(ki_np), q, kv, kr, do, lse, delta)


def _attn_dkv(q, kv, kr, do, lset, deltat, seq, heads, name):
    t = min(ATTN_BLOCK, seq)
    nb = seq // t
    ki_l, h_l, qi_l = [], [], []
    for k in range(nb):
        for h in range(heads):
            for qq in range(k, nb):
                ki_l.append(k)
                h_l.append(h)
                qi_l.append(qq)
    n_steps = len(ki_l)
    ki_np, h_np, qi_np = (np.asarray(v + [v[-1]] * TABLE_TAIL, np.int32) for v in (ki_l, h_l, qi_l))

    def body(ki_ref, h_ref, qi_ref, q_ref, kv_ref, kr_ref, do_ref, lset_ref, dlt_ref, dkv_ref, dkr_ref,
             dk_s, dv_s, dkr_s):
        step = pl.program_id(0)
        ki, h, qi = ki_ref[step], h_ref[step], qi_ref[step]

        @pl.when(qi == ki)
        def _():
            dk_s[...] = jnp.zeros_like(dk_s)
            dv_s[...] = jnp.zeros_like(dv_s)

        @pl.when(jnp.logical_and(qi == ki, h == 0))
        def _():
            dkr_s[...] = jnp.zeros_like(dkr_s)

        kvb = kv_ref[...]
        kf = jnp.concatenate([kvb[:, :NOPE], kr_ref[...]], axis=1)
        qb = q_ref[...]
        dob = do_ref[...]
        st = lax.dot_general(kf, qb, _DIMS["nt"], preferred_element_type=F32) * ATTN_SCALE
        pt = jnp.exp(st - lset_ref[0:1, :])
        pt = jnp.where(jnp.logical_or(ki < qi, _causal_mask(t, transposed=True)), pt, 0.0)
        dv_s[...] += jnp.dot(pt.astype(BF16), dob, preferred_element_type=F32)
        dpt = lax.dot_general(kvb[:, NOPE:], dob, _DIMS["nt"], preferred_element_type=F32)
        dst = pt * (dpt - dlt_ref[0:1, :]) * ATTN_SCALE
        dk_s[...] += jnp.dot(dst.astype(BF16), qb, preferred_element_type=F32)

        @pl.when(qi == nb - 1)
        def _():
            dk = dk_s[...]
            dkv_ref[...] = jnp.concatenate([dk[:, :NOPE], dv_s[...]], axis=1).astype(BF16)
            dkr_s[...] += dk[:, NOPE:]

        @pl.when(jnp.logical_and(qi == nb - 1, h == heads - 1))
        def _():
            dkr_ref[...] = dkr_s[...]

    grid_spec = pltpu.PrefetchScalarGridSpec(
        num_scalar_prefetch=3, grid=(n_steps,),
        in_specs=[pl.BlockSpec((t, QPAD), lambda s, ki, h, qi: (qi[s], h[s])),
                  pl.BlockSpec((t, QPAD), lambda s, ki, h, qi: (ki[s], h[s])),
                  pl.BlockSpec((t, LANES), lambda s, ki, h, qi: (ki[s], 0)),
                  pl.BlockSpec((t, VDIM), lambda s, ki, h, qi: (qi[s], h[s])),
                  pl.BlockSpec((None, SUBLANES, t), lambda s, ki, h, qi: (h[s], 0, qi[s])),
                  pl.BlockSpec((None, SUBLANES, t), lambda s, ki, h, qi: (h[s], 0, qi[s]))],
        out_specs=[pl.BlockSpec((t, QPAD), lambda s, ki, h, qi: (ki[s], h[s])),
                   pl.BlockSpec((t, LANES), lambda s, ki, h, qi: (ki[s], 0))],
        scratch_shapes=[pltpu.VMEM((t, QPAD), F32), pltpu.VMEM((t, VDIM), F32), pltpu.VMEM((t, LANES), F32)])
    return pl.pallas_call(
        body, name=name, grid_spec=grid_spec,
        out_shape=[jax.ShapeDtypeStruct((seq, heads * QPAD), BF16), jax.ShapeDtypeStruct((seq, LANES), F32)],
        compiler_params=_params(("arbitrary",)),
    )(jnp.asarray(ki_np), jnp.asarray(h_np), jnp.asarray(qi_np), q, kv, kr, do, lset, deltat)


def _place():
    x, y, c = lax.axis_index("x"), lax.axis_index("y"), lax.axis_index("c")
    return x, y, c


def _all_gather(shard, name):
    r, n = shard.shape

    def body(x_ref, out_ref, send_sems, recv_sems, local_sem):
        x, y, c = _place()
        me, sibling = (x, y, c), (x, y, 1 - c)
        chips = [(1 - x, y), (x, 1 - y), (1 - x, 1 - y)]

        def slot(px, py, pc):
            return out_ref.at[4 * px + 2 * py + pc]

        def copy(k, block, to, src=None):
            return pltpu.make_async_remote_copy(
                src_ref=slot(*block) if src is None else src, dst_ref=slot(*block),
                send_sem=send_sems.at[k], recv_sem=recv_sems.at[k], device_id=to, device_id_type=MESH)

        mine = pltpu.make_async_copy(x_ref, slot(*me), local_sem)
        mine.start()
        first = [copy(0, me, sibling, src=x_ref)]
        first += [copy(1 + j, me, (*chip, c), src=x_ref) for j, chip in enumerate(chips)]
        for cp in first:
            cp.start()
        passed = [copy(4 + j, (*chip, c), sibling) for j, chip in enumerate(chips)]
        for j, chip in enumerate(chips):
            copy(1 + j, (*chip, c), me).wait_recv()
            passed[j].start()
        copy(0, sibling, me).wait_recv()
        for j, chip in enumerate(chips):
            copy(4 + j, (*chip, 1 - c), me).wait_recv()
        for cp in first + passed:
            cp.wait_send()
        mine.wait()

    return pl.pallas_call(
        body, name=name, in_specs=[ANY], out_specs=ANY,
        out_shape=jax.ShapeDtypeStruct((N_DEV, r, n), shard.dtype),
        scratch_shapes=[pltpu.SemaphoreType.DMA((7,)), pltpu.SemaphoreType.DMA((7,)), pltpu.SemaphoreType.DMA],
        compiler_params=pltpu.CompilerParams(has_side_effects=True),
    )(shard)


def _pair_exchange(g8, name):
    _, r, n = g8.shape

    def body(g_ref, out_ref, send_sems, recv_sems):
        x, y, c = _place()
        sibling = (x, y, 1 - c)
        copies = [pltpu.make_async_remote_copy(
            src_ref=g_ref.at[2 * p + (1 - c)], dst_ref=out_ref.at[p], send_sem=send_sems.at[p],
            recv_sem=recv_sems.at[p], device_id=sibling, device_id_type=MESH) for p in range(N_CHIP)]
        for cp in copies:
            cp.start()
        for cp in copies:
            cp.wait_recv()
        for cp in copies:
            cp.wait_send()

    return pl.pallas_call(
        body, name=name, in_specs=[ANY], out_specs=ANY,
        out_shape=jax.ShapeDtypeStruct((N_CHIP, r, n), g8.dtype),
        scratch_shapes=[pltpu.SemaphoreType.DMA((N_CHIP,)), pltpu.SemaphoreType.DMA((N_CHIP,))],
        compiler_params=pltpu.CompilerParams(has_side_effects=True),
    )(g8)


def _pair_add(g8, got, core, name):
    _, r, n = g8.shape
    tr = _pick(r, (256, 128, 64, 32, 16, 8))

    def body(c_ref, g_ref, r_ref, o_ref):
        o_ref[...] = (g_ref[...].astype(F32) + r_ref[...].astype(F32)).astype(o_ref.dtype)

    grid_spec = pltpu.PrefetchScalarGridSpec(
        num_scalar_prefetch=1, grid=(N_CHIP, r // tr),
        in_specs=[pl.BlockSpec((None, tr, n), lambda p, i, c: (2 * p + c[0], i, 0)),
                  pl.BlockSpec((None, tr, n), lambda p, i, c: (p, i, 0))],
        out_specs=pl.BlockSpec((None, tr, n), lambda p, i, c: (p, i, 0)))
    return pl.pallas_call(
        body, name=name, grid_spec=grid_spec, out_shape=jax.ShapeDtypeStruct((N_CHIP, r, n), g8.dtype),
        compiler_params=_params(("parallel", "parallel")),
    )(core, g8, got)


def _chip_exchange(t4, name):
    _, r, n = t4.shape

    def body(t_ref, out_ref, send_sems, recv_sems, local_sem):
        x, y, c = _place()
        mine = 2 * x + y
        chips = [(1 - x, y), (x, 1 - y), (1 - x, 1 - y)]
        local = pltpu.make_async_copy(t_ref.at[mine], out_ref.at[mine], local_sem)
        local.start()
        copies = [pltpu.make_async_remote_copy(
            src_ref=t_ref.at[2 * px + py], dst_ref=out_ref.at[mine], send_sem=send_sems.at[k],
            recv_sem=recv_sems.at[k], device_id=(px, py, c), device_id_type=MESH)
            for k, (px, py) in enumerate(chips)]
        for cp in copies:
            cp.start()
        for k, (px, py) in enumerate(chips):
            pltpu.make_async_remote_copy(
                src_ref=t_ref.at[mine], dst_ref=out_ref.at[2 * px + py], send_sem=send_sems.at[k],
                recv_sem=recv_sems.at[k], device_id=(px, py, c), device_id_type=MESH).wait_recv()
        for cp in copies:
            cp.wait_send()
        local.wait()

    return pl.pallas_call(
        body, name=name, in_specs=[ANY], out_specs=ANY,
        out_shape=jax.ShapeDtypeStruct((N_CHIP, r, n), t4.dtype),
        scratch_shapes=[pltpu.SemaphoreType.DMA((3,)), pltpu.SemaphoreType.DMA((3,)), pltpu.SemaphoreType.DMA],
        compiler_params=pltpu.CompilerParams(has_side_effects=True),
    )(t4)


def _reduce_scatter(g8, core, name):
    got = _pair_exchange(g8, name + "_pair")
    t4 = _pair_add(g8, got, core, name + "_add")
    return _chip_exchange(t4, name + "_chip")


def _slot_sum(g, name):
    k, r, n = g.shape

    def body(g_ref, o_ref):
        acc = g_ref[0].astype(F32)
        for s in range(1, k):
            acc = acc + g_ref[s].astype(F32)
        o_ref[...] = acc

    return pl.pallas_call(body, name=name, out_shape=jax.ShapeDtypeStruct((r, n), F32),
                          compiler_params=pltpu.CompilerParams(vmem_limit_bytes=VMEM_LIMIT))(g)


def _pack_rows(parts, n_rows, width, name):
    starts = [r0 for _, r0 in parts]

    def body(*refs):
        o_ref = refs[-1]
        o_ref[...] = jnp.zeros_like(o_ref)
        for ref, r0 in zip(refs[:-1], starts):
            k, w = ref.shape
            o_ref[r0:r0 + k, 0:w] = ref[...]

    return pl.pallas_call(body, name=name, out_shape=jax.ShapeDtypeStruct((n_rows, width), F32),
                          compiler_params=pltpu.CompilerParams(vmem_limit_bytes=VMEM_LIMIT))(*[a for a, _ in parts])


def _adamw(w, gparts, m, v, name):
    r, n = w.shape
    k = gparts.shape[0]
    tr = _pick(r, [t for t in (512, 256, 128, 64, 32, 16) if t * n <= ADAMW_TILE_ELEMS])
    c1 = 1.0 - ADAM_B1 ** ADAM_STEP
    c2 = 1.0 - ADAM_B2 ** ADAM_STEP

    def body(w_ref, g_ref, m_ref, v_ref, go_ref, d_ref, mo_ref, vo_ref):
        g = g_ref[0].astype(F32)
        for s in range(1, k):
            g = g + g_ref[s].astype(F32)
        mn = ADAM_B1 * m_ref[...] + (1.0 - ADAM_B1) * g
        vn = ADAM_B2 * v_ref[...] + (1.0 - ADAM_B2) * (g * g)
        go_ref[...] = g
        mo_ref[...] = mn
        vo_ref[...] = vn
        d_ref[...] = -ADAM_LR * ((mn / c1) / (jnp.sqrt(vn / c2) + ADAM_EPS) + ADAM_WD * w_ref[...])

    spec = pl.BlockSpec((tr, n), lambda i: (i, 0))
    return pl.pallas_call(
        body, name=name, grid=(r // tr,),
        in_specs=[spec, pl.BlockSpec((k, tr, n), lambda i: (0, i, 0)), spec, spec],
        out_specs=[spec] * 4, out_shape=[jax.ShapeDtypeStruct((r, n), F32)] * 4,
        compiler_params=_params(("parallel",)),
    )(w, gparts, m, v)


def kernel(x, c, w_ada, b_ada, ln_g, ln_b, a_w_in, a_w_dw, a_b_dw, a_norm_g, a_norm_b, a_w_out, b_w_in, b_q_norm_g, b_w_qb, b_w_out, kv_w_a, kv_norm_g, kv_w_b, loss_target, m_w_ada, m_b_ada, m_ln_g, m_ln_b, m_a_w_in, m_a_w_dw, m_a_b_dw, m_a_norm_g, m_a_norm_b, m_a_w_out, m_b_w_in, m_b_q_norm_g, m_b_w_qb, m_b_w_out, m_kv_w_a, m_kv_norm_g, m_kv_w_b, v_w_ada, v_b_ada, v_ln_g, v_ln_b, v_a_w_in, v_a_w_dw, v_a_b_dw, v_a_norm_g, v_a_norm_b, v_a_w_out, v_b_w_in, v_b_q_norm_g, v_b_w_qb, v_b_w_out, v_kv_w_a, v_kv_norm_g, v_kv_w_b):
    S, D = x.shape[1], x.shape[2]
    C = a_w_out.shape[1] * N_DEV
    QL = b_q_norm_g.shape[1]
    KL = kv_norm_g.shape[0]
    H = kv_w_b.shape[1] * N_DEV // (NOPE + VDIM)
    HL = H // N_DEV
    W = H * VDIM
    assert a_w_dw.shape[1] == CONV_KERNEL and kv_w_a.shape[1] == KL + ROPE and b_w_qb.shape[2] == HL * (NOPE + ROPE)
    assert b_w_in.shape[2] * N_DEV == QL + W and D % (N_DEV * LANES) == 0 and S % CONV_ROWS == 0
    TW = 1024 if W % 1024 == 0 and QL % 1024 == 0 else LANES

    xi, yi, ci = _place()
    idx = 4 * xi + 2 * yi + ci
    core = jnp.reshape(ci, (1,)).astype(jnp.int32)
    x2, tgt = x[0], loss_target[0]
    cos_t, sin_a, sin_b = _rope_tables(S)
    rope_rows = [(cos_t, LANES, 0, False), (sin_a, LANES, 0, False), (sin_b, LANES, 0, False)]

    win0 = _all_gather(a_w_in[0].astype(BF16), "ag_a_w_in")
    wout0 = _all_gather(a_w_out[0].astype(BF16), "ag_a_w_out").reshape(C, D)
    wbin = _all_gather(b_w_in[0].astype(BF16), "ag_b_w_in")
    wqb_loc = jnp.pad(b_w_qb[0].reshape(QL, HL, NOPE + ROPE), ((0, 0), (0, 0), (0, QPAD - NOPE - ROPE)))
    wqb = _all_gather(wqb_loc.reshape(QL, HL * QPAD).astype(BF16), "ag_b_w_qb")
    wbout = _all_gather(b_w_out[0].astype(BF16), "ag_b_w_out").reshape(W, D)
    kva_w = KL + LANES
    wkva_loc = jnp.pad(kv_w_a, ((0, 0), (0, kva_w - KL - ROPE)))
    wkva = _all_gather(wkva_loc.astype(BF16), "ag_kv_w_a").reshape(D, kva_w)
    wkvb = _all_gather(kv_w_b.astype(BF16), "ag_kv_w_b")

    cl = C // N_DEV
    v0 = CONV_TOP
    small_loc = _pack_rows([(a_w_dw[0], 0), (a_b_dw, v0), (a_norm_g, v0 + 1), (a_norm_b, v0 + 2)], 40, cl, "pack_conv_small")
    small = _all_gather(small_loc, "ag_conv_small")
    small = jnp.transpose(small, (1, 0, 2)).reshape(40, C)
    w_dw, b_dw, g_cn, b_cn = small[:CONV_KERNEL], small[v0:v0 + 1], small[v0 + 1:v0 + 2], small[v0 + 2:v0 + 3]

    c_all = _all_gather(jnp.broadcast_to(c, (SUBLANES, D)), "ag_c")[:, 0, :]
    c_pad = jnp.concatenate([c_all, jnp.zeros((16 - N_DEV, D), F32)], axis=0)
    (sc,), _ = _rowwise(lambda r, v: ([_silu(r[0])], []), [(c_pad, D, 0, False)], [], [(BF16, D, D)], [],
                        n_rows=16, tr=16, name="silu_c")
    nl = 3 * D // N_DEV
    mod_loc = jnp.concatenate([_mm(sc, w_ada[l], "nn", F32, f"mod{l}") for l in range(DEPTH)], axis=0)
    mod_all = jnp.transpose(_all_gather(mod_loc, "ag_mod"), (1, 0, 2)).reshape(DEPTH, 16, 3 * D)
    mods = []
    for l in range(DEPTH):
        row = lax.dynamic_slice(mod_all[l], (idx, 0), (1, 3 * D)) + b_ada[l][None, :]
        mods.append((row[:, :D], row[:, D:2 * D], row[:, 2 * D:]))
    (shift0, scale0, gate0), (shift1, scale1, gate1) = mods
    vec = lambda a, w=None: (a, a.shape[1] if w is None else w, 0, False)

    def f_mod0(r, v):
        xh, _ = _ln_stats(r[0])
        return [xh * (1.0 + v[0]) + v[1]], []
    (h0,), _ = _rowwise(f_mod0, [(x2, D, 0, False)], [vec(scale0), vec(shift0)], [(BF16, D, D)], [],
                        n_rows=S, tr=128, name="l0_modulate")
    u0 = _mm(h0, win0, "nn", F32, "l0_in", b_slots=True)
    v2 = _conv_fwd(u0, w_dw, b_dw, S, C, "l0_conv")

    def f_gate0(r, v):
        xh, _ = _ln_stats(r[0])
        return [_silu(xh * v[0] + v[1]) * _silu(r[1])], []
    (p0,), _ = _rowwise(f_gate0, [(v2, C, 0, False), (u0, C, 2, False)], [vec(g_cn), vec(b_cn)], [(BF16, C, C)], [],
                        n_rows=S, tr=128, name="l0_gate")
    o0 = _mm(p0, wout0, "nn", F32, "l0_out")

    def f_res0(r, v):
        gate, g0, b0, sc1, sh1 = v
        xh, _ = _ln_stats(ALPHA * r[0] + (1.0 + gate) * r[1])
        x1 = xh * g0 + b0
        xh1, _ = _ln_stats(x1)
        return [x1, x1, xh1 * (1.0 + sc1) + sh1], []
    (x1, x1b, h1), _ = _rowwise(f_res0, [(x2, D, 0, False), (o0, D, 0, False)],
                                [vec(gate0), vec(ln_g[0:1]), vec(ln_b[0:1]), vec(scale1), vec(shift1)],
                                [(F32, D, D), (BF16, D, D), (BF16, D, D)], [], n_rows=S, tr=128, name="l0_residual")

    kva = _mm(x1b, wkva, "nn", F32, "kv_a")
    g_kv = kv_norm_g[None, :]

    def f_kvn(r, v):
        xh, _ = _rms_stats(r[0][:, :KL])
        return [xh * v[0], _rope_fwd(r[0][:, KL:], r[1], r[2], r[3])], []
    (ckv, krp), _ = _rowwise(f_kvn, [(kva, kva_w, 0, False)] + rope_rows, [vec(g_kv)],
                             [(BF16, KL, KL), (BF16, LANES, LANES)], [], n_rows=S, tr=256, name="kv_norm_rope")
    kvh = _mm(ckv, wkvb, "nn", BF16, "kv_b", b_slots=True)

    u1 = _mm(h1, wbin, "nn", F32, "l1_in", b_slots=True)
    g_q = b_q_norm_g[0:1]

    def f_qn(r, v):
        xh, _ = _rms_stats(r[0])
        return [xh * v[0]], []
    (cqn,), _ = _rowwise(f_qn, [(u1, QL, 0, False)], [vec(g_q)], [(BF16, QL, QL)], [], n_rows=S, tr=256, name="q_norm")
    qf = _mm(cqn, wqb, "nn", F32, "q_b", b_slots=True)

    def f_qrope(r, v):
        return [jnp.concatenate([r[0][:, :NOPE], _rope_fwd(r[0][:, NOPE:], r[1], r[2], r[3])], axis=1)], []
    (qb,), _ = _rowwise(f_qrope, [(qf, QPAD, 0, True)] + rope_rows, [], [(BF16, H * QPAD, QPAD)], [],
                        n_rows=S, tr=512 if S % 512 == 0 else S, name="q_rope", nc=H)
    oa, lse, lset = _attn_fwd(qb, kvh, krp, S, H, "attn_fwd")
    zc0 = QL // TW

    def f_gate1(r, v):
        return [r[0] * _silu(r[1])], []
    (p1,), _ = _rowwise(f_gate1, [(oa, TW, 0, True), (u1, TW, zc0, True)], [], [(BF16, W, TW)], [],
                        n_rows=S, tr=512 if S % 512 == 0 else S, name="l1_gate", nc=W // TW)
    o1 = _mm(p1, wbout, "nn", F32, "l1_out")

    def f_head(r, v):
        gate, g1, b1 = v
        xh, rstd = _ln_stats(ALPHA * r[0] + (1.0 + gate) * r[1])
        err = xh * g1 + b1 - r[2]
        dy = err * (1.0 / D)
        dr = _ln_bwd(dy * g1, xh, rstd)
        return [dr, (1.0 + gate) * dr], [_colsum(err * err), _colsum(dy * xh), _colsum(dy), _colsum(dr * r[1])]
    (dr1, do1b), (loss_row, dlng1, dlnb1, dgate1) = _rowwise(
        f_head, [(x1, D, 0, False), (o1, D, 0, False), (tgt, D, 0, False)], [vec(gate1), vec(ln_g[1:2]), vec(ln_b[1:2])],
        [(F32, D, D), (BF16, D, D)], [D, D, D, D], n_rows=S, tr=64, name="head_loss")
    loss = lax.psum(0.5 / D * jnp.sum(loss_row), ("x", "y", "c"))

    dp1 = _mm(do1b, wbout, "nt", F32, "l1_out_dx")
    g_wbout = _mm(p1, do1b, "tn", BF16, "l1_out_dw").reshape(N_DEV, W // N_DEV, D)

    def f_dgate1(r, v):
        dp, o, z = r
        return [dp * _silu(z), dp * o * _dsilu(z)], []
    (doa, dz1), _ = _rowwise(f_dgate1, [(dp1, TW, 0, True), (oa, TW, 0, True), (u1, TW, zc0, True)], [],
                             [(BF16, W, TW), (BF16, W, TW)], [], n_rows=S, tr=512 if S % 512 == 0 else S,
                             name="l1_gate_bwd", nc=W // TW)
    delta, deltat = _attn_delta(doa, oa, S, H, "attn_delta")
    dkvh, dkrp = _attn_dkv(qb, kvh, krp, doa, lset, deltat, S, H, "attn_dkv")
    dqf = _attn_dq(qb, kvh, krp, doa, lse, delta, S, H, "attn_dq")

    def f_qrope_bwd(r, v):
        return [jnp.concatenate([r[0][:, :NOPE], _rope_bwd(r[0][:, NOPE:], r[1], r[2], r[3])], axis=1)], []
    (dqb,), _ = _rowwise(f_qrope_bwd, [(dqf, QPAD, 0, True)] + rope_rows, [], [(BF16, H * QPAD, QPAD)], [],
                         n_rows=S, tr=512 if S % 512 == 0 else S, name="q_rope_bwd", nc=H)
    g_wqb = _mm(cqn, dqb, "tn", BF16, "q_b_dw", out_slots=True)
    dcqn = _mm(dqb, wqb, "nt", F32, "q_b_dx", b_slots=True)

    def f_qn_bwd(r, v):
        xh, rr = _rms_stats(r[0])
        return [_rms_bwd(r[1] * v[0], xh, rr)], [_colsum(r[1] * xh)]
    (dcq,), (dg_q,) = _rowwise(f_qn_bwd, [(u1, QL, 0, False), (dcqn, QL, 0, False)], [vec(g_q)], [(BF16, QL, QL)], [QL],
                               n_rows=S, tr=256, name="q_norm_bwd")
    du1 = jnp.concatenate([dcq, dz1], axis=1)
    g_wbin = _mm(h1, du1, "tn", BF16, "l1_in_dw", out_slots=True)
    dh1 = _mm(du1, wbin, "nt", F32, "l1_in_dx", b_slots=True)

    g_wkvb = _mm(ckv, dkvh, "tn", BF16, "kv_b_dw", out_slots=True)
    dckv = _mm(dkvh, wkvb, "nt", F32, "kv_b_dx", b_slots=True)

    def f_kvn_bwd(r, v):
        xh, rr = _rms_stats(r[0][:, :KL])
        dck = _rms_bwd(r[1] * v[0], xh, rr)
        dkr = _rope_bwd(r[2], r[3], r[4], r[5])
        return [jnp.concatenate([dck, dkr], axis=1)], [_colsum(r[1] * xh)]
    (dkva,), (dg_kv,) = _rowwise(f_kvn_bwd, [(kva, kva_w, 0, False), (dckv, KL, 0, False), (dkrp, LANES, 0, False)] + rope_rows,
                                 [vec(g_kv)], [(BF16, kva_w, kva_w)], [KL], n_rows=S, tr=256, name="kv_norm_rope_bwd")
    g_wkva = _mm(x1b, dkva, "tn", BF16, "kv_a_dw").reshape(N_DEV, D // N_DEV, kva_w)
    dx1_kv = _mm(dkva, wkva, "nt", F32, "kv_a_dx")

    def f_res0_bwd(r, v):
        dr1_, dxkv, dh, x1_, x_, o_ = r
        sc1, gate, g0 = v
        xh1, rstd1 = _ln_stats(x1_)
        dx1 = ALPHA * dr1_ + dxkv + _ln_bwd(dh * (1.0 + sc1), xh1, rstd1)
        xh0, rstd0 = _ln_stats(ALPHA * x_ + (1.0 + gate) * o_)
        dr0 = _ln_bwd(dx1 * g0, xh0, rstd0)
        return ([dr0, (1.0 + gate) * dr0],
                [_colsum(dh * xh1), _colsum(dh), _colsum(dx1 * xh0), _colsum(dx1), _colsum(dr0 * o_)])
    (dr0, do0b), (dscale1, dshift1, dlng0, dlnb0, dgate0) = _rowwise(
        f_res0_bwd, [(a, D, 0, False) for a in (dr1, dx1_kv, dh1, x1, x2, o0)], [vec(scale1), vec(gate0), vec(ln_g[0:1])],
        [(F32, D, D), (BF16, D, D)], [D] * 5, n_rows=S, tr=64, name="l0_residual_bwd")

    dp0 = _mm(do0b, wout0, "nt", F32, "l0_out_dx")
    g_wout0 = _mm(p0, do0b, "tn", BF16, "l0_out_dw").reshape(N_DEV, C // N_DEV, D)

    def f_gate0_bwd(r, v):
        dp, v2_, z = r
        g, b = v
        xh, rstd = _ln_stats(v2_)
        v3 = xh * g + b
        dv4 = dp * _silu(z)
        dz = dp * _silu(v3) * _dsilu(z)
        dv3 = dv4 * _dsilu(v3)
        dv2_ = _ln_bwd(dv3 * g, xh, rstd)
        return [dz, dv2_], [_colsum(dv3 * xh), _colsum(dv3), _colsum(dv2_)]
    (dz0, dv2), (dg_cn, db_cn, db_dw) = _rowwise(
        f_gate0_bwd, [(dp0, C, 0, False), (v2, C, 0, False), (u0, C, 2, False)], [vec(g_cn), vec(b_cn)],
        [(BF16, C, C), (F32, C, C)], [C, C, C], n_rows=S, tr=64, name="l0_gate_bwd")
    da0, dg0, dw_dw = _conv_bwd(u0, dv2, w_dw, S, C, "l0_conv_bwd")
    du0 = jnp.concatenate([da0, dg0, dz0], axis=1)
    g_win0 = _mm(h0, du0, "tn", BF16, "l0_in_dw", out_slots=True)
    dh0 = _mm(du0, win0, "nt", F32, "l0_in_dx", b_slots=True)

    def f_mod0_bwd(r, v):
        xh, rstd = _ln_stats(r[2])
        return [ALPHA * r[0] + _ln_bwd(r[1] * (1.0 + v[0]), xh, rstd)], [_colsum(r[1] * xh), _colsum(r[1])]
    (grad_x,), (dscale0, dshift0) = _rowwise(
        f_mod0_bwd, [(dr0, D, 0, False), (dh0, D, 0, False), (x2, D, 0, False)], [vec(scale0)], [(F32, D, D)], [D, D],
        n_rows=S, tr=128, name="l0_modulate_bwd")

    assert C == D and QL <= D and KL <= D
    n_small = 48
    singles = [dlng0, dlng1, dlnb0, dlnb1, dshift0, dscale0, dgate0, dshift1, dscale1, dgate1, db_dw, dg_cn, db_cn, dg_q, dg_kv]
    rows_loc = _pack_rows([(dw_dw, 0)] + [(a, v0 + i) for i, a in enumerate(singles)], n_small, D, "pack_small_grads")
    parts = _all_gather(rows_loc, "ag_small_grads")
    tot = _slot_sum(parts, "small_grads_sum")
    g_ln_g, g_ln_b = tot[v0:v0 + 2], tot[v0 + 2:v0 + 4]
    g_b_ada = tot[v0 + 4:v0 + 10].reshape(DEPTH, 3 * D)
    loc_cols = lambda a: lax.dynamic_slice(a, (0, idx * cl), (a.shape[0], cl))
    g_a_w_dw = loc_cols(tot[0:CONV_KERNEL])
    g_a_b_dw = loc_cols(tot[v0 + 10:v0 + 11])
    g_a_norm_g = loc_cols(tot[v0 + 11:v0 + 12])
    g_a_norm_b = loc_cols(tot[v0 + 12:v0 + 13])
    g_q_norm = tot[v0 + 13:v0 + 14, :QL]
    g_kv_norm = tot[v0 + 14:v0 + 15, :KL]

    g_w_ada = []
    for l in range(DEPTH):
        dmod = parts[:, v0 + 4 + 3 * l:v0 + 7 + 3 * l, :].reshape(N_DEV, 3 * D)
        dmod = lax.dynamic_slice(dmod, (0, idx * nl), (N_DEV, nl))
        dmod = jnp.concatenate([dmod, jnp.zeros((16 - N_DEV, nl), F32)], axis=0)
        g_w_ada.append(_mm(sc, dmod, "tn", F32, f"mod{l}_dw"))
    g_w_ada = jnp.stack(g_w_ada)

    r_win0 = _reduce_scatter(g_win0, core, "rs_a_w_in")
    r_wout0 = _reduce_scatter(g_wout0, core, "rs_a_w_out")
    r_wbin = _reduce_scatter(g_wbin, core, "rs_b_w_in")
    r_wqb = _reduce_scatter(g_wqb, core, "rs_b_w_qb")
    r_wbout = _reduce_scatter(g_wbout, core, "rs_b_w_out")
    r_wkva = _reduce_scatter(g_wkva, core, "rs_kv_w_a")
    r_wkvb = _reduce_scatter(g_wkvb, core, "rs_kv_w_b")
    r_wqb = r_wqb.reshape(N_CHIP, QL, HL, QPAD)[..., :NOPE + ROPE].reshape(N_CHIP, QL, HL * (NOPE + ROPE))
    r_wkva = r_wkva[..., :KL + ROPE]

    def upd(name, w, gparts, m, v):
        shp = w.shape
        r2 = (1, shp[0]) if len(shp) == 1 else (math.prod(shp[:-1]), shp[-1])
        gp = gparts.reshape((gparts.shape[0],) + r2)
        res = _adamw(w.reshape(r2), gp, m.reshape(r2), v.reshape(r2), "adamw_" + name)
        return [a.reshape(shp) for a in res]

    one = lambda g: g[None]
    results = [
        upd("w_ada", w_ada, one(g_w_ada), m_w_ada, v_w_ada),
        upd("b_ada", b_ada, one(g_b_ada), m_b_ada, v_b_ada),
        upd("ln_g", ln_g, one(g_ln_g), m_ln_g, v_ln_g),
        upd("ln_b", ln_b, one(g_ln_b), m_ln_b, v_ln_b),
        upd("a_w_in", a_w_in, r_win0[:, None], m_a_w_in, v_a_w_in),
        upd("a_w_dw", a_w_dw, one(g_a_w_dw[None]), m_a_w_dw, v_a_w_dw),
        upd("a_b_dw", a_b_dw, one(g_a_b_dw), m_a_b_dw, v_a_b_dw),
        upd("a_norm_g", a_norm_g, one(g_a_norm_g), m_a_norm_g, v_a_norm_g),
        upd("a_norm_b", a_norm_b, one(g_a_norm_b), m_a_norm_b, v_a_norm_b),
        upd("a_w_out", a_w_out, r_wout0[:, None], m_a_w_out, v_a_w_out),
        upd("b_w_in", b_w_in, r_wbin[:, None], m_b_w_in, v_b_w_in),
        upd("b_q_norm_g", b_q_norm_g, one(g_q_norm), m_b_q_norm_g, v_b_q_norm_g),
        upd("b_w_qb", b_w_qb, r_wqb[:, None], m_b_w_qb, v_b_w_qb),
        upd("b_w_out", b_w_out, r_wbout[:, None], m_b_w_out, v_b_w_out),
        upd("kv_w_a", kv_w_a, r_wkva, m_kv_w_a, v_kv_w_a),
        upd("kv_norm_g", kv_norm_g, one(g_kv_norm[0]), m_kv_norm_g, v_kv_norm_g),
        upd("kv_w_b", kv_w_b, r_wkvb, m_kv_w_b, v_kv_w_b),
    ]
    grads, deltas, new_m, new_v = zip(*results)
    return (loss, grad_x[None], *grads, *deltas, *new_m, *new_v)
```

```python
import functools
import math

import jax
import jax.numpy as jnp
from jax import lax
from jax.experimental import pallas as pl
from jax.experimental.pallas import tpu as pltpu

F32 = jnp.float32
BF16 = jnp.bfloat16
MESH = pl.DeviceIdType.MESH
ANY = pl.BlockSpec(memory_space=pl.ANY)

N_DEV = 8
N_CHIP = 4
V7X_VMEM_BYTES = 64 * 1024 * 1024
VMEM_LIMIT = V7X_VMEM_BYTES - 8 * 1024 * 1024
LANES = 128
SUBLANES = 8

DEPTH = 2
CONV_KERNEL = 31
NOPE = 128
ROPE = 64
VDIM = 128
QPAD = NOPE + LANES
ROPE_BASE = 10000.0
LN_EPS = 1e-5
RMS_EPS = 1e-6
ALPHA = (2.0 * DEPTH) ** 0.25
ATTN_SCALE = (NOPE + ROPE) ** -0.5
ADAM_LR = 0.001
ADAM_B1 = 0.9
ADAM_B2 = 0.999
ADAM_EPS = 1e-08
ADAM_WD = 0.01
ADAM_STEP = 10

CONV_TOP = 32
CONV_ROWS = 64
ATTN_BLOCK = 512
ATTN_CHUNK = 32
ROPE_HEADS = 8
MM_INPUT_BYTES = 24 * 1024 * 1024
ADAMW_TILE_ELEMS = 256 * 1024


def _pick(dim, prefs):
    for p in prefs:
        if dim % p == 0:
            return p
    return dim


def _params(sem, **kw):
    return pltpu.CompilerParams(dimension_semantics=sem, vmem_limit_bytes=VMEM_LIMIT, **kw)


_DIMS = {"nn": (((1,), (0,)), ((), ())), "nt": (((1,), (1,)), ((), ())), "tn": (((0,), (0,)), ((), ()))}


def _mm(a, b, mode, out_dtype, name, *, b_slots=False, out_slots=False):
    if mode == "tn":
        K, M = a.shape
    else:
        M, K = a.shape
    if b_slots:
        assert mode in ("nn", "nt")
        if mode == "nn":
            _, kb, n_loc = b.shape
            N, slot_w = N_DEV * n_loc, n_loc
        else:
            _, N, k_loc = b.shape
            kb, slot_w = N_DEV * k_loc, k_loc
    else:
        if mode == "nt":
            N, kb = b.shape
        else:
            kb, N = b.shape
        slot_w = N // N_DEV if out_slots else None
    assert kb == K, (name, a.shape, b.shape)
    tm = _pick(M, (1024, 512, 256, 128, 64, 32, 16))
    n_unit = slot_w if (mode == "nn" and b_slots) or out_slots else N
    k_unit = slot_w if (mode == "nt" and b_slots) else K
    tn = _pick(n_unit, (1024, 1152, 768, 640, 512, 384, 256, 128))
    step_bytes = 2 * (tm * a.dtype.itemsize + tn * b.dtype.itemsize)
    tk = _pick(k_unit, [t for t in (4096, 2048, 1536, 1152, 1024, 768, 640, 512, 384, 256, 128, 64, 32, 16)
                        if t * step_bytes <= MM_INPUT_BYTES])
    nk = K // tk

    if mode == "tn":
        a_spec = pl.BlockSpec((tk, tm), lambda i, j, k: (k, i))
    else:
        a_spec = pl.BlockSpec((tm, tk), lambda i, j, k: (i, k))
    if b_slots and mode == "nn":
        per = slot_w // tn
        b_spec = pl.BlockSpec((None, tk, tn), lambda i, j, k: (j // per, k, j % per))
    elif b_slots and mode == "nt":
        per = slot_w // tk
        b_spec = pl.BlockSpec((None, tn, tk), lambda i, j, k: (k // per, j, k % per))
    elif mode == "nt":
        b_spec = pl.BlockSpec((tn, tk), lambda i, j, k: (j, k))
    else:
        b_spec = pl.BlockSpec((tk, tn), lambda i, j, k: (k, j))
    if out_slots:
        per_o = slot_w // tn
        o_spec = pl.BlockSpec((None, tm, tn), lambda i, j, k: (j // per_o, i, j % per_o))
        o_shape = jax.ShapeDtypeStruct((N_DEV, M, slot_w), out_dtype)
    else:
        o_spec = pl.BlockSpec((tm, tn), lambda i, j, k: (i, j))
        o_shape = jax.ShapeDtypeStruct((M, N), out_dtype)
    dims = _DIMS[mode]

    def body(a_ref, b_ref, o_ref, acc_ref):
        k = pl.program_id(2)

        @pl.when(k == 0)
        def _():
            acc_ref[...] = jnp.zeros_like(acc_ref)

        acc_ref[...] += lax.dot_general(a_ref[...].astype(BF16), b_ref[...].astype(BF16), dims,
                                        preferred_element_type=F32)

        @pl.when(k == nk - 1)
        def _():
            o_ref[...] = acc_ref[...].astype(out_dtype)

    return pl.pallas_call(
        body, name=name, grid=(M // tm, N // tn, nk), in_specs=[a_spec, b_spec], out_specs=o_spec,
        out_shape=o_shape, scratch_shapes=[pltpu.VMEM((tm, tn), F32)],
        compiler_params=_params(("parallel", "parallel", "arbitrary")),
    )(a, b)


def _rowwise(fn, rows, vecs, outs, accs, *, n_rows, tr, name, nc=1):
    nr = n_rows // tr
    n_in, n_vec, n_out, n_acc = len(rows), len(vecs), len(outs), len(accs)

    def cmap(cb0, follow):
        if follow:
            return lambda j, i: (i, cb0 + j)
        return lambda j, i: (i, cb0)

    def vmap_(cb0, follow):
        if follow:
            return lambda j, i: (0, cb0 + j)
        return lambda j, i: (0, cb0)

    in_specs = [pl.BlockSpec((tr, w), cmap(cb0, fol)) for (_, w, cb0, fol) in rows]
    in_specs += [pl.BlockSpec((1, w), vmap_(cb0, fol)) for (_, w, cb0, fol) in vecs]
    out_specs = [pl.BlockSpec((tr, bw), lambda j, i: (i, j)) for (_, _, bw) in outs]
    out_specs += [pl.BlockSpec((1, w), lambda j, i: (0, j)) for w in accs]
    out_shape = [jax.ShapeDtypeStruct((n_rows, tw), dt) for (dt, tw, _) in outs]
    out_shape += [jax.ShapeDtypeStruct((1, w * nc), F32) for w in accs]

    def body(*refs):
        ins = [r[...] for r in refs[:n_in]]
        vs = [r[...] for r in refs[n_in:n_in + n_vec]]
        o_refs = refs[n_in + n_vec:n_in + n_vec + n_out]
        a_refs = refs[n_in + n_vec + n_out:]
        o_vals, a_vals = fn(ins, vs)
        for r, v in zip(o_refs, o_vals):
            r[...] = v.astype(r.dtype)
        if n_acc:
            i = pl.program_id(1)

            @pl.when(i == 0)
            def _():
                for r in a_refs:
                    r[...] = jnp.zeros_like(r)

            for r, v in zip(a_refs, a_vals):
                r[...] += v

    res = pl.pallas_call(
        body, name=name, grid=(nc, nr), in_specs=in_specs, out_specs=out_specs, out_shape=out_shape,
        compiler_params=_params(("parallel", "arbitrary")),
    )(*[r[0] for r in rows], *[v[0] for v in vecs])
    return res[:n_out], res[n_out:]


def _colsum(x):
    return jnp.sum(x, axis=0, keepdims=True)


def _ln_stats(x):
    mu = jnp.mean(x, axis=-1, keepdims=True)
    xc = x - mu
    var = jnp.mean(xc * xc, axis=-1, keepdims=True)
    rstd = lax.rsqrt(var + LN_EPS)
    return xc * rstd, rstd


def _ln_bwd(dxhat, xhat, rstd):
    return rstd * (dxhat - jnp.mean(dxhat, axis=-1, keepdims=True)
                   - xhat * jnp.mean(dxhat * xhat, axis=-1, keepdims=True))


def _rms_stats(x):
    r = lax.rsqrt(jnp.mean(x * x, axis=-1, keepdims=True) + RMS_EPS)
    return x * r, r


def _rms_bwd(dxhat, xhat, r):
    return r * (dxhat - xhat * jnp.mean(dxhat * xhat, axis=-1, keepdims=True))


def _silu(x):
    return x * jax.nn.sigmoid(x)


def _dsilu(x):
    s = jax.nn.sigmoid(x)
    return s * (1.0 + x * (1.0 - s))


def _rope_fwd(x, cos_t, sin_a, sin_b):
    return x * cos_t + pltpu.roll(x, LANES - ROPE // 2, 1) * sin_a + pltpu.roll(x, ROPE // 2, 1) * sin_b


def _rope_bwd(dy, cos_t, sin_a, sin_b):
    return dy * cos_t + pltpu.roll(dy * sin_a, ROPE // 2, 1) + pltpu.roll(dy * sin_b, LANES - ROPE // 2, 1)


def _rope_tables(seq):
    half = ROPE // 2
    inv_freq = ROPE_BASE ** (-jnp.arange(half, dtype=F32) / half)
    ang = jnp.arange(seq, dtype=jnp.int32).astype(F32)[:, None] * inv_freq[None, :]
    cos, sin = jnp.cos(ang), jnp.sin(ang)
    z = jnp.zeros_like(cos)
    cos_t = jnp.concatenate([cos, cos, z, z], axis=1)
    sin_a = jnp.concatenate([-sin, z, z, z], axis=1)
    sin_b = jnp.concatenate([z, sin, z, z], axis=1)
    return cos_t, sin_a, sin_b


def _shifted(win, n):
    return [win] + [pltpu.roll(win, n - j, 0) for j in range(1, SUBLANES)]


def _conv_fwd(u0, w_dw, b_dw, seq, width, name):
    tc = LANES
    nct = width // tc
    rows = CONV_ROWS
    n_win = rows + CONV_TOP

    def body(a_ref, g_ref, w_ref, b_ref, o_ref, vpad):
        vpad[0:CONV_TOP, :] = jnp.zeros((CONV_TOP, tc), F32)
        vpad[CONV_TOP:, :] = a_ref[...] * jax.nn.sigmoid(g_ref[...])

        def chunk(i, carry):
            t0 = pl.multiple_of(i * rows, rows)
            rot = _shifted(vpad[pl.ds(t0, n_win), :], n_win)
            acc = jnp.zeros((rows, tc), F32) + b_ref[...]
            for k in range(CONV_KERNEL):
                s = k + CONV_TOP - (CONV_KERNEL - 1)
                q, j = divmod(s, SUBLANES)
                acc = acc + w_ref[k:k + 1, :] * rot[j][SUBLANES * q:SUBLANES * q + rows]
            o_ref[pl.ds(t0, rows), :] = acc
            return carry

        lax.fori_loop(0, seq // rows, chunk, 0)

    return pl.pallas_call(
        body, name=name, grid=(nct,),
        in_specs=[pl.BlockSpec((seq, tc), lambda j: (0, j)), pl.BlockSpec((seq, tc), lambda j: (0, nct + j)),
                  pl.BlockSpec((CONV_KERNEL, tc), lambda j: (0, j)), pl.BlockSpec((1, tc), lambda j: (0, j))],
        out_specs=pl.BlockSpec((seq, tc), lambda j: (0, j)),
        out_shape=jax.ShapeDtypeStruct((seq, width), F32),
        scratch_shapes=[pltpu.VMEM((seq + CONV_TOP, tc), F32)],
        compiler_params=_params(("parallel",)),
    )(u0, u0, w_dw, b_dw)


def _conv_bwd(u0, dv2, w_dw, seq, width, name):
    tc = LANES
    nct = width // tc
    rows = CONV_ROWS
    n_win = rows + CONV_TOP
    nsh = CONV_KERNEL * SUBLANES

    def body(a_ref, g_ref, d_ref, w_ref, da_ref, dg_ref, dw_ref, vpad, dpad, dwacc):
        vpad[0:CONV_TOP, :] = jnp.zeros((CONV_TOP, tc), F32)
        vpad[CONV_TOP:, :] = a_ref[...] * jax.nn.sigmoid(g_ref[...])
        dpad[0:seq, :] = d_ref[...]
        dpad[seq:, :] = jnp.zeros((CONV_TOP, tc), F32)
        dwacc[...] = jnp.zeros_like(dwacc)

        def chunk(i, carry):
            t0 = pl.multiple_of(i * rows, rows)
            vrot = _shifted(vpad[pl.ds(t0, n_win), :], n_win)
            drot = _shifted(dpad[pl.ds(t0, n_win), :], n_win)
            dcur = drot[0][0:rows]
            dv1 = jnp.zeros((rows, tc), F32)
            for k in range(CONV_KERNEL):
                q, j = divmod(CONV_KERNEL - 1 - k, SUBLANES)
                dv1 = dv1 + w_ref[k:k + 1, :] * drot[j][SUBLANES * q:SUBLANES * q + rows]
                q, j = divmod(k + CONV_TOP - (CONV_KERNEL - 1), SUBLANES)
                prod = dcur * vrot[j][SUBLANES * q:SUBLANES * q + rows]
                part = prod[0:SUBLANES]
                for r in range(1, rows // SUBLANES):
                    part = part + prod[SUBLANES * r:SUBLANES * (r + 1)]
                dwacc[SUBLANES * k:SUBLANES * (k + 1), :] += part
            a = a_ref[pl.ds(t0, rows), :]
            sg = jax.nn.sigmoid(g_ref[pl.ds(t0, rows), :])
            da_ref[pl.ds(t0, rows), :] = (dv1 * sg).astype(BF16)
            dg_ref[pl.ds(t0, rows), :] = (dv1 * a * sg * (1.0 - sg)).astype(BF16)
            return carry

        lax.fori_loop(0, seq // rows, chunk, 0)
        for k in range(CONV_KERNEL):
            dw_ref[k:k + 1, :] = jnp.sum(dwacc[SUBLANES * k:SUBLANES * (k + 1), :], axis=0, keepdims=True)

    return pl.pallas_call(
        body, name=name, grid=(nct,),
        in_specs=[pl.BlockSpec((seq, tc), lambda j: (0, j)), pl.BlockSpec((seq, tc), lambda j: (0, nct + j)),
                  pl.BlockSpec((seq, tc), lambda j: (0, j)), pl.BlockSpec((CONV_KERNEL, tc), lambda j: (0, j))],
        out_specs=[pl.BlockSpec((seq, tc), lambda j: (0, j)), pl.BlockSpec((seq, tc), lambda j: (0, j)),
                   pl.BlockSpec((CONV_KERNEL, tc), lambda j: (0, j))],
        out_shape=[jax.ShapeDtypeStruct((seq, width), BF16), jax.ShapeDtypeStruct((seq, width), BF16),
                   jax.ShapeDtypeStruct((CONV_KERNEL, width), F32)],
        scratch_shapes=[pltpu.VMEM((seq + CONV_TOP, tc), F32), pltpu.VMEM((seq + CONV_TOP, tc), F32),
                        pltpu.VMEM((nsh, tc), F32)],
        compiler_params=_params(("parallel",)),
    )(u0, u0, dv2, w_dw)


def _lanes(x, n):
    return jnp.tile(x, (1, n // LANES))


def _diag_mask(r0, rows, t, transposed=False):
    r = lax.broadcasted_iota(jnp.int32, (rows, t), 0) + r0
    c = lax.broadcasted_iota(jnp.int32, (rows, t), 1)
    return (c >= r) if transposed else (r >= c)


def _attn_fwd(q, kv, kr, seq, heads, name):
    t = min(ATTN_BLOCK, seq)
    nb = seq // t
    ch = ATTN_CHUNK

    def body(q_ref, kv_ref, kr_ref, o_ref, lse_ref, lset_ref, kf_s, s_s, p_s, m_s, l_s, c_s, acc_s):
        qi = pl.program_id(1)

        @pl.when(qi == 0)
        def _():
            kf_s[:, 0:NOPE] = kv_ref[:, 0:NOPE]
            kf_s[:, NOPE:] = kr_ref[...]

        m_s[...] = jnp.full_like(m_s, -jnp.inf)
        l_s[...] = jnp.zeros_like(l_s)
        acc_s[...] = jnp.zeros_like(acc_s)

        def block(ki, masked):
            k0 = pl.multiple_of(ki * t, t)
            s_s[...] = lax.dot_general(q_ref[...], kf_s[pl.ds(k0, t), :], _DIMS["nt"], preferred_element_type=F32)

            def chunk(c, carry):
                r0 = pl.multiple_of(c * ch, ch)
                rows = pl.ds(r0, ch)
                s_c = s_s[rows, :]
                if masked:
                    s_c = jnp.where(_diag_mask(r0, ch, t), s_c, -jnp.inf)
                m_prev = m_s[rows, :]
                m_new = jnp.maximum(m_prev, jnp.max(s_c, axis=1, keepdims=True))
                p = jnp.exp(s_c - _lanes(m_new, t))
                corr = jnp.exp(m_prev - m_new)
                l_s[rows, :] = corr * l_s[rows, :] + jnp.sum(p, axis=1, keepdims=True)
                m_s[rows, :] = m_new
                c_s[rows, :] = corr
                p_s[rows, :] = p.astype(BF16)
                return carry

            lax.fori_loop(0, t // ch, chunk, 0, unroll=True)
            acc_s[...] = c_s[...] * acc_s[...] + jnp.dot(p_s[...], kv_ref[pl.ds(k0, t), pl.ds(NOPE, VDIM)],
                                                         preferred_element_type=F32)

        def below(ki, carry):
            block(ki, False)
            return carry

        lax.fori_loop(0, qi, below, 0)
        block(qi, True)
        o_ref[...] = acc_s[...] / l_s[...]
        lse = m_s[...] + jnp.log(l_s[...])
        lse_ref[...] = lse
        lset_ref[...] = jnp.transpose(lse)[0:SUBLANES, :]

    return pl.pallas_call(
        body, name=name, grid=(heads, nb),
        in_specs=[pl.BlockSpec((t, QPAD), lambda h, i: (i, h)),
                  pl.BlockSpec((seq, QPAD), lambda h, i: (0, h)),
                  pl.BlockSpec((seq, LANES), lambda h, i: (0, 0))],
        out_specs=[pl.BlockSpec((t, VDIM), lambda h, i: (i, h)),
                   pl.BlockSpec((t, LANES), lambda h, i: (i, h)),
                   pl.BlockSpec((None, None, SUBLANES, t), lambda h, i: (h, i, 0, 0))],
        out_shape=[jax.ShapeDtypeStruct((seq, heads * VDIM), F32), jax.ShapeDtypeStruct((seq, heads * LANES), F32),
                   jax.ShapeDtypeStruct((heads, nb, SUBLANES, t), F32)],
        scratch_shapes=[pltpu.VMEM((seq, QPAD), BF16), pltpu.VMEM((t, t), F32), pltpu.VMEM((t, t), BF16),
                        pltpu.VMEM((t, LANES), F32), pltpu.VMEM((t, LANES), F32), pltpu.VMEM((t, LANES), F32),
                        pltpu.VMEM((t, VDIM), F32)],
        compiler_params=_params(("parallel", "arbitrary")),
    )(q, kv, kr)


def _attn_delta(do, o, seq, heads, name):
    t = min(ATTN_BLOCK, seq)
    nb = seq // t

    def body(do_ref, o_ref, d_ref, dt_ref):
        d = jnp.sum(do_ref[...].astype(F32) * o_ref[...], axis=1, keepdims=True)
        db = jnp.broadcast_to(d, (t, LANES))
        d_ref[...] = db
        dt_ref[...] = jnp.transpose(db)[0:SUBLANES, :]

    return pl.pallas_call(
        body, name=name, grid=(nb, heads),
        in_specs=[pl.BlockSpec((t, VDIM), lambda i, h: (i, h)), pl.BlockSpec((t, VDIM), lambda i, h: (i, h))],
        out_specs=[pl.BlockSpec((t, LANES), lambda i, h: (i, h)),
                   pl.BlockSpec((None, None, SUBLANES, t), lambda i, h: (h, i, 0, 0))],
        out_shape=[jax.ShapeDtypeStruct((seq, heads * LANES), F32), jax.ShapeDtypeStruct((heads, nb, SUBLANES, t), F32)],
        compiler_params=_params(("parallel", "parallel")),
    )(do, o)


def _attn_dq(q, kv, kr, do, lse, delta, seq, heads, name):
    t = min(ATTN_BLOCK, seq)
    nb = seq // t
    ch = ATTN_CHUNK

    def body(q_ref, kv_ref, kr_ref, do_ref, lse_ref, dl_ref, dq_ref, kf_s, s_s, dp_s, ds_s, acc_s):
        qi = pl.program_id(1)

        @pl.when(qi == 0)
        def _():
            kf_s[:, 0:NOPE] = kv_ref[:, 0:NOPE]
            kf_s[:, NOPE:] = kr_ref[...]

        acc_s[...] = jnp.zeros_like(acc_s)

        def block(ki, masked):
            k0 = pl.multiple_of(ki * t, t)
            kfb = kf_s[pl.ds(k0, t), :]
            s_s[...] = lax.dot_general(q_ref[...], kfb, _DIMS["nt"], preferred_element_type=F32)
            dp_s[...] = lax.dot_general(do_ref[...], kv_ref[pl.ds(k0, t), pl.ds(NOPE, VDIM)], _DIMS["nt"],
                                        preferred_element_type=F32)

            def chunk(c, carry):
                r0 = pl.multiple_of(c * ch, ch)
                rows = pl.ds(r0, ch)
                p = jnp.exp(s_s[rows, :] - _lanes(lse_ref[rows, :], t))
                if masked:
                    p = jnp.where(_diag_mask(r0, ch, t), p, 0.0)
                ds_s[rows, :] = (p * (dp_s[rows, :] - _lanes(dl_ref[rows, :], t))).astype(BF16)
                return carry

            lax.fori_loop(0, t // ch, chunk, 0, unroll=True)
            acc_s[...] += jnp.dot(ds_s[...], kfb, preferred_element_type=F32)

        def below(ki, carry):
            block(ki, False)
            return carry

        lax.fori_loop(0, qi, below, 0)
        block(qi, True)
        dq_ref[...] = acc_s[...]

    return pl.pallas_call(
        body, name=name, grid=(heads, nb),
        in_specs=[pl.BlockSpec((t, QPAD), lambda h, i: (i, h)),
                  pl.BlockSpec((seq, QPAD), lambda h, i: (0, h)),
                  pl.BlockSpec((seq, LANES), lambda h, i: (0, 0)),
                  pl.BlockSpec((t, VDIM), lambda h, i: (i, h)),
                  pl.BlockSpec((t, LANES), lambda h, i: (i, h)),
                  pl.BlockSpec((t, LANES), lambda h, i: (i, h))],
        out_specs=pl.BlockSpec((t, QPAD), lambda h, i: (i, h)),
        out_shape=jax.ShapeDtypeStruct((seq, heads * QPAD), F32),
        scratch_shapes=[pltpu.VMEM((seq, QPAD), BF16), pltpu.VMEM((t, t), F32), pltpu.VMEM((t, t), F32),
                        pltpu.VMEM((t, t), BF16), pltpu.VMEM((t, QPAD), F32)],
        compiler_params=_params(("parallel", "arbitrary")),
    )(q, kv, kr, do, lse, delta)


def _attn_dkv(q, kv, kr, do, lset, deltat, seq, heads, name):
    t = min(ATTN_BLOCK, seq)
    nb = seq // t
    ch = ATTN_CHUNK

    def body(q_ref, kv_ref, kr_ref, do_ref, lset_ref, dlt_ref, dkv_ref, dkr_ref, kf_s, st_s, dpt_s, pt_s, dst_s, dk_s, dv_s):
        h, ki = pl.program_id(0), pl.program_id(1)
        k0 = pl.multiple_of(ki * t, t)

        @pl.when(jnp.logical_and(h == 0, ki == 0))
        def _():
            dkr_ref[...] = jnp.zeros_like(dkr_ref)

        kf_s[:, 0:NOPE] = kv_ref[:, 0:NOPE]
        kf_s[:, NOPE:] = kr_ref[...]
        dk_s[...] = jnp.zeros_like(dk_s)
        dv_s[...] = jnp.zeros_like(dv_s)

        def block(qi, masked):
            q0 = pl.multiple_of(qi * t, t)
            qb = q_ref[pl.ds(q0, t), :]
            dob = do_ref[pl.ds(q0, t), :]
            st_s[...] = lax.dot_general(kf_s[...], qb, _DIMS["nt"], preferred_element_type=F32)
            dpt_s[...] = lax.dot_general(kv_ref[:, NOPE:], dob, _DIMS["nt"], preferred_element_type=F32)
            lrow = lset_ref[qi][0:1, :]
            drow = dlt_ref[qi][0:1, :]

            def chunk(c, carry):
                r0 = pl.multiple_of(c * ch, ch)
                rows = pl.ds(r0, ch)
                pt = jnp.exp(st_s[rows, :] - lrow)
                if masked:
                    pt = jnp.where(_diag_mask(r0, ch, t, transposed=True), pt, 0.0)
                pt_s[rows, :] = pt.astype(BF16)
                dst_s[rows, :] = (pt * (dpt_s[rows, :] - drow)).astype(BF16)
                return carry

            lax.fori_loop(0, t // ch, chunk, 0, unroll=True)
            dv_s[...] += jnp.dot(pt_s[...], dob, preferred_element_type=F32)
            dk_s[...] += jnp.dot(dst_s[...], qb, preferred_element_type=F32)

        def above(qi, carry):
            block(qi, False)
            return carry

        block(ki, True)
        lax.fori_loop(ki + 1, nb, above, 0)
        dk = dk_s[...]
        dkv_ref[...] = jnp.concatenate([dk[:, :NOPE], dv_s[...]], axis=1).astype(BF16)
        dkr_ref[pl.ds(k0, t), :] += dk[:, NOPE:]

    return pl.pallas_call(
        body, name=name, grid=(heads, nb),
        in_specs=[pl.BlockSpec((seq, QPAD), lambda h, i: (0, h)),
                  pl.BlockSpec((t, QPAD), lambda h, i: (i, h)),
                  pl.BlockSpec((t, LANES), lambda h, i: (i, 0)),
                  pl.BlockSpec((seq, VDIM), lambda h, i: (0, h)),
                  pl.BlockSpec((None, nb, SUBLANES, t), lambda h, i: (h, 0, 0, 0)),
                  pl.BlockSpec((None, nb, SUBLANES, t), lambda h, i: (h, 0, 0, 0))],
        out_specs=[pl.BlockSpec((t, QPAD), lambda h, i: (i, h)),
                   pl.BlockSpec((seq, LANES), lambda h, i: (0, 0))],
        out_shape=[jax.ShapeDtypeStruct((seq, heads * QPAD), BF16), jax.ShapeDtypeStruct((seq, LANES), F32)],
        scratch_shapes=[pltpu.VMEM((t, QPAD), BF16), pltpu.VMEM((t, t), F32), pltpu.VMEM((t, t), F32),
                        pltpu.VMEM((t, t), BF16), pltpu.VMEM((t, t), BF16), pltpu.VMEM((t, QPAD), F32),
                        pltpu.VMEM((t, VDIM), F32)],
        compiler_params=_params(("arbitrary", "arbitrary")),
    )(q, kv, kr, do, lset, deltat)


def _place():
    x, y, c = lax.axis_index("x"), lax.axis_index("y"), lax.axis_index("c")
    return x, y, c


def _all_gather(shard, name):
    r, n = shard.shape

    def body(x_ref, out_ref, send_sems, recv_sems, local_sem):
        x, y, c = _place()
        me, sibling = (x, y, c), (x, y, 1 - c)
        chips = [(1 - x, y), (x, 1 - y), (1 - x, 1 - y)]

        def slot(px, py, pc):
            return out_ref.at[4 * px + 2 * py + pc]

        def copy(k, block, to, src=None):
            return pltpu.make_async_remote_copy(
                src_ref=slot(*block) if src is None else src, dst_ref=slot(*block),
                send_sem=send_sems.at[k], recv_sem=recv_sems.at[k], device_id=to, device_id_type=MESH)

        mine = pltpu.make_async_copy(x_ref, slot(*me), local_sem)
        mine.start()
        first = [copy(0, me, sibling, src=x_ref)]
        first += [copy(1 + j, me, (*chip, c), src=x_ref) for j, chip in enumerate(chips)]
        for cp in first:
            cp.start()
        passed = [copy(4 + j, (*chip, c), sibling) for j, chip in enumerate(chips)]
        for j, chip in enumerate(chips):
            copy(1 + j, (*chip, c), me).wait_recv()
            passed[j].start()
        copy(0, sibling, me).wait_recv()
        for j, chip in enumerate(chips):
            copy(4 + j, (*chip, 1 - c), me).wait_recv()
        for cp in first + passed:
            cp.wait_send()
        mine.wait()

    return pl.pallas_call(
        body, name=name, in_specs=[ANY], out_specs=ANY,
        out_shape=jax.ShapeDtypeStruct((N_DEV, r, n), shard.dtype),
        scratch_shapes=[pltpu.SemaphoreType.DMA((7,)), pltpu.SemaphoreType.DMA((7,)), pltpu.SemaphoreType.DMA],
        compiler_params=pltpu.CompilerParams(has_side_effects=True),
    )(shard)


def _pair_exchange(g8, name):
    _, r, n = g8.shape

    def body(g_ref, out_ref, send_sems, recv_sems):
        x, y, c = _place()
        sibling = (x, y, 1 - c)
        copies = [pltpu.make_async_remote_copy(
            src_ref=g_ref.at[2 * p + (1 - c)], dst_ref=out_ref.at[p], send_sem=send_sems.at[p],
            recv_sem=recv_sems.at[p], device_id=sibling, device_id_type=MESH) for p in range(N_CHIP)]
        for cp in copies:
            cp.start()
        for cp in copies:
            cp.wait_recv()
        for cp in copies:
            cp.wait_send()

    return pl.pallas_call(
        body, name=name, in_specs=[ANY], out_specs=ANY,
        out_shape=jax.ShapeDtypeStruct((N_CHIP, r, n), g8.dtype),
        scratch_shapes=[pltpu.SemaphoreType.DMA((N_CHIP,)), pltpu.SemaphoreType.DMA((N_CHIP,))],
        compiler_params=pltpu.CompilerParams(has_side_effects=True),
    )(g8)


def _pair_add(g8, got, core, name):
    _, r, n = g8.shape
    tr = _pick(r, (256, 128, 64, 32, 16, 8))

    def body(c_ref, g_ref, r_ref, o_ref):
        o_ref[...] = (g_ref[...].astype(F32) + r_ref[...].astype(F32)).astype(o_ref.dtype)

    grid_spec = pltpu.PrefetchScalarGridSpec(
        num_scalar_prefetch=1, grid=(N_CHIP, r // tr),
        in_specs=[pl.BlockSpec((None, tr, n), lambda p, i, c: (2 * p + c[0], i, 0)),
                  pl.BlockSpec((None, tr, n), lambda p, i, c: (p, i, 0))],
        out_specs=pl.BlockSpec((None, tr, n), lambda p, i, c: (p, i, 0)))
    return pl.pallas_call(
        body, name=name, grid_spec=grid_spec, out_shape=jax.ShapeDtypeStruct((N_CHIP, r, n), g8.dtype),
        compiler_params=_params(("parallel", "parallel")),
    )(core, g8, got)


def _chip_exchange(t4, name):
    _, r, n = t4.shape

    def body(t_ref, out_ref, send_sems, recv_sems, local_sem):
        x, y, c = _place()
        mine = 2 * x + y
        chips = [(1 - x, y), (x, 1 - y), (1 - x, 1 - y)]
        local = pltpu.make_async_copy(t_ref.at[mine], out_ref.at[mine], local_sem)
        local.start()
        copies = [pltpu.make_async_remote_copy(
            src_ref=t_ref.at[2 * px + py], dst_ref=out_ref.at[mine], send_sem=send_sems.at[k],
            recv_sem=recv_sems.at[k], device_id=(px, py, c), device_id_type=MESH)
            for k, (px, py) in enumerate(chips)]
        for cp in copies:
            cp.start()
        for k, (px, py) in enumerate(chips):
            pltpu.make_async_remote_copy(
                src_ref=t_ref.at[mine], dst_ref=out_ref.at[2 * px + py], send_sem=send_sems.at[k],
                recv_sem=recv_sems.at[k], device_id=(px, py, c), device_id_type=MESH).wait_recv()
        for cp in copies:
            cp.wait_send()
        local.wait()

    return pl.pallas_call(
        body, name=name, in_specs=[ANY], out_specs=ANY,
        out_shape=jax.ShapeDtypeStruct((N_CHIP, r, n), t4.dtype),
        scratch_shapes=[pltpu.SemaphoreType.DMA((3,)), pltpu.SemaphoreType.DMA((3,)), pltpu.SemaphoreType.DMA],
        compiler_params=pltpu.CompilerParams(has_side_effects=True),
    )(t4)


def _reduce_scatter(g8, core, name):
    got = _pair_exchange(g8, name + "_pair")
    t4 = _pair_add(g8, got, core, name + "_add")
    return _chip_exchange(t4, name + "_chip")


def _slot_sum(g, name):
    k, r, n = g.shape

    def body(g_ref, o_ref):
        acc = g_ref[0].astype(F32)
        for s in range(1, k):
            acc = acc + g_ref[s].astype(F32)
        o_ref[...] = acc

    return pl.pallas_call(body, name=name, out_shape=jax.ShapeDtypeStruct((r, n), F32),
                          compiler_params=pltpu.CompilerParams(vmem_limit_bytes=VMEM_LIMIT))(g)


def _pack_rows(parts, n_rows, width, name):
    starts = [r0 for _, r0 in parts]

    def body(*refs):
        o_ref = refs[-1]
        o_ref[...] = jnp.zeros_like(o_ref)
        for ref, r0 in zip(refs[:-1], starts):
            k, w = ref.shape
            o_ref[r0:r0 + k, 0:w] = ref[...]

    return pl.pallas_call(body, name=name, out_shape=jax.ShapeDtypeStruct((n_rows, width), F32),
                          compiler_params=pltpu.CompilerParams(vmem_limit_bytes=VMEM_LIMIT))(*[a for a, _ in parts])


def _adamw(w, gparts, m, v, name):
    r, n = w.shape
    k = gparts.shape[0]
    tr = _pick(r, [t for t in (512, 256, 128, 64, 32, 16) if t * n <= ADAMW_TILE_ELEMS])
    c1 = 1.0 - ADAM_B1 ** ADAM_STEP
    c2 = 1.0 - ADAM_B2 ** ADAM_STEP

    def body(w_ref, g_ref, m_ref, v_ref, go_ref, d_ref, mo_ref, vo_ref):
        g = g_ref[0].astype(F32)
        for s in range(1, k):
            g = g + g_ref[s].astype(F32)
        mn = ADAM_B1 * m_ref[...] + (1.0 - ADAM_B1) * g
        vn = ADAM_B2 * v_ref[...] + (1.0 - ADAM_B2) * (g * g)
        go_ref[...] = g
        mo_ref[...] = mn
        vo_ref[...] = vn
        d_ref[...] = -ADAM_LR * ((mn / c1) / (jnp.sqrt(vn / c2) + ADAM_EPS) + ADAM_WD * w_ref[...])

    spec = pl.BlockSpec((tr, n), lambda i: (i, 0))
    return pl.pallas_call(
        body, name=name, grid=(r // tr,),
        in_specs=[spec, pl.BlockSpec((k, tr, n), lambda i: (0, i, 0)), spec, spec],
        out_specs=[spec] * 4, out_shape=[jax.ShapeDtypeStruct((r, n), F32)] * 4,
        compiler_params=_params(("parallel",)),
    )(w, gparts, m, v)


def kernel(x, c, w_ada, b_ada, ln_g, ln_b, a_w_in, a_w_dw, a_b_dw, a_norm_g, a_norm_b, a_w_out, b_w_in, b_q_norm_g, b_w_qb, b_w_out, kv_w_a, kv_norm_g, kv_w_b, loss_target, m_w_ada, m_b_ada, m_ln_g, m_ln_b, m_a_w_in, m_a_w_dw, m_a_b_dw, m_a_norm_g, m_a_norm_b, m_a_w_out, m_b_w_in, m_b_q_norm_g, m_b_w_qb, m_b_w_out, m_kv_w_a, m_kv_norm_g, m_kv_w_b, v_w_ada, v_b_ada, v_ln_g, v_ln_b, v_a_w_in, v_a_w_dw, v_a_b_dw, v_a_norm_g, v_a_norm_b, v_a_w_out, v_b_w_in, v_b_q_norm_g, v_b_w_qb, v_b_w_out, v_kv_w_a, v_kv_norm_g, v_kv_w_b):
    S, D = x.shape[1], x.shape[2]
    C = a_w_out.shape[1] * N_DEV
    QL = b_q_norm_g.shape[1]
    KL = kv_norm_g.shape[0]
    H = kv_w_b.shape[1] * N_DEV // (NOPE + VDIM)
    HL = H // N_DEV
    W = H * VDIM
    assert a_w_dw.shape[1] == CONV_KERNEL and kv_w_a.shape[1] == KL + ROPE and b_w_qb.shape[2] == HL * (NOPE + ROPE)
    assert b_w_in.shape[2] * N_DEV == QL + W and D % (N_DEV * LANES) == 0 and S % CONV_ROWS == 0
    TW = 1024 if W % 1024 == 0 and QL % 1024 == 0 else LANES

    xi, yi, ci = _place()
    idx = 4 * xi + 2 * yi + ci
    core = jnp.reshape(ci, (1,)).astype(jnp.int32)
    x2, tgt = x[0], loss_target[0]
    cos_t, sin_a, sin_b = _rope_tables(S)
    rope_rows = [(cos_t, LANES, 0, False), (sin_a, LANES, 0, False), (sin_b, LANES, 0, False)]

    win0 = _all_gather(a_w_in[0].astype(BF16), "ag_a_w_in")
    wout0 = _all_gather(a_w_out[0].astype(BF16), "ag_a_w_out").reshape(C, D)
    wbin = _all_gather(b_w_in[0].astype(BF16), "ag_b_w_in")
    wqb_loc = jnp.pad(b_w_qb[0].reshape(QL, HL, NOPE + ROPE), ((0, 0), (0, 0), (0, QPAD - NOPE - ROPE)))
    wqb = _all_gather(wqb_loc.reshape(QL, HL * QPAD).astype(BF16), "ag_b_w_qb")
    wbout = _all_gather(b_w_out[0].astype(BF16), "ag_b_w_out").reshape(W, D)
    kva_w = KL + LANES
    wkva_loc = jnp.pad(kv_w_a, ((0, 0), (0, kva_w - KL - ROPE)))
    wkva = _all_gather(wkva_loc.astype(BF16), "ag_kv_w_a").reshape(D, kva_w)
    wkvb = _all_gather(kv_w_b.astype(BF16), "ag_kv_w_b")

    cl = C // N_DEV
    v0 = CONV_TOP
    small_loc = _pack_rows([(a_w_dw[0], 0), (a_b_dw, v0), (a_norm_g, v0 + 1), (a_norm_b, v0 + 2)], 40, cl, "pack_conv_small")
    small = _all_gather(small_loc, "ag_conv_small")
    small = jnp.transpose(small, (1, 0, 2)).reshape(40, C)
    w_dw, b_dw, g_cn, b_cn = small[:CONV_KERNEL], small[v0:v0 + 1], small[v0 + 1:v0 + 2], small[v0 + 2:v0 + 3]

    c_all = _all_gather(jnp.broadcast_to(c, (SUBLANES, D)), "ag_c")[:, 0, :]
    c_pad = jnp.concatenate([c_all, jnp.zeros((16 - N_DEV, D), F32)], axis=0)
    (sc,), _ = _rowwise(lambda r, v: ([_silu(r[0])], []), [(c_pad, D, 0, False)], [], [(BF16, D, D)], [],
                        n_rows=16, tr=16, name="silu_c")
    nl = 3 * D // N_DEV
    mod_loc = jnp.concatenate([_mm(sc, w_ada[l], "nn", F32, f"mod{l}") for l in range(DEPTH)], axis=0)
    mod_all = jnp.transpose(_all_gather(mod_loc, "ag_mod"), (1, 0, 2)).reshape(DEPTH, 16, 3 * D)
    mods = []
    for l in range(DEPTH):
        row = lax.dynamic_slice(mod_all[l], (idx, 0), (1, 3 * D)) + b_ada[l][None, :]
        mods.append((row[:, :D], row[:, D:2 * D], row[:, 2 * D:]))
    (shift0, scale0, gate0), (shift1, scale1, gate1) = mods
    vec = lambda a, w=None: (a, a.shape[1] if w is None else w, 0, False)

    def f_mod0(r, v):
        xh, _ = _ln_stats(r[0])
        return [xh * (1.0 + v[0]) + v[1]], []
    (h0,), _ = _rowwise(f_mod0, [(x2, D, 0, False)], [vec(scale0), vec(shift0)], [(BF16, D, D)], [],
                        n_rows=S, tr=128, name="l0_modulate")
    u0 = _mm(h0, win0, "nn", F32, "l0_in", b_slots=True)
    v2 = _conv_fwd(u0, w_dw, b_dw, S, C, "l0_conv")

    def f_gate0(r, v):
        xh, _ = _ln_stats(r[0])
        return [_silu(xh * v[0] + v[1]) * _silu(r[1])], []
    (p0,), _ = _rowwise(f_gate0, [(v2, C, 0, False), (u0, C, 2, False)], [vec(g_cn), vec(b_cn)], [(BF16, C, C)], [],
                        n_rows=S, tr=128, name="l0_gate")
    o0 = _mm(p0, wout0, "nn", F32, "l0_out")

    def f_res0(r, v):
        gate, g0, b0, sc1, sh1 = v
        xh, _ = _ln_stats(ALPHA * r[0] + (1.0 + gate) * r[1])
        x1 = xh * g0 + b0
        xh1, _ = _ln_stats(x1)
        return [x1, x1, xh1 * (1.0 + sc1) + sh1], []
    (x1, x1b, h1), _ = _rowwise(f_res0, [(x2, D, 0, False), (o0, D, 0, False)],
                                [vec(gate0), vec(ln_g[0:1]), vec(ln_b[0:1]), vec(scale1), vec(shift1)],
                                [(F32, D, D), (BF16, D, D), (BF16, D, D)], [], n_rows=S, tr=128, name="l0_residual")

    kva = _mm(x1b, wkva, "nn", F32, "kv_a")
    g_kv = kv_norm_g[None, :]

    def f_kvn(r, v):
        xh, _ = _rms_stats(r[0][:, :KL])
        return [xh * v[0], _rope_fwd(r[0][:, KL:], r[1], r[2], r[3])], []
    (ckv, krp), _ = _rowwise(f_kvn, [(kva, kva_w, 0, False)] + rope_rows, [vec(g_kv)],
                             [(BF16, KL, KL), (BF16, LANES, LANES)], [], n_rows=S, tr=256, name="kv_norm_rope")
    kvh = _mm(ckv, wkvb, "nn", BF16, "kv_b", b_slots=True)

    u1 = _mm(h1, wbin, "nn", F32, "l1_in", b_slots=True)
    g_q = b_q_norm_g[0:1]

    def f_qn(r, v):
        xh, _ = _rms_stats(r[0])
        return [xh * v[0]], []
    (cqn,), _ = _rowwise(f_qn, [(u1, QL, 0, False)], [vec(g_q)], [(BF16, QL, QL)], [], n_rows=S, tr=256, name="q_norm")
    qf = _mm(cqn, wqb, "nn", F32, "q_b", b_slots=True)

    gh = ROPE_HEADS if H % ROPE_HEADS == 0 else 1

    def rope_heads(rope):
        def fn(r, v):
            out = []
            for g in range(gh):
                out.append(r[0][:, g * QPAD:g * QPAD + NOPE] * ATTN_SCALE)
                out.append(rope(r[0][:, g * QPAD + NOPE:(g + 1) * QPAD], r[1], r[2], r[3]) * ATTN_SCALE)
            return [jnp.concatenate(out, axis=1)], []
        return fn
    (qb,), _ = _rowwise(rope_heads(_rope_fwd), [(qf, gh * QPAD, 0, True)] + rope_rows, [], [(BF16, H * QPAD, gh * QPAD)], [],
                        n_rows=S, tr=256, name="q_rope", nc=H // gh)
    oa, lse, lset = _attn_fwd(qb, kvh, krp, S, H, "attn_fwd")
    zc0 = QL // TW

    def f_gate1(r, v):
        return [r[0] * _silu(r[1])], []
    (p1,), _ = _rowwise(f_gate1, [(oa, TW, 0, True), (u1, TW, zc0, True)], [], [(BF16, W, TW)], [],
                        n_rows=S, tr=512 if S % 512 == 0 else S, name="l1_gate", nc=W // TW)
    o1 = _mm(p1, wbout, "nn", F32, "l1_out")

    def f_head(r, v):
        gate, g1, b1 = v
        xh, rstd = _ln_stats(ALPHA * r[0] + (1.0 + gate) * r[1])
        err = xh * g1 + b1 - r[2]
        dy = err * (1.0 / D)
        dr = _ln_bwd(dy * g1, xh, rstd)
        return [dr, (1.0 + gate) * dr], [_colsum(err * err), _colsum(dy * xh), _colsum(dy), _colsum(dr * r[1])]
    (dr1, do1b), (loss_row, dlng1, dlnb1, dgate1) = _rowwise(
        f_head, [(x1, D, 0, False), (o1, D, 0, False), (tgt, D, 0, False)], [vec(gate1), vec(ln_g[1:2]), vec(ln_b[1:2])],
        [(F32, D, D), (BF16, D, D)], [D, D, D, D], n_rows=S, tr=64, name="head_loss")
    loss = lax.psum(0.5 / D * jnp.sum(loss_row), ("x", "y", "c"))

    dp1 = _mm(do1b, wbout, "nt", F32, "l1_out_dx")
    g_wbout = _mm(p1, do1b, "tn", BF16, "l1_out_dw").reshape(N_DEV, W // N_DEV, D)

    def f_dgate1(r, v):
        dp, o, z = r
        return [dp * _silu(z), dp * o * _dsilu(z)], []
    (doa, dz1), _ = _rowwise(f_dgate1, [(dp1, TW, 0, True), (oa, TW, 0, True), (u1, TW, zc0, True)], [],
                             [(BF16, W, TW), (BF16, W, TW)], [], n_rows=S, tr=512 if S % 512 == 0 else S,
                             name="l1_gate_bwd", nc=W // TW)
    delta, deltat = _attn_delta(doa, oa, S, H, "attn_delta")
    dkvh, dkrp = _attn_dkv(qb, kvh, krp, doa, lset, deltat, S, H, "attn_dkv")
    dqf = _attn_dq(qb, kvh, krp, doa, lse, delta, S, H, "attn_dq")

    (dqb,), _ = _rowwise(rope_heads(_rope_bwd), [(dqf, gh * QPAD, 0, True)] + rope_rows, [], [(BF16, H * QPAD, gh * QPAD)], [],
                         n_rows=S, tr=256, name="q_rope_bwd", nc=H // gh)
    g_wqb = _mm(cqn, dqb, "tn", BF16, "q_b_dw", out_slots=True)
    dcqn = _mm(dqb, wqb, "nt", F32, "q_b_dx", b_slots=True)

    def f_qn_bwd(r, v):
        xh, rr = _rms_stats(r[0])
        return [_rms_bwd(r[1] * v[0], xh, rr)], [_colsum(r[1] * xh)]
    (dcq,), (dg_q,) = _rowwise(f_qn_bwd, [(u1, QL, 0, False), (dcqn, QL, 0, False)], [vec(g_q)], [(BF16, QL, QL)], [QL],
                               n_rows=S, tr=256, name="q_norm_bwd")
    du1 = jnp.concatenate([dcq, dz1], axis=1)
    g_wbin = _mm(h1, du1, "tn", BF16, "l1_in_dw", out_slots=True)
    dh1 = _mm(du1, wbin, "nt", F32, "l1_in_dx", b_slots=True)

    g_wkvb = _mm(ckv, dkvh, "tn", BF16, "kv_b_dw", out_slots=True)
    dckv = _mm(dkvh, wkvb, "nt", F32, "kv_b_dx", b_slots=True)

    def f_kvn_bwd(r, v):
        xh, rr = _rms_stats(r[0][:, :KL])
        dck = _rms_bwd(r[1] * v[0], xh, rr)
        dkr = _rope_bwd(r[2], r[3], r[4], r[5])
        return [jnp.concatenate([dck, dkr], axis=1)], [_colsum(r[1] * xh)]
    (dkva,), (dg_kv,) = _rowwise(f_kvn_bwd, [(kva, kva_w, 0, False), (dckv, KL, 0, False), (dkrp, LANES, 0, False)] + rope_rows,
                                 [vec(g_kv)], [(BF16, kva_w, kva_w)], [KL], n_rows=S, tr=256, name="kv_norm_rope_bwd")
    g_wkva = _mm(x1b, dkva, "tn", BF16, "kv_a_dw").reshape(N_DEV, D // N_DEV, kva_w)
    dx1_kv = _mm(dkva, wkva, "nt", F32, "kv_a_dx")

    def f_res0_bwd(r, v):
        dr1_, dxkv, dh, x1_, x_, o_ = r
        sc1, gate, g0 = v
        xh1, rstd1 = _ln_stats(x1_)
        dx1 = ALPHA * dr1_ + dxkv + _ln_bwd(dh * (1.0 + sc1), xh1, rstd1)
        xh0, rstd0 = _ln_stats(ALPHA * x_ + (1.0 + gate) * o_)
        dr0 = _ln_bwd(dx1 * g0, xh0, rstd0)
        return ([dr0, (1.0 + gate) * dr0],
                [_colsum(dh * xh1), _colsum(dh), _colsum(dx1 * xh0), _colsum(dx1), _colsum(dr0 * o_)])
    (dr0, do0b), (dscale1, dshift1, dlng0, dlnb0, dgate0) = _rowwise(
        f_res0_bwd, [(a, D, 0, False) for a in (dr1, dx1_kv, dh1, x1, x2, o0)], [vec(scale1), vec(gate0), vec(ln_g[0:1])],
        [(F32, D, D), (BF16, D, D)], [D] * 5, n_rows=S, tr=64, name="l0_residual_bwd")

    dp0 = _mm(do0b, wout0, "nt", F32, "l0_out_dx")
    g_wout0 = _mm(p0, do0b, "tn", BF16, "l0_out_dw").reshape(N_DEV, C // N_DEV, D)

    def f_gate0_bwd(r, v):
        dp, v2_, z = r
        g, b = v
        xh, rstd = _ln_stats(v2_)
        v3 = xh * g + b
        dv4 = dp * _silu(z)
        dz = dp * _silu(v3) * _dsilu(z)
        dv3 = dv4 * _dsilu(v3)
        dv2_ = _ln_bwd(dv3 * g, xh, rstd)
        return [dz, dv2_], [_colsum(dv3 * xh), _colsum(dv3), _colsum(dv2_)]
    (dz0, dv2), (dg_cn, db_cn, db_dw) = _rowwise(
        f_gate0_bwd, [(dp0, C, 0, False), (v2, C, 0, False), (u0, C, 2, False)], [vec(g_cn), vec(b_cn)],
        [(BF16, C, C), (F32, C, C)], [C, C, C], n_rows=S, tr=64, name="l0_gate_bwd")
    da0, dg0, dw_dw = _conv_bwd(u0, dv2, w_dw, S, C, "l0_conv_bwd")
    du0 = jnp.concatenate([da0, dg0, dz0], axis=1)
    g_win0 = _mm(h0, du0, "tn", BF16, "l0_in_dw", out_slots=True)
    dh0 = _mm(du0, win0, "nt", F32, "l0_in_dx", b_slots=True)

    def f_mod0_bwd(r, v):
        xh, rstd = _ln_stats(r[2])
        return [ALPHA * r[0] + _ln_bwd(r[1] * (1.0 + v[0]), xh, rstd)], [_colsum(r[1] * xh), _colsum(r[1])]
    (grad_x,), (dscale0, dshift0) = _rowwise(
        f_mod0_bwd, [(dr0, D, 0, False), (dh0, D, 0, False), (x2, D, 0, False)], [vec(scale0)], [(F32, D, D)], [D, D],
        n_rows=S, tr=128, name="l0_modulate_bwd")

    assert C == D and QL <= D and KL <= D
    n_small = 48
    singles = [dlng0, dlng1, dlnb0, dlnb1, dshift0, dscale0, dgate0, dshift1, dscale1, dgate1, db_dw, dg_cn, db_cn, dg_q, dg_kv]
    rows_loc = _pack_rows([(dw_dw, 0)] + [(a, v0 + i) for i, a in enumerate(singles)], n_small, D, "pack_small_grads")
    parts = _all_gather(rows_loc, "ag_small_grads")
    tot = _slot_sum(parts, "small_grads_sum")
    g_ln_g, g_ln_b = tot[v0:v0 + 2], tot[v0 + 2:v0 + 4]
    g_b_ada = tot[v0 + 4:v0 + 10].reshape(DEPTH, 3 * D)
    loc_cols = lambda a: lax.dynamic_slice(a, (0, idx * cl), (a.shape[0], cl))
    g_a_w_dw = loc_cols(tot[0:CONV_KERNEL])
    g_a_b_dw = loc_cols(tot[v0 + 10:v0 + 11])
    g_a_norm_g = loc_cols(tot[v0 + 11:v0 + 12])
    g_a_norm_b = loc_cols(tot[v0 + 12:v0 + 13])
    g_q_norm = tot[v0 + 13:v0 + 14, :QL]
    g_kv_norm = tot[v0 + 14:v0 + 15, :KL]

    g_w_ada = []
    for l in range(DEPTH):
        dmod = parts[:, v0 + 4 + 3 * l:v0 + 7 + 3 * l, :].reshape(N_DEV, 3 * D)
        dmod = lax.dynamic_slice(dmod, (0, idx * nl), (N_DEV, nl))
        dmod = jnp.concatenate([dmod, jnp.zeros((16 - N_DEV, nl), F32)], axis=0)
        g_w_ada.append(_mm(sc, dmod, "tn", F32, f"mod{l}_dw"))
    g_w_ada = jnp.stack(g_w_ada)

    r_win0 = _reduce_scatter(g_win0, core, "rs_a_w_in")
    r_wout0 = _reduce_scatter(g_wout0, core, "rs_a_w_out")
    r_wbin = _reduce_scatter(g_wbin, core, "rs_b_w_in")
    r_wqb = _reduce_scatter(g_wqb, core, "rs_b_w_qb")
    r_wbout = _reduce_scatter(g_wbout, core, "rs_b_w_out")
    r_wkva = _reduce_scatter(g_wkva, core, "rs_kv_w_a")
    r_wkvb = _reduce_scatter(g_wkvb, core, "rs_kv_w_b")
    r_wqb = r_wqb.reshape(N_CHIP, QL, HL, QPAD)[..., :NOPE + ROPE].reshape(N_CHIP, QL, HL * (NOPE + ROPE))
    r_wkva = r_wkva[..., :KL + ROPE]

    def upd(name, w, gparts, m, v):
        shp = w.shape
        r2 = (1, shp[0]) if len(shp) == 1 else (math.prod(shp[:-1]), shp[-1])
        gp = gparts.reshape((gparts.shape[0],) + r2)
        res = _adamw(w.reshape(r2), gp, m.reshape(r2), v.reshape(r2), "adamw_" + name)
        return [a.reshape(shp) for a in res]

    one = lambda g: g[None]
    results = [
        upd("w_ada", w_ada, one(g_w_ada), m_w_ada, v_w_ada),
        upd("b_ada", b_ada, one(g_b_ada), m_b_ada, v_b_ada),
        upd("ln_g", ln_g, one(g_ln_g), m_ln_g, v_ln_g),
        upd("ln_b", ln_b, one(g_ln_b), m_ln_b, v_ln_b),
        upd("a_w_in", a_w_in, r_win0[:, None], m_a_w_in, v_a_w_in),
        upd("a_w_dw", a_w_dw, one(g_a_w_dw[None]), m_a_w_dw, v_a_w_dw),
        upd("a_b_dw", a_b_dw, one(g_a_b_dw), m_a_b_dw, v_a_b_dw),
        upd("a_norm_g", a_norm_g, one(g_a_norm_g), m_a_norm_g, v_a_norm_g),
        upd("a_norm_b", a_norm_b, one(g_a_norm_b), m_a_norm_b, v_a_norm_b),
        upd("a_w_out", a_w_out, r_wout0[:, None], m_a_w_out, v_a_w_out),
        upd("b_w_in", b_w_in, r_wbin[:, None], m_b_w_in, v_b_w_in),
        upd("b_q_norm_g", b_q_norm_g, one(g_q_norm), m_b_q_norm_g, v_b_q_norm_g),
        upd("b_w_qb", b_w_qb, r_wqb[:, None], m_b_w_qb, v_b_w_qb),
        upd("b_w_out", b_w_out, r_wbout[:, None], m_b_w_out, v_b_w_out),
        upd("kv_w_a", kv_w_a, r_wkva, m_kv_w_a, v_kv_w_a),
        upd("kv_norm_g", kv_norm_g, one(g_kv_norm[0]), m_kv_norm_g, v_kv_norm_g),
        upd("kv_w_b", kv_w_b, r_wkvb, m_kv_w_b, v_kv_w_b),
    ]
    grads, deltas, new_m, new_v = zip(*results)
    return (loss, grad_x[None], *grads, *deltas, *new_m, *new_v)
```

```python
import functools
import math

import jax
import jax.numpy as jnp
from jax import lax
from jax.experimental import pallas as pl
from jax.experimental.pallas import tpu as pltpu

F32 = jnp.float32
BF16 = jnp.bfloat16
MESH = pl.DeviceIdType.MESH
ANY = pl.BlockSpec(memory_space=pl.ANY)

N_DEV = 8
N_CHIP = 4
V7X_VMEM_BYTES = 64 * 1024 * 1024
VMEM_LIMIT = V7X_VMEM_BYTES - 8 * 1024 * 1024
LANES = 128
SUBLANES = 8

DEPTH = 2
CONV_KERNEL = 31
NOPE = 128
ROPE = 64
VDIM = 128
QPAD = NOPE + LANES
ROPE_BASE = 10000.0
LN_EPS = 1e-5
RMS_EPS = 1e-6
ALPHA = (2.0 * DEPTH) ** 0.25
ATTN_SCALE = (NOPE + ROPE) ** -0.5
ADAM_LR = 0.001
ADAM_B1 = 0.9
ADAM_B2 = 0.999
ADAM_EPS = 1e-08
ADAM_WD = 0.01
ADAM_STEP = 10

CONV_TOP = 32
CONV_ROWS = 64
ATTN_BLOCK = 512
ATTN_CHUNK = 32
ROPE_HEADS = 8
MM_INPUT_BYTES = 24 * 1024 * 1024
ADAMW_TILE_ELEMS = 256 * 1024


def _pick(dim, prefs):
    for p in prefs:
        if dim % p == 0:
            return p
    return dim


def _params(sem, **kw):
    return pltpu.CompilerParams(dimension_semantics=sem, vmem_limit_bytes=VMEM_LIMIT, **kw)


_DIMS = {"nn": (((1,), (0,)), ((), ())), "nt": (((1,), (1,)), ((), ())), "tn": (((0,), (0,)), ((), ()))}


def _mm(a, b, mode, out_dtype, name, *, b_slots=False, out_slots=False, m_rows=None, jobs=()):
    m_first = 0
    if mode == "tn":
        K, M = a.shape
        if m_rows is not None:
            m_first, M = m_rows
    else:
        M, K = a.shape
    if b_slots:
        assert mode in ("nn", "nt")
        if mode == "nn":
            _, kb, n_loc = b.shape
            N, slot_w = N_DEV * n_loc, n_loc
        else:
            _, N, k_loc = b.shape
            kb, slot_w = N_DEV * k_loc, k_loc
    else:
        if mode == "nt":
            N, kb = b.shape
        else:
            kb, N = b.shape
        slot_w = N // N_DEV if out_slots else None
    assert kb == K, (name, a.shape, b.shape)
    tm = _pick(M, (1024, 512, 256, 128, 64, 32, 16))
    n_unit = slot_w if (mode == "nn" and b_slots) or out_slots else N
    k_unit = slot_w if (mode == "nt" and b_slots) else K
    tn = _pick(n_unit, (1024, 1152, 768, 640, 512, 384, 256, 128))
    step_bytes = 2 * (tm * a.dtype.itemsize + tn * b.dtype.itemsize)
    tk = _pick(k_unit, [t for t in (4096, 2048, 1536, 1152, 1024, 768, 640, 512, 384, 256, 128, 64, 32, 16)
                        if t * step_bytes <= MM_INPUT_BYTES])
    nk = K // tk

    if mode == "tn":
        assert m_first % tm == 0
        a_spec = pl.BlockSpec((tk, tm), lambda i, j, k: (k, i + m_first // tm))
    else:
        a_spec = pl.BlockSpec((tm, tk), lambda i, j, k: (i, k))
    if b_slots and mode == "nn":
        per = slot_w // tn
        b_spec = pl.BlockSpec((None, tk, tn), lambda i, j, k: (j // per, k, j % per))
    elif b_slots and mode == "nt":
        per = slot_w // tk
        b_spec = pl.BlockSpec((None, tn, tk), lambda i, j, k: (k // per, j, k % per))
    elif mode == "nt":
        b_spec = pl.BlockSpec((tn, tk), lambda i, j, k: (j, k))
    else:
        b_spec = pl.BlockSpec((tk, tn), lambda i, j, k: (k, j))
    if out_slots:
        per_o = slot_w // tn
        o_spec = pl.BlockSpec((None, tm, tn), lambda i, j, k: (j // per_o, i, j % per_o))
        o_shape = jax.ShapeDtypeStruct((N_DEV, M, slot_w), out_dtype)
    else:
        o_spec = pl.BlockSpec((tm, tn), lambda i, j, k: (i, j))
        o_shape = jax.ShapeDtypeStruct((M, N), out_dtype)
    dims = _DIMS[mode]

    def body(a_ref, b_ref, o_ref, acc_ref):
        k = pl.program_id(2)

        @pl.when(k == 0)
        def _():
            acc_ref[...] = jnp.zeros_like(acc_ref)

        acc_ref[...] += lax.dot_general(a_ref[...].astype(BF16), b_ref[...].astype(BF16), dims,
                                        preferred_element_type=F32)

        @pl.when(k == nk - 1)
        def _():
            o_ref[...] = acc_ref[...].astype(out_dtype)

    grid = (M // tm, N // tn, nk)
    j_ins, j_in_specs, j_out_specs, j_outs, j_sems = _carry_call(jobs)
    res = pl.pallas_call(
        _carry(body, jobs, 2, 1, grid), name=name, grid=grid, in_specs=[a_spec, b_spec] + j_in_specs,
        out_specs=[o_spec] + j_out_specs, out_shape=[o_shape] + j_outs,
        scratch_shapes=[pltpu.VMEM((tm, tn), F32)] + j_sems,
        compiler_params=_params(("arbitrary",) * 3 if jobs else ("parallel", "parallel", "arbitrary"),
                                has_side_effects=bool(jobs)),
    )(a, b, *j_ins)
    return tuple(res) if jobs else res[0]


def _rowwise(fn, rows, vecs, outs, accs, *, n_rows, tr, name, nc=1, jobs=()):
    nr = n_rows // tr
    n_in, n_vec, n_out, n_acc = len(rows), len(vecs), len(outs), len(accs)

    def cmap(cb0, follow):
        if follow:
            return lambda j, i: (i, cb0 + j)
        return lambda j, i: (i, cb0)

    def vmap_(cb0, follow):
        if follow:
            return lambda j, i: (0, cb0 + j)
        return lambda j, i: (0, cb0)

    in_specs = [pl.BlockSpec((tr, w), cmap(cb0, fol)) for (_, w, cb0, fol) in rows]
    in_specs += [pl.BlockSpec((1, w), vmap_(cb0, fol)) for (_, w, cb0, fol) in vecs]
    out_specs = [pl.BlockSpec((tr, bw), lambda j, i: (i, j)) for (_, _, bw) in outs]
    out_specs += [pl.BlockSpec((1, w), lambda j, i: (0, j)) for w in accs]
    out_shape = [jax.ShapeDtypeStruct((n_rows, tw), dt) for (dt, tw, _) in outs]
    out_shape += [jax.ShapeDtypeStruct((1, w * nc), F32) for w in accs]

    def body(*refs):
        ins = [r[...] for r in refs[:n_in]]
        vs = [r[...] for r in refs[n_in:n_in + n_vec]]
        o_refs = refs[n_in + n_vec:n_in + n_vec + n_out]
        a_refs = refs[n_in + n_vec + n_out:]
        o_vals, a_vals = fn(ins, vs)
        for r, v in zip(o_refs, o_vals):
            r[...] = v.astype(r.dtype)
        if n_acc:
            i = pl.program_id(1)

            @pl.when(i == 0)
            def _():
                for r in a_refs:
                    r[...] = jnp.zeros_like(r)

            for r, v in zip(a_refs, a_vals):
                r[...] += v

    res = _grid_call(
        body, [r[0] for r in rows] + [v[0] for v in vecs], name=name, grid=(nc, nr), in_specs=in_specs,
        out_specs=out_specs, out_shape=out_shape, scratch_shapes=[], sem=("parallel", "arbitrary"), jobs=jobs)
    outs, accs_out, extra = res[:n_out], res[n_out:n_out + n_acc], res[n_out + n_acc:]
    return (outs, accs_out, extra) if jobs else (outs, accs_out)


def _colsum(x):
    return jnp.sum(x, axis=0, keepdims=True)


def _ln_stats(x):
    mu = jnp.mean(x, axis=-1, keepdims=True)
    xc = x - mu
    var = jnp.mean(xc * xc, axis=-1, keepdims=True)
    rstd = lax.rsqrt(var + LN_EPS)
    return xc * rstd, rstd


def _ln_bwd(dxhat, xhat, rstd):
    return rstd * (dxhat - jnp.mean(dxhat, axis=-1, keepdims=True)
                   - xhat * jnp.mean(dxhat * xhat, axis=-1, keepdims=True))


def _rms_stats(x):
    r = lax.rsqrt(jnp.mean(x * x, axis=-1, keepdims=True) + RMS_EPS)
    return x * r, r


def _rms_bwd(dxhat, xhat, r):
    return r * (dxhat - xhat * jnp.mean(dxhat * xhat, axis=-1, keepdims=True))


def _silu(x):
    return x * jax.nn.sigmoid(x)


def _dsilu(x):
    s = jax.nn.sigmoid(x)
    return s * (1.0 + x * (1.0 - s))


def _rope_fwd(x, cos_t, sin_a, sin_b):
    return x * cos_t + pltpu.roll(x, LANES - ROPE // 2, 1) * sin_a + pltpu.roll(x, ROPE // 2, 1) * sin_b


def _rope_bwd(dy, cos_t, sin_a, sin_b):
    return dy * cos_t + pltpu.roll(dy * sin_a, ROPE // 2, 1) + pltpu.roll(dy * sin_b, LANES - ROPE // 2, 1)


def _rope_tables(seq):
    half = ROPE // 2
    inv_freq = ROPE_BASE ** (-jnp.arange(half, dtype=F32) / half)
    ang = jnp.arange(seq, dtype=jnp.int32).astype(F32)[:, None] * inv_freq[None, :]
    cos, sin = jnp.cos(ang), jnp.sin(ang)
    z = jnp.zeros_like(cos)
    cos_t = jnp.concatenate([cos, cos, z, z], axis=1)
    sin_a = jnp.concatenate([-sin, z, z, z], axis=1)
    sin_b = jnp.concatenate([z, sin, z, z], axis=1)
    return cos_t, sin_a, sin_b


def _shifted(win, n):
    return [win] + [pltpu.roll(win, n - j, 0) for j in range(1, SUBLANES)]


def _conv_fwd(u0, w_dw, b_dw, seq, width, name):
    tc = LANES
    nct = width // tc
    rows = CONV_ROWS
    n_win = rows + CONV_TOP

    def body(a_ref, g_ref, w_ref, b_ref, o_ref, vpad):
        vpad[0:CONV_TOP, :] = jnp.zeros((CONV_TOP, tc), F32)
        vpad[CONV_TOP:, :] = a_ref[...] * jax.nn.sigmoid(g_ref[...])

        def chunk(i, carry):
            t0 = pl.multiple_of(i * rows, rows)
            rot = _shifted(vpad[pl.ds(t0, n_win), :], n_win)
            acc = jnp.zeros((rows, tc), F32) + b_ref[...]
            for k in range(CONV_KERNEL):
                s = k + CONV_TOP - (CONV_KERNEL - 1)
                q, j = divmod(s, SUBLANES)
                acc = acc + w_ref[k:k + 1, :] * rot[j][SUBLANES * q:SUBLANES * q + rows]
            o_ref[pl.ds(t0, rows), :] = acc
            return carry

        lax.fori_loop(0, seq // rows, chunk, 0)

    return pl.pallas_call(
        body, name=name, grid=(nct,),
        in_specs=[pl.BlockSpec((seq, tc), lambda j: (0, j)), pl.BlockSpec((seq, tc), lambda j: (0, nct + j)),
                  pl.BlockSpec((CONV_KERNEL, tc), lambda j: (0, j)), pl.BlockSpec((1, tc), lambda j: (0, j))],
        out_specs=pl.BlockSpec((seq, tc), lambda j: (0, j)),
        out_shape=jax.ShapeDtypeStruct((seq, width), F32),
        scratch_shapes=[pltpu.VMEM((seq + CONV_TOP, tc), F32)],
        compiler_params=_params(("parallel",)),
    )(u0, u0, w_dw, b_dw)


def _conv_bwd(u0, dv2, w_dw, seq, width, name, jobs=()):
    tc = LANES
    nct = width // tc
    rows = CONV_ROWS
    n_win = rows + CONV_TOP
    nsh = CONV_KERNEL * SUBLANES

    def body(a_ref, g_ref, d_ref, w_ref, da_ref, dg_ref, dw_ref, vpad, dpad, dwacc):
        vpad[0:CONV_TOP, :] = jnp.zeros((CONV_TOP, tc), F32)
        vpad[CONV_TOP:, :] = a_ref[...] * jax.nn.sigmoid(g_ref[...])
        dpad[0:seq, :] = d_ref[...]
        dpad[seq:, :] = jnp.zeros((CONV_TOP, tc), F32)
        dwacc[...] = jnp.zeros_like(dwacc)

        def chunk(i, carry):
            t0 = pl.multiple_of(i * rows, rows)
            vrot = _shifted(vpad[pl.ds(t0, n_win), :], n_win)
            drot = _shifted(dpad[pl.ds(t0, n_win), :], n_win)
            dcur = drot[0][0:rows]
            dv1 = jnp.zeros((rows, tc), F32)
            for k in range(CONV_KERNEL):
                q, j = divmod(CONV_KERNEL - 1 - k, SUBLANES)
                dv1 = dv1 + w_ref[k:k + 1, :] * drot[j][SUBLANES * q:SUBLANES * q + rows]
                q, j = divmod(k + CONV_TOP - (CONV_KERNEL - 1), SUBLANES)
                prod = dcur * vrot[j][SUBLANES * q:SUBLANES * q + rows]
                part = prod[0:SUBLANES]
                for r in range(1, rows // SUBLANES):
                    part = part + prod[SUBLANES * r:SUBLANES * (r + 1)]
                dwacc[SUBLANES * k:SUBLANES * (k + 1), :] += part
            a = a_ref[pl.ds(t0, rows), :]
            sg = jax.nn.sigmoid(g_ref[pl.ds(t0, rows), :])
            da_ref[pl.ds(t0, rows), :] = (dv1 * sg).astype(BF16)
            dg_ref[pl.ds(t0, rows), :] = (dv1 * a * sg * (1.0 - sg)).astype(BF16)
            return carry

        lax.fori_loop(0, seq // rows, chunk, 0)
        for k in range(CONV_KERNEL):
            dw_ref[k:k + 1, :] = jnp.sum(dwacc[SUBLANES * k:SUBLANES * (k + 1), :], axis=0, keepdims=True)

    return _grid_call(
        body, (u0, u0, dv2, w_dw), name=name, grid=(nct,),
        in_specs=[pl.BlockSpec((seq, tc), lambda j: (0, j)), pl.BlockSpec((seq, tc), lambda j: (0, nct + j)),
                  pl.BlockSpec((seq, tc), lambda j: (0, j)), pl.BlockSpec((CONV_KERNEL, tc), lambda j: (0, j))],
        out_specs=[pl.BlockSpec((seq, tc), lambda j: (0, j)), pl.BlockSpec((seq, tc), lambda j: (0, j)),
                   pl.BlockSpec((CONV_KERNEL, tc), lambda j: (0, j))],
        out_shape=[jax.ShapeDtypeStruct((seq, width), BF16), jax.ShapeDtypeStruct((seq, width), BF16),
                   jax.ShapeDtypeStruct((CONV_KERNEL, width), F32)],
        scratch_shapes=[pltpu.VMEM((seq + CONV_TOP, tc), F32), pltpu.VMEM((seq + CONV_TOP, tc), F32),
                        pltpu.VMEM((nsh, tc), F32)],
        sem=("parallel",), jobs=jobs)


def _lanes(x, n):
    return jnp.tile(x, (1, n // LANES))


def _diag_mask(r0, rows, t, transposed=False):
    r = lax.broadcasted_iota(jnp.int32, (rows, t), 0) + r0
    c = lax.broadcasted_iota(jnp.int32, (rows, t), 1)
    return (c >= r) if transposed else (r >= c)


def _attn_fwd(q, kv, kr, seq, heads, name, jobs=()):
    t = min(ATTN_BLOCK, seq)
    nb = seq // t
    ch = ATTN_CHUNK

    def body(q_ref, kv_ref, kr_ref, o_ref, lse_ref, lset_ref, kf_s, s_s, p_s, m_s, l_s, c_s, acc_s):
        qi = pl.program_id(1)

        @pl.when(qi == 0)
        def _():
            kf_s[:, 0:NOPE] = kv_ref[:, 0:NOPE]
            kf_s[:, NOPE:] = kr_ref[...]

        m_s[...] = jnp.full_like(m_s, -jnp.inf)
        l_s[...] = jnp.zeros_like(l_s)
        acc_s[...] = jnp.zeros_like(acc_s)

        def block(ki, masked):
            k0 = pl.multiple_of(ki * t, t)
            s_s[...] = lax.dot_general(q_ref[...], kf_s[pl.ds(k0, t), :], _DIMS["nt"], preferred_element_type=F32)

            def chunk(c, carry):
                r0 = pl.multiple_of(c * ch, ch)
                rows = pl.ds(r0, ch)
                s_c = s_s[rows, :]
                if masked:
                    s_c = jnp.where(_diag_mask(r0, ch, t), s_c, -jnp.inf)
                m_prev = m_s[rows, :]
                m_new = jnp.maximum(m_prev, jnp.max(s_c, axis=1, keepdims=True))
                p = jnp.exp(s_c - _lanes(m_new, t))
                corr = jnp.exp(m_prev - m_new)
                l_s[rows, :] = corr * l_s[rows, :] + jnp.sum(p, axis=1, keepdims=True)
                m_s[rows, :] = m_new
                c_s[rows, :] = corr
                p_s[rows, :] = p.astype(BF16)
                return carry

            lax.fori_loop(0, t // ch, chunk, 0, unroll=True)
            acc_s[...] = c_s[...] * acc_s[...] + jnp.dot(p_s[...], kv_ref[pl.ds(k0, t), pl.ds(NOPE, VDIM)],
                                                         preferred_element_type=F32)

        def below(ki, carry):
            block(ki, False)
            return carry

        lax.fori_loop(0, qi, below, 0)
        block(qi, True)
        o_ref[...] = acc_s[...] / l_s[...]
        lse = m_s[...] + jnp.log(l_s[...])
        lse_ref[...] = lse
        lset_ref[...] = jnp.transpose(lse)[0:SUBLANES, :]

    return _grid_call(
        body, (q, kv, kr), name=name, grid=(heads, nb),
        in_specs=[pl.BlockSpec((t, QPAD), lambda h, i: (i, h)),
                  pl.BlockSpec((seq, QPAD), lambda h, i: (0, h)),
                  pl.BlockSpec((seq, LANES), lambda h, i: (0, 0))],
        out_specs=[pl.BlockSpec((t, VDIM), lambda h, i: (i, h)),
                   pl.BlockSpec((t, LANES), lambda h, i: (i, h)),
                   pl.BlockSpec((None, None, SUBLANES, t), lambda h, i: (h, i, 0, 0))],
        out_shape=[jax.ShapeDtypeStruct((seq, heads * VDIM), F32), jax.ShapeDtypeStruct((seq, heads * LANES), F32),
                   jax.ShapeDtypeStruct((heads, nb, SUBLANES, t), F32)],
        scratch_shapes=[pltpu.VMEM((seq, QPAD), BF16), pltpu.VMEM((t, t), F32), pltpu.VMEM((t, t), BF16),
                        pltpu.VMEM((t, LANES), F32), pltpu.VMEM((t, LANES), F32), pltpu.VMEM((t, LANES), F32),
                        pltpu.VMEM((t, VDIM), F32)],
        sem=("parallel", "arbitrary"), jobs=jobs)


def _attn_delta(do, o, seq, heads, name):
    t = min(ATTN_BLOCK, seq)
    nb = seq // t

    def body(do_ref, o_ref, d_ref, dt_ref):
        d = jnp.sum(do_ref[...].astype(F32) * o_ref[...], axis=1, keepdims=True)
        db = jnp.broadcast_to(d, (t, LANES))
        d_ref[...] = db
        dt_ref[...] = jnp.transpose(db)[0:SUBLANES, :]

    return pl.pallas_call(
        body, name=name, grid=(nb, heads),
        in_specs=[pl.BlockSpec((t, VDIM), lambda i, h: (i, h)), pl.BlockSpec((t, VDIM), lambda i, h: (i, h))],
        out_specs=[pl.BlockSpec((t, LANES), lambda i, h: (i, h)),
                   pl.BlockSpec((None, None, SUBLANES, t), lambda i, h: (h, i, 0, 0))],
        out_shape=[jax.ShapeDtypeStruct((seq, heads * LANES), F32), jax.ShapeDtypeStruct((heads, nb, SUBLANES, t), F32)],
        compiler_params=_params(("parallel", "parallel")),
    )(do, o)


def _attn_dq(q, kv, kr, do, lse, delta, seq, heads, name, jobs=()):
    t = min(ATTN_BLOCK, seq)
    nb = seq // t
    ch = ATTN_CHUNK

    def body(q_ref, kv_ref, kr_ref, do_ref, lse_ref, dl_ref, dq_ref, kf_s, s_s, dp_s, ds_s, acc_s):
        qi = pl.program_id(1)

        @pl.when(qi == 0)
        def _():
            kf_s[:, 0:NOPE] = kv_ref[:, 0:NOPE]
            kf_s[:, NOPE:] = kr_ref[...]

        acc_s[...] = jnp.zeros_like(acc_s)

        def block(ki, masked):
            k0 = pl.multiple_of(ki * t, t)
            kfb = kf_s[pl.ds(k0, t), :]
            s_s[...] = lax.dot_general(q_ref[...], kfb, _DIMS["nt"], preferred_element_type=F32)
            dp_s[...] = lax.dot_general(do_ref[...], kv_ref[pl.ds(k0, t), pl.ds(NOPE, VDIM)], _DIMS["nt"],
                                        preferred_element_type=F32)

            def chunk(c, carry):
                r0 = pl.multiple_of(c * ch, ch)
                rows = pl.ds(r0, ch)
                p = jnp.exp(s_s[rows, :] - _lanes(lse_ref[rows, :], t))
                if masked:
                    p = jnp.where(_diag_mask(r0, ch, t), p, 0.0)
                ds_s[rows, :] = (p * (dp_s[rows, :] - _lanes(dl_ref[rows, :], t))).astype(BF16)
                return carry

            lax.fori_loop(0, t // ch, chunk, 0, unroll=True)
            acc_s[...] += jnp.dot(ds_s[...], kfb, preferred_element_type=F32)

        def below(ki, carry):
            block(ki, False)
            return carry

        lax.fori_loop(0, qi, below, 0)
        block(qi, True)
        dq_ref[...] = acc_s[...]

    return _grid_call(
        body, (q, kv, kr, do, lse, delta), name=name, grid=(heads, nb),
        in_specs=[pl.BlockSpec((t, QPAD), lambda h, i: (i, h)),
                  pl.BlockSpec((seq, QPAD), lambda h, i: (0, h)),
                  pl.BlockSpec((seq, LANES), lambda h, i: (0, 0)),
                  pl.BlockSpec((t, VDIM), lambda h, i: (i, h)),
                  pl.BlockSpec((t, LANES), lambda h, i: (i, h)),
                  pl.BlockSpec((t, LANES), lambda h, i: (i, h))],
        out_specs=[pl.BlockSpec((t, QPAD), lambda h, i: (i, h))],
        out_shape=[jax.ShapeDtypeStruct((seq, heads * QPAD), F32)],
        scratch_shapes=[pltpu.VMEM((seq, QPAD), BF16), pltpu.VMEM((t, t), F32), pltpu.VMEM((t, t), F32),
                        pltpu.VMEM((t, t), BF16), pltpu.VMEM((t, QPAD), F32)],
        sem=("parallel", "arbitrary"), jobs=jobs)


def _attn_dkv(q, kv, kr, do, lset, deltat, seq, heads, name, jobs=()):
    t = min(ATTN_BLOCK, seq)
    nb = seq // t
    ch = ATTN_CHUNK

    def body(q_ref, kv_ref, kr_ref, do_ref, lset_ref, dlt_ref, dkv_ref, dkr_ref, kf_s, st_s, dpt_s, pt_s, dst_s, dk_s, dv_s):
        h, ki = pl.program_id(0), pl.program_id(1)
        k0 = pl.multiple_of(ki * t, t)

        @pl.when(jnp.logical_and(h == 0, ki == 0))
        def _():
            dkr_ref[...] = jnp.zeros_like(dkr_ref)

        kf_s[:, 0:NOPE] = kv_ref[:, 0:NOPE]
        kf_s[:, NOPE:] = kr_ref[...]
        dk_s[...] = jnp.zeros_like(dk_s)
        dv_s[...] = jnp.zeros_like(dv_s)

        def block(qi, masked):
            q0 = pl.multiple_of(qi * t, t)
            qb = q_ref[pl.ds(q0, t), :]
            dob = do_ref[pl.ds(q0, t), :]
            st_s[...] = lax.dot_general(kf_s[...], qb, _DIMS["nt"], preferred_element_type=F32)
            dpt_s[...] = lax.dot_general(kv_ref[:, NOPE:], dob, _DIMS["nt"], preferred_element_type=F32)
            lrow = lset_ref[qi][0:1, :]
            drow = dlt_ref[qi][0:1, :]

            def chunk(c, carry):
                r0 = pl.multiple_of(c * ch, ch)
                rows = pl.ds(r0, ch)
                pt = jnp.exp(st_s[rows, :] - lrow)
                if masked:
                    pt = jnp.where(_diag_mask(r0, ch, t, transposed=True), pt, 0.0)
                pt_s[rows, :] = pt.astype(BF16)
                dst_s[rows, :] = (pt * (dpt_s[rows, :] - drow)).astype(BF16)
                return carry

            lax.fori_loop(0, t // ch, chunk, 0, unroll=True)
            dv_s[...] += jnp.dot(pt_s[...], dob, preferred_element_type=F32)
            dk_s[...] += jnp.dot(dst_s[...], qb, preferred_element_type=F32)

        def above(qi, carry):
            block(qi, False)
            return carry

        block(ki, True)
        lax.fori_loop(ki + 1, nb, above, 0)
        dk = dk_s[...]
        dkv_ref[...] = jnp.concatenate([dk[:, :NOPE], dv_s[...]], axis=1).astype(BF16)
        dkr_ref[pl.ds(k0, t), :] += dk[:, NOPE:]

    return _grid_call(
        body, (q, kv, kr, do, lset, deltat), name=name, grid=(heads, nb),
        in_specs=[pl.BlockSpec((seq, QPAD), lambda h, i: (0, h)),
                  pl.BlockSpec((t, QPAD), lambda h, i: (i, h)),
                  pl.BlockSpec((t, LANES), lambda h, i: (i, 0)),
                  pl.BlockSpec((seq, VDIM), lambda h, i: (0, h)),
                  pl.BlockSpec((None, nb, SUBLANES, t), lambda h, i: (h, 0, 0, 0)),
                  pl.BlockSpec((None, nb, SUBLANES, t), lambda h, i: (h, 0, 0, 0))],
        out_specs=[pl.BlockSpec((t, QPAD), lambda h, i: (i, h)),
                   pl.BlockSpec((seq, LANES), lambda h, i: (0, 0))],
        out_shape=[jax.ShapeDtypeStruct((seq, heads * QPAD), BF16), jax.ShapeDtypeStruct((seq, LANES), F32)],
        scratch_shapes=[pltpu.VMEM((t, QPAD), BF16), pltpu.VMEM((t, t), F32), pltpu.VMEM((t, t), F32),
                        pltpu.VMEM((t, t), BF16), pltpu.VMEM((t, t), BF16), pltpu.VMEM((t, QPAD), F32),
                        pltpu.VMEM((t, VDIM), F32)],
        sem=("arbitrary", "arbitrary"), jobs=jobs)


def _place():
    x, y, c = lax.axis_index("x"), lax.axis_index("y"), lax.axis_index("c")
    return x, y, c


class _Gather:
    def __init__(self, shard):
        r, n = shard.shape
        self.ins = [shard]
        self.outs = [jax.ShapeDtypeStruct((N_DEV, r, n), shard.dtype)]
        self.sems = [pltpu.SemaphoreType.DMA((7,)), pltpu.SemaphoreType.DMA((7,)), pltpu.SemaphoreType.DMA]

    def bind(self, ins, outs, sems):
        (self.x_ref,), (self.out_ref,) = ins, outs
        self.send, self.recv, self.local = sems

    def _plan(self):
        x, y, c = _place()
        chips = [(1 - x, y), (x, 1 - y), (1 - x, 1 - y)]

        def copy(k, block, to, from_shard=False):
            px, py, pc = block
            slot = self.out_ref.at[4 * px + 2 * py + pc]
            return pltpu.make_async_remote_copy(
                src_ref=self.x_ref if from_shard else slot, dst_ref=slot, send_sem=self.send.at[k],
                recv_sem=self.recv.at[k], device_id=to, device_id_type=MESH)

        me, sibling = (x, y, c), (x, y, 1 - c)
        mine = pltpu.make_async_copy(self.x_ref, self.out_ref.at[4 * x + 2 * y + c], self.local)
        first = [copy(0, me, sibling, True)] + [copy(1 + j, me, (*chip, c), True) for j, chip in enumerate(chips)]
        landed = [copy(1 + j, (*chip, c), me) for j, chip in enumerate(chips)]
        passed = [copy(4 + j, (*chip, c), sibling) for j, chip in enumerate(chips)]
        from_sibling = [copy(0, sibling, me)] + [copy(4 + j, (*chip, 1 - c), me) for j, chip in enumerate(chips)]
        return mine, first, landed, passed, from_sibling

    def start(self):
        mine, first, _, _, _ = self._plan()
        mine.start()
        for cp in first:
            cp.start()

    def mid(self):
        _, _, landed, passed, _ = self._plan()
        for got, fwd in zip(landed, passed):
            got.wait_recv()
            fwd.start()

    def finish(self):
        mine, first, _, passed, from_sibling = self._plan()
        for cp in from_sibling:
            cp.wait_recv()
        for cp in first + passed:
            cp.wait_send()
        mine.wait()


class _PairExchange:
    def __init__(self, g8):
        _, r, n = g8.shape
        self.ins = [g8]
        self.outs = [jax.ShapeDtypeStruct((N_CHIP, r, n), g8.dtype)]
        self.sems = [pltpu.SemaphoreType.DMA((N_CHIP,)), pltpu.SemaphoreType.DMA((N_CHIP,))]

    def bind(self, ins, outs, sems):
        (self.g_ref,), (self.out_ref,) = ins, outs
        self.send, self.recv = sems

    def _plan(self):
        x, y, c = _place()
        return [pltpu.make_async_remote_copy(
            src_ref=self.g_ref.at[2 * p + (1 - c)], dst_ref=self.out_ref.at[p], send_sem=self.send.at[p],
            recv_sem=self.recv.at[p], device_id=(x, y, 1 - c), device_id_type=MESH) for p in range(N_CHIP)]

    def start(self):
        for cp in self._plan():
            cp.start()

    def mid(self):
        pass

    def finish(self):
        copies = self._plan()
        for cp in copies:
            cp.wait_recv()
        for cp in copies:
            cp.wait_send()


class _ChipExchange:
    def __init__(self, t4):
        _, r, n = t4.shape
        self.ins = [t4]
        self.outs = [jax.ShapeDtypeStruct((N_CHIP, r, n), t4.dtype)]
        self.sems = [pltpu.SemaphoreType.DMA((3,)), pltpu.SemaphoreType.DMA((3,)), pltpu.SemaphoreType.DMA]

    def bind(self, ins, outs, sems):
        (self.t_ref,), (self.out_ref,) = ins, outs
        self.send, self.recv, self.local_sem = sems

    def _plan(self):
        x, y, c = _place()
        mine = 2 * x + y
        chips = [(1 - x, y), (x, 1 - y), (1 - x, 1 - y)]
        local = pltpu.make_async_copy(self.t_ref.at[mine], self.out_ref.at[mine], self.local_sem)
        sends = [pltpu.make_async_remote_copy(
            src_ref=self.t_ref.at[2 * px + py], dst_ref=self.out_ref.at[mine], send_sem=self.send.at[k],
            recv_sem=self.recv.at[k], device_id=(px, py, c), device_id_type=MESH) for k, (px, py) in enumerate(chips)]
        arrivals = [pltpu.make_async_remote_copy(
            src_ref=self.t_ref.at[mine], dst_ref=self.out_ref.at[2 * px + py], send_sem=self.send.at[k],
            recv_sem=self.recv.at[k], device_id=(px, py, c), device_id_type=MESH) for k, (px, py) in enumerate(chips)]
        return local, sends, arrivals

    def start(self):
        local, sends, _ = self._plan()
        local.start()
        for cp in sends:
            cp.start()

    def mid(self):
        pass

    def finish(self):
        local, sends, arrivals = self._plan()
        for cp in arrivals:
            cp.wait_recv()
        for cp in sends:
            cp.wait_send()
        local.wait()


def _job_specs(jobs):
    ins = [a for j in jobs for a in j.ins]
    outs = [o for j in jobs for o in j.outs]
    sems = [s for j in jobs for s in j.sems]
    return ins, outs, sems


def _bind_jobs(jobs, ins, outs, sems):
    for j in jobs:
        ni, no, ns = len(j.ins), len(j.outs), len(j.sems)
        j.bind(ins[:ni], outs[:no], sems[:ns])
        ins, outs, sems = ins[ni:], outs[no:], sems[ns:]


def _carry(body, jobs, n_in, n_out, grid):
    if not jobs:
        return body
    j_in, j_out, j_sem = (len(v) for v in _job_specs(jobs))
    total = math.prod(grid)
    mid_step = (3 * total) // 4

    def wrapped(*refs):
        ins, rest = refs[:n_in], refs[n_in:]
        jin, rest = rest[:j_in], rest[j_in:]
        outs, rest = rest[:n_out], rest[n_out:]
        jout, rest = rest[:j_out], rest[j_out:]
        scratch, jsem = rest[:len(rest) - j_sem], rest[len(rest) - j_sem:]
        _bind_jobs(jobs, jin, jout, jsem)
        lin = 0
        for d, size in enumerate(grid):
            lin = lin * size + pl.program_id(d)

        @pl.when(lin == 0)
        def _():
            for j in jobs:
                j.start()

        body(*ins, *outs, *scratch)

        @pl.when(lin == mid_step)
        def _():
            for j in jobs:
                j.mid()

        @pl.when(lin == total - 1)
        def _():
            for j in jobs:
                j.finish()

    return wrapped


def _carry_call(jobs):
    ins, outs, sems = _job_specs(jobs)
    return ins, [ANY] * len(ins), [ANY] * len(outs), outs, sems


def _grid_call(body, operands, *, name, grid, in_specs, out_specs, out_shape, scratch_shapes, sem, jobs=()):
    j_ins, j_in_specs, j_out_specs, j_outs, j_sems = _carry_call(jobs)
    return pl.pallas_call(
        _carry(body, jobs, len(in_specs), len(out_specs), grid), name=name, grid=grid,
        in_specs=list(in_specs) + j_in_specs, out_specs=list(out_specs) + j_out_specs,
        out_shape=list(out_shape) + j_outs, scratch_shapes=list(scratch_shapes) + j_sems,
        compiler_params=_params(("arbitrary",) * len(grid) if jobs else sem, has_side_effects=bool(jobs)),
    )(*operands, *j_ins)


def _comm(jobs, name):
    ins, outs, sems = _job_specs(jobs)

    def body(*refs):
        _bind_jobs(jobs, refs[:len(ins)], refs[len(ins):len(ins) + len(outs)], refs[len(ins) + len(outs):])
        for j in jobs:
            j.start()
        for j in jobs:
            j.mid()
        for j in jobs:
            j.finish()

    return pl.pallas_call(
        body, name=name, in_specs=[ANY] * len(ins), out_specs=[ANY] * len(outs), out_shape=outs,
        scratch_shapes=sems, compiler_params=pltpu.CompilerParams(has_side_effects=True),
    )(*ins)


def _all_gather(shard, name):
    return _comm([_Gather(shard)], name)[0]


def _pair_add(g8, got, core, name):
    _, r, n = g8.shape
    tr = _pick(r, (256, 128, 64, 32, 16, 8))

    def body(c_ref, g_ref, r_ref, o_ref):
        o_ref[...] = (g_ref[...].astype(F32) + r_ref[...].astype(F32)).astype(o_ref.dtype)

    grid_spec = pltpu.PrefetchScalarGridSpec(
        num_scalar_prefetch=1, grid=(N_CHIP, r // tr),
        in_specs=[pl.BlockSpec((None, tr, n), lambda p, i, c: (2 * p + c[0], i, 0)),
                  pl.BlockSpec((None, tr, n), lambda p, i, c: (p, i, 0))],
        out_specs=pl.BlockSpec((None, tr, n), lambda p, i, c: (p, i, 0)))
    return pl.pallas_call(
        body, name=name, grid_spec=grid_spec, out_shape=jax.ShapeDtypeStruct((N_CHIP, r, n), g8.dtype),
        compiler_params=_params(("parallel", "parallel")),
    )(core, g8, got)


def _slot_sum(g, name):
    k, r, n = g.shape

    def body(g_ref, o_ref):
        acc = g_ref[0].astype(F32)
        for s in range(1, k):
            acc = acc + g_ref[s].astype(F32)
        o_ref[...] = acc

    return pl.pallas_call(body, name=name, out_shape=jax.ShapeDtypeStruct((r, n), F32),
                          compiler_params=pltpu.CompilerParams(vmem_limit_bytes=VMEM_LIMIT))(g)


def _pack_rows(parts, n_rows, width, name):
    starts = [r0 for _, r0 in parts]

    def body(*refs):
        o_ref = refs[-1]
        o_ref[...] = jnp.zeros_like(o_ref)
        for ref, r0 in zip(refs[:-1], starts):
            k, w = ref.shape
            o_ref[r0:r0 + k, 0:w] = ref[...]

    return pl.pallas_call(body, name=name, out_shape=jax.ShapeDtypeStruct((n_rows, width), F32),
                          compiler_params=pltpu.CompilerParams(vmem_limit_bytes=VMEM_LIMIT))(*[a for a, _ in parts])


def _adamw(w, gparts, m, v, name):
    r, n = w.shape
    k = gparts.shape[0]
    tr = _pick(r, [t for t in (512, 256, 128, 64, 32, 16) if t * n <= ADAMW_TILE_ELEMS])
    c1 = 1.0 - ADAM_B1 ** ADAM_STEP
    c2 = 1.0 - ADAM_B2 ** ADAM_STEP

    def body(w_ref, g_ref, m_ref, v_ref, go_ref, d_ref, mo_ref, vo_ref):
        g = g_ref[0].astype(F32)
        for s in range(1, k):
            g = g + g_ref[s].astype(F32)
        mn = ADAM_B1 * m_ref[...] + (1.0 - ADAM_B1) * g
        vn = ADAM_B2 * v_ref[...] + (1.0 - ADAM_B2) * (g * g)
        go_ref[...] = g
        mo_ref[...] = mn
        vo_ref[...] = vn
        d_ref[...] = -ADAM_LR * ((mn / c1) / (jnp.sqrt(vn / c2) + ADAM_EPS) + ADAM_WD * w_ref[...])

    spec = pl.BlockSpec((tr, n), lambda i: (i, 0))
    return pl.pallas_call(
        body, name=name, grid=(r // tr,),
        in_specs=[spec, pl.BlockSpec((k, tr, n), lambda i: (0, i, 0)), spec, spec],
        out_specs=[spec] * 4, out_shape=[jax.ShapeDtypeStruct((r, n), F32)] * 4,
        compiler_params=_params(("parallel",)),
    )(w, gparts, m, v)


def kernel(x, c, w_ada, b_ada, ln_g, ln_b, a_w_in, a_w_dw, a_b_dw, a_norm_g, a_norm_b, a_w_out, b_w_in, b_q_norm_g, b_w_qb, b_w_out, kv_w_a, kv_norm_g, kv_w_b, loss_target, m_w_ada, m_b_ada, m_ln_g, m_ln_b, m_a_w_in, m_a_w_dw, m_a_b_dw, m_a_norm_g, m_a_norm_b, m_a_w_out, m_b_w_in, m_b_q_norm_g, m_b_w_qb, m_b_w_out, m_kv_w_a, m_kv_norm_g, m_kv_w_b, v_w_ada, v_b_ada, v_ln_g, v_ln_b, v_a_w_in, v_a_w_dw, v_a_b_dw, v_a_norm_g, v_a_norm_b, v_a_w_out, v_b_w_in, v_b_q_norm_g, v_b_w_qb, v_b_w_out, v_kv_w_a, v_kv_norm_g, v_kv_w_b):
    S, D = x.shape[1], x.shape[2]
    C = a_w_out.shape[1] * N_DEV
    QL = b_q_norm_g.shape[1]
    KL = kv_norm_g.shape[0]
    H = kv_w_b.shape[1] * N_DEV // (NOPE + VDIM)
    HL = H // N_DEV
    W = H * VDIM
    assert a_w_dw.shape[1] == CONV_KERNEL and kv_w_a.shape[1] == KL + ROPE and b_w_qb.shape[2] == HL * (NOPE + ROPE)
    assert b_w_in.shape[2] * N_DEV == QL + W and D % (N_DEV * LANES) == 0 and S % CONV_ROWS == 0
    TW = 1024 if W % 1024 == 0 and QL % 1024 == 0 else LANES

    xi, yi, ci = _place()
    idx = 4 * xi + 2 * yi + ci
    core = jnp.reshape(ci, (1,)).astype(jnp.int32)
    x2, tgt = x[0], loss_target[0]
    cos_t, sin_a, sin_b = _rope_tables(S)
    rope_rows = [(cos_t, LANES, 0, False), (sin_a, LANES, 0, False), (sin_b, LANES, 0, False)]

    win0 = _all_gather(a_w_in[0].astype(BF16), "ag_a_w_in")
    sh_wout0 = a_w_out[0].astype(BF16)
    sh_wbin = b_w_in[0].astype(BF16)
    wqb_loc = jnp.pad(b_w_qb[0].reshape(QL, HL, NOPE + ROPE), ((0, 0), (0, 0), (0, QPAD - NOPE - ROPE)))
    sh_wqb = wqb_loc.reshape(QL, HL * QPAD).astype(BF16)
    sh_wbout = b_w_out[0].astype(BF16)
    kva_w = KL + LANES
    sh_wkva = jnp.pad(kv_w_a, ((0, 0), (0, kva_w - KL - ROPE))).astype(BF16)
    sh_wkvb = kv_w_b.astype(BF16)

    cl = C // N_DEV
    v0 = CONV_TOP
    small_loc = _pack_rows([(a_w_dw[0], 0), (a_b_dw, v0), (a_norm_g, v0 + 1), (a_norm_b, v0 + 2)], 40, cl, "pack_conv_small")
    small = _all_gather(small_loc, "ag_conv_small")
    small = jnp.transpose(small, (1, 0, 2)).reshape(40, C)
    w_dw, b_dw, g_cn, b_cn = small[:CONV_KERNEL], small[v0:v0 + 1], small[v0 + 1:v0 + 2], small[v0 + 2:v0 + 3]

    c_all = _all_gather(jnp.broadcast_to(c, (SUBLANES, D)), "ag_c")[:, 0, :]
    c_pad = jnp.concatenate([c_all, jnp.zeros((16 - N_DEV, D), F32)], axis=0)
    (sc,), _ = _rowwise(lambda r, v: ([_silu(r[0])], []), [(c_pad, D, 0, False)], [], [(BF16, D, D)], [],
                        n_rows=16, tr=16, name="silu_c")
    nl = 3 * D // N_DEV
    mod_loc = jnp.concatenate([_mm(sc, w_ada[l], "nn", F32, f"mod{l}") for l in range(DEPTH)], axis=0)
    mod_all = jnp.transpose(_all_gather(mod_loc, "ag_mod"), (1, 0, 2)).reshape(DEPTH, 16, 3 * D)
    mods = []
    for l in range(DEPTH):
        row = lax.dynamic_slice(mod_all[l], (idx, 0), (1, 3 * D)) + b_ada[l][None, :]
        mods.append((row[:, :D], row[:, D:2 * D], row[:, 2 * D:]))
    (shift0, scale0, gate0), (shift1, scale1, gate1) = mods
    vec = lambda a, w=None: (a, a.shape[1] if w is None else w, 0, False)

    def f_mod0(r, v):
        xh, _ = _ln_stats(r[0])
        return [xh * (1.0 + v[0]) + v[1]], []
    (h0,), _ = _rowwise(f_mod0, [(x2, D, 0, False)], [vec(scale0), vec(shift0)], [(BF16, D, D)], [],
                        n_rows=S, tr=128, name="l0_modulate")
    u0, wout0, wbin = _mm(h0, win0, "nn", F32, "l0_in", b_slots=True,
                          jobs=[_Gather(sh_wout0), _Gather(sh_wbin)])
    wout0 = wout0.reshape(C, D)
    v2 = _conv_fwd(u0, w_dw, b_dw, S, C, "l0_conv")

    def f_gate0(r, v):
        xh, _ = _ln_stats(r[0])
        return [_silu(xh * v[0] + v[1]) * _silu(r[1])], []
    (p0,), _ = _rowwise(f_gate0, [(v2, C, 0, False), (u0, C, 2, False)], [vec(g_cn), vec(b_cn)], [(BF16, C, C)], [],
                        n_rows=S, tr=128, name="l0_gate")
    o0, wkva, wkvb = _mm(p0, wout0, "nn", F32, "l0_out", jobs=[_Gather(sh_wkva), _Gather(sh_wkvb)])
    wkva = wkva.reshape(D, kva_w)

    def f_res0(r, v):
        gate, g0, b0, sc1, sh1 = v
        xh, _ = _ln_stats(ALPHA * r[0] + (1.0 + gate) * r[1])
        x1 = xh * g0 + b0
        xh1, _ = _ln_stats(x1)
        return [x1, x1, xh1 * (1.0 + sc1) + sh1], []
    (x1, x1b, h1), _ = _rowwise(f_res0, [(x2, D, 0, False), (o0, D, 0, False)],
                                [vec(gate0), vec(ln_g[0:1]), vec(ln_b[0:1]), vec(scale1), vec(shift1)],
                                [(F32, D, D), (BF16, D, D), (BF16, D, D)], [], n_rows=S, tr=128, name="l0_residual")

    kva = _mm(x1b, wkva, "nn", F32, "kv_a")
    g_kv = kv_norm_g[None, :]

    def f_kvn(r, v):
        xh, _ = _rms_stats(r[0][:, :KL])
        return [xh * v[0], _rope_fwd(r[0][:, KL:], r[1], r[2], r[3])], []
    (ckv, krp), _ = _rowwise(f_kvn, [(kva, kva_w, 0, False)] + rope_rows, [vec(g_kv)],
                             [(BF16, KL, KL), (BF16, LANES, LANES)], [], n_rows=S, tr=256, name="kv_norm_rope")
    kvh = _mm(ckv, wkvb, "nn", BF16, "kv_b", b_slots=True)

    u1, wqb = _mm(h1, wbin, "nn", F32, "l1_in", b_slots=True, jobs=[_Gather(sh_wqb)])
    g_q = b_q_norm_g[0:1]

    def f_qn(r, v):
        xh, _ = _rms_stats(r[0])
        return [xh * v[0]], []
    (cqn,), _ = _rowwise(f_qn, [(u1, QL, 0, False)], [vec(g_q)], [(BF16, QL, QL)], [], n_rows=S, tr=256, name="q_norm")
    qf = _mm(cqn, wqb, "nn", F32, "q_b", b_slots=True)

    gh = ROPE_HEADS if H % ROPE_HEADS == 0 else 1

    def rope_heads(rope):
        def fn(r, v):
            out = []
            for g in range(gh):
                out.append(r[0][:, g * QPAD:g * QPAD + NOPE] * ATTN_SCALE)
                out.append(rope(r[0][:, g * QPAD + NOPE:(g + 1) * QPAD], r[1], r[2], r[3]) * ATTN_SCALE)
            return [jnp.concatenate(out, axis=1)], []
        return fn
    (qb,), _ = _rowwise(rope_heads(_rope_fwd), [(qf, gh * QPAD, 0, True)] + rope_rows, [], [(BF16, H * QPAD, gh * QPAD)], [],
                        n_rows=S, tr=256, name="q_rope", nc=H // gh)
    oa, lse, lset, wbout = _attn_fwd(qb, kvh, krp, S, H, "attn_fwd", jobs=[_Gather(sh_wbout)])
    wbout = wbout.reshape(W, D)
    zc0 = QL // TW

    def f_gate1(r, v):
        return [r[0] * _silu(r[1])], []
    (p1,), _ = _rowwise(f_gate1, [(oa, TW, 0, True), (u1, TW, zc0, True)], [], [(BF16, W, TW)], [],
                        n_rows=S, tr=512 if S % 512 == 0 else S, name="l1_gate", nc=W // TW)
    o1 = _mm(p1, wbout, "nn", F32, "l1_out")

    def f_head(r, v):
        gate, g1, b1 = v
        xh, rstd = _ln_stats(ALPHA * r[0] + (1.0 + gate) * r[1])
        err = xh * g1 + b1 - r[2]
        dy = err * (1.0 / D)
        dr = _ln_bwd(dy * g1, xh, rstd)
        return [dr, (1.0 + gate) * dr], [_colsum(err * err), _colsum(dy * xh), _colsum(dy), _colsum(dr * r[1])]
    (dr1, do1b), (loss_row, dlng1, dlnb1, dgate1) = _rowwise(
        f_head, [(x1, D, 0, False), (o1, D, 0, False), (tgt, D, 0, False)], [vec(gate1), vec(ln_g[1:2]), vec(ln_b[1:2])],
        [(F32, D, D), (BF16, D, D)], [D, D, D, D], n_rows=S, tr=64, name="head_loss")
    loss = lax.psum(0.5 / D * jnp.sum(loss_row), ("x", "y", "c"))

    dp1 = _mm(do1b, wbout, "nt", F32, "l1_out_dx")
    g_wbout = _mm(p1, do1b, "tn", BF16, "l1_out_dw").reshape(N_DEV, W // N_DEV, D)

    def f_dgate1(r, v):
        dp, o, z = r
        return [dp * _silu(z), dp * o * _dsilu(z)], []
    (doa, dz1), _ = _rowwise(f_dgate1, [(dp1, TW, 0, True), (oa, TW, 0, True), (u1, TW, zc0, True)], [],
                             [(BF16, W, TW), (BF16, W, TW)], [], n_rows=S, tr=512 if S % 512 == 0 else S,
                             name="l1_gate_bwd", nc=W // TW)
    delta, deltat = _attn_delta(doa, oa, S, H, "attn_delta")
    dkvh, dkrp, got = _attn_dkv(qb, kvh, krp, doa, lset, deltat, S, H, "attn_dkv", jobs=[_PairExchange(g_wbout)])
    t_wbout = _pair_add(g_wbout, got, core, "rs_b_w_out_add")
    dqf, r_wbout = _attn_dq(qb, kvh, krp, doa, lse, delta, S, H, "attn_dq", jobs=[_ChipExchange(t_wbout)])

    (dqb,), _ = _rowwise(rope_heads(_rope_bwd), [(dqf, gh * QPAD, 0, True)] + rope_rows, [], [(BF16, H * QPAD, gh * QPAD)], [],
                         n_rows=S, tr=256, name="q_rope_bwd", nc=H // gh)
    g_wqb = _mm(cqn, dqb, "tn", BF16, "q_b_dw", out_slots=True)
    dcqn, got = _mm(dqb, wqb, "nt", F32, "q_b_dx", b_slots=True, jobs=[_PairExchange(g_wqb)])
    t_wqb = _pair_add(g_wqb, got, core, "rs_b_w_qb_add")

    def f_qn_bwd(r, v):
        xh, rr = _rms_stats(r[0])
        return [_rms_bwd(r[1] * v[0], xh, rr)], [_colsum(r[1] * xh)]
    (dcq,), (dg_q,) = _rowwise(f_qn_bwd, [(u1, QL, 0, False), (dcqn, QL, 0, False)], [vec(g_q)], [(BF16, QL, QL)], [QL],
                               n_rows=S, tr=256, name="q_norm_bwd")
    du1 = jnp.concatenate([dcq, dz1], axis=1)
    g_wbin, r_wqb = _mm(h1, du1, "tn", BF16, "l1_in_dw", out_slots=True, jobs=[_ChipExchange(t_wqb)])
    dh1, got = _mm(du1, wbin, "nt", F32, "l1_in_dx", b_slots=True, jobs=[_PairExchange(g_wbin)])
    t_wbin = _pair_add(g_wbin, got, core, "rs_b_w_in_add")

    g_wkvb = _mm(ckv, dkvh, "tn", BF16, "kv_b_dw", out_slots=True)
    dckv, got = _mm(dkvh, wkvb, "nt", F32, "kv_b_dx", b_slots=True, jobs=[_PairExchange(g_wkvb)])
    t_wkvb = _pair_add(g_wkvb, got, core, "rs_kv_w_b_add")

    def f_kvn_bwd(r, v):
        xh, rr = _rms_stats(r[0][:, :KL])
        dck = _rms_bwd(r[1] * v[0], xh, rr)
        dkr = _rope_bwd(r[2], r[3], r[4], r[5])
        return [jnp.concatenate([dck, dkr], axis=1)], [_colsum(r[1] * xh)]
    (dkva,), (dg_kv,) = _rowwise(f_kvn_bwd, [(kva, kva_w, 0, False), (dckv, KL, 0, False), (dkrp, LANES, 0, False)] + rope_rows,
                                 [vec(g_kv)], [(BF16, kva_w, kva_w)], [KL], n_rows=S, tr=256, name="kv_norm_rope_bwd")
    g_wkva = _mm(x1b, dkva, "tn", BF16, "kv_a_dw").reshape(N_DEV, D // N_DEV, kva_w)
    dx1_kv, got = _mm(dkva, wkva, "nt", F32, "kv_a_dx", jobs=[_PairExchange(g_wkva)])
    t_wkva = _pair_add(g_wkva, got, core, "rs_kv_w_a_add")

    def f_res0_bwd(r, v):
        dr1_, dxkv, dh, x1_, x_, o_ = r
        sc1, gate, g0 = v
        xh1, rstd1 = _ln_stats(x1_)
        dx1 = ALPHA * dr1_ + dxkv + _ln_bwd(dh * (1.0 + sc1), xh1, rstd1)
        xh0, rstd0 = _ln_stats(ALPHA * x_ + (1.0 + gate) * o_)
        dr0 = _ln_bwd(dx1 * g0, xh0, rstd0)
        return ([dr0, (1.0 + gate) * dr0],
                [_colsum(dh * xh1), _colsum(dh), _colsum(dx1 * xh0), _colsum(dx1), _colsum(dr0 * o_)])
    (dr0, do0b), (dscale1, dshift1, dlng0, dlnb0, dgate0) = _rowwise(
        f_res0_bwd, [(a, D, 0, False) for a in (dr1, dx1_kv, dh1, x1, x2, o0)], [vec(scale1), vec(gate0), vec(ln_g[0:1])],
        [(F32, D, D), (BF16, D, D)], [D] * 5, n_rows=S, tr=64, name="l0_residual_bwd")

    dp0, r_wkvb, r_wkva = _mm(do0b, wout0, "nt", F32, "l0_out_dx",
                              jobs=[_ChipExchange(t_wkvb), _ChipExchange(t_wkva)])
    g_wout0 = _mm(p0, do0b, "tn", BF16, "l0_out_dw").reshape(N_DEV, C // N_DEV, D)

    def f_gate0_bwd(r, v):
        dp, v2_, z = r
        g, b = v
        xh, rstd = _ln_stats(v2_)
        v3 = xh * g + b
        dv4 = dp * _silu(z)
        dz = dp * _silu(v3) * _dsilu(z)
        dv3 = dv4 * _dsilu(v3)
        dv2_ = _ln_bwd(dv3 * g, xh, rstd)
        return [dz, dv2_], [_colsum(dv3 * xh), _colsum(dv3), _colsum(dv2_)]
    (dz0, dv2), (dg_cn, db_cn, db_dw) = _rowwise(
        f_gate0_bwd, [(dp0, C, 0, False), (v2, C, 0, False), (u0, C, 2, False)], [vec(g_cn), vec(b_cn)],
        [(BF16, C, C), (F32, C, C)], [C, C, C], n_rows=S, tr=64, name="l0_gate_bwd")
    da0, dg0, dw_dw, r_wbin, got = _conv_bwd(u0, dv2, w_dw, S, C, "l0_conv_bwd",
                                             jobs=[_ChipExchange(t_wbin), _PairExchange(g_wout0)])
    t_wout0 = _pair_add(g_wout0, got, core, "rs_a_w_out_add")
    du0 = jnp.concatenate([da0, dg0, dz0], axis=1)
    g_win0_a, r_wout0 = _mm(h0, du0, "tn", BF16, "l0_in_dw_a", out_slots=True, m_rows=(0, D // 2),
                            jobs=[_ChipExchange(t_wout0)])
    g_win0_b, got = _mm(h0, du0, "tn", BF16, "l0_in_dw_b", out_slots=True, m_rows=(D // 2, D // 2),
                        jobs=[_PairExchange(g_win0_a)])
    t_win0_a = _pair_add(g_win0_a, got, core, "rs_a_w_in_add_a")
    dh0, r_win0_a, got = _mm(du0, win0, "nt", F32, "l0_in_dx", b_slots=True,
                             jobs=[_ChipExchange(t_win0_a), _PairExchange(g_win0_b)])
    t_win0_b = _pair_add(g_win0_b, got, core, "rs_a_w_in_add_b")

    def f_mod0_bwd(r, v):
        xh, rstd = _ln_stats(r[2])
        return [ALPHA * r[0] + _ln_bwd(r[1] * (1.0 + v[0]), xh, rstd)], [_colsum(r[1] * xh), _colsum(r[1])]
    (grad_x,), (dscale0, dshift0), (r_win0_b,) = _rowwise(
        f_mod0_bwd, [(dr0, D, 0, False), (dh0, D, 0, False), (x2, D, 0, False)], [vec(scale0)], [(F32, D, D)], [D, D],
        n_rows=S, tr=128, name="l0_modulate_bwd", jobs=[_ChipExchange(t_win0_b)])
    r_win0 = jnp.concatenate([r_win0_a, r_win0_b], axis=1)

    assert C == D and QL <= D and KL <= D
    n_small = 48
    singles = [dlng0, dlng1, dlnb0, dlnb1, dshift0, dscale0, dgate0, dshift1, dscale1, dgate1, db_dw, dg_cn, db_cn, dg_q, dg_kv]
    rows_loc = _pack_rows([(dw_dw, 0)] + [(a, v0 + i) for i, a in enumerate(singles)], n_small, D, "pack_small_grads")
    parts = _all_gather(rows_loc, "ag_small_grads")
    tot = _slot_sum(parts, "small_grads_sum")
    g_ln_g, g_ln_b = tot[v0:v0 + 2], tot[v0 + 2:v0 + 4]
    g_b_ada = tot[v0 + 4:v0 + 10].reshape(DEPTH, 3 * D)
    loc_cols = lambda a: lax.dynamic_slice(a, (0, idx * cl), (a.shape[0], cl))
    g_a_w_dw = loc_cols(tot[0:CONV_KERNEL])
    g_a_b_dw = loc_cols(tot[v0 + 10:v0 + 11])
    g_a_norm_g = loc_cols(tot[v0 + 11:v0 + 12])
    g_a_norm_b = loc_cols(tot[v0 + 12:v0 + 13])
    g_q_norm = tot[v0 + 13:v0 + 14, :QL]
    g_kv_norm = tot[v0 + 14:v0 + 15, :KL]

    g_w_ada = []
    for l in range(DEPTH):
        dmod = parts[:, v0 + 4 + 3 * l:v0 + 7 + 3 * l, :].reshape(N_DEV, 3 * D)
        dmod = lax.dynamic_slice(dmod, (0, idx * nl), (N_DEV, nl))
        dmod = jnp.concatenate([dmod, jnp.zeros((16 - N_DEV, nl), F32)], axis=0)
        g_w_ada.append(_mm(sc, dmod, "tn", F32, f"mod{l}_dw"))
    g_w_ada = jnp.stack(g_w_ada)

    r_wqb = r_wqb.reshape(N_CHIP, QL, HL, QPAD)[..., :NOPE + ROPE].reshape(N_CHIP, QL, HL * (NOPE + ROPE))
    r_wkva = r_wkva[..., :KL + ROPE]

    def upd(name, w, gparts, m, v):
        shp = w.shape
        r2 = (1, shp[0]) if len(shp) == 1 else (math.prod(shp[:-1]), shp[-1])
        gp = gparts.reshape((gparts.shape[0],) + r2)
        res = _adamw(w.reshape(r2), gp, m.reshape(r2), v.reshape(r2), "adamw_" + name)
        return [a.reshape(shp) for a in res]

    one = lambda g: g[None]
    results = [
        upd("w_ada", w_ada, one(g_w_ada), m_w_ada, v_w_ada),
        upd("b_ada", b_ada, one(g_b_ada), m_b_ada, v_b_ada),
        upd("ln_g", ln_g, one(g_ln_g), m_ln_g, v_ln_g),
        upd("ln_b", ln_b, one(g_ln_b), m_ln_b, v_ln_b),
        upd("a_w_in", a_w_in, r_win0[:, None], m_a_w_in, v_a_w_in),
        upd("a_w_dw", a_w_dw, one(g_a_w_dw[None]), m_a_w_dw, v_a_w_dw),
        upd("a_b_dw", a_b_dw, one(g_a_b_dw), m_a_b_dw, v_a_b_dw),
        upd("a_norm_g", a_norm_g, one(g_a_norm_g), m_a_norm_g, v_a_norm_g),
        upd("a_norm_b", a_norm_b, one(g_a_norm_b), m_a_norm_b, v_a_norm_b),
        upd("a_w_out", a_w_out, r_wout0[:, None], m_a_w_out, v_a_w_out),
        upd("b_w_in", b_w_in, r_wbin[:, None], m_b_w_in, v_b_w_in),
        upd("b_q_norm_g", b_q_norm_g, one(g_q_norm), m_b_q_norm_g, v_b_q_norm_g),
        upd("b_w_qb", b_w_qb, r_wqb[:, None], m_b_w_qb, v_b_w_qb),
        upd("b_w_out", b_w_out, r_wbout[:, None], m_b_w_out, v_b_w_out),
        upd("kv_w_a", kv_w_a, r_wkva, m_kv_w_a, v_kv_w_a),
        upd("kv_norm_g", kv_norm_g, one(g_kv_norm[0]), m_kv_norm_g, v_kv_norm_g),
        upd("kv_w_b", kv_w_b, r_wkvb, m_kv_w_b, v_kv_w_b),
    ]
    grads, deltas, new_m, new_v = zip(*results)
    return (loss, grad_x[None], *grads, *deltas, *new_m, *new_v)
```

```python
import functools
import math

import jax
import jax.numpy as jnp
from jax import lax
from jax.experimental import pallas as pl
from jax.experimental.pallas import tpu as pltpu

F32 = jnp.float32
BF16 = jnp.bfloat16
MESH = pl.DeviceIdType.MESH
ANY = pl.BlockSpec(memory_space=pl.ANY)

N_DEV = 8
N_CHIP = 4
V7X_VMEM_BYTES = 64 * 1024 * 1024
VMEM_LIMIT = V7X_VMEM_BYTES - 8 * 1024 * 1024
LANES = 128
SUBLANES = 8

DEPTH = 2
CONV_KERNEL = 31
NOPE = 128
ROPE = 64
VDIM = 128
QPAD = NOPE + LANES
ROPE_BASE = 10000.0
LN_EPS = 1e-5
RMS_EPS = 1e-6
ALPHA = (2.0 * DEPTH) ** 0.25
ATTN_SCALE = (NOPE + ROPE) ** -0.5
ADAM_LR = 0.001
ADAM_B1 = 0.9
ADAM_B2 = 0.999
ADAM_EPS = 1e-08
ADAM_WD = 0.01
ADAM_STEP = 10

CONV_TOP = 32
CONV_ROWS = 64
ATTN_BLOCK = 512
ATTN_CHUNK = 32
ROPE_HEADS = 8
MM_INPUT_BYTES = 24 * 1024 * 1024
ADAMW_TILE_ELEMS = 256 * 1024


def _pick(dim, prefs):
    for p in prefs:
        if dim % p == 0:
            return p
    return dim


def _params(sem, **kw):
    return pltpu.CompilerParams(dimension_semantics=sem, vmem_limit_bytes=VMEM_LIMIT, **kw)


_DIMS = {"nn": (((1,), (0,)), ((), ())), "nt": (((1,), (1,)), ((), ())), "tn": (((0,), (0,)), ((), ()))}


def _mm(a, b, mode, out_dtype, name, *, b_slots=False, out_slots=False, m_rows=None, jobs=()):
    m_first = 0
    if mode == "tn":
        K, M = a.shape
        if m_rows is not None:
            m_first, M = m_rows
    else:
        M, K = a.shape
    if b_slots:
        assert mode in ("nn", "nt")
        if mode == "nn":
            _, kb, n_loc = b.shape
            N, slot_w = N_DEV * n_loc, n_loc
        else:
            _, N, k_loc = b.shape
            kb, slot_w = N_DEV * k_loc, k_loc
    else:
        if mode == "nt":
            N, kb = b.shape
        else:
            kb, N = b.shape
        slot_w = N // N_DEV if out_slots else None
    assert kb == K, (name, a.shape, b.shape)
    tm = _pick(M, (1024, 512, 256, 128, 64, 32, 16))
    n_unit = slot_w if (mode == "nn" and b_slots) or out_slots else N
    k_unit = slot_w if (mode == "nt" and b_slots) else K
    tn = _pick(n_unit, (1024, 1152, 768, 640, 512, 384, 256, 128))
    step_bytes = 2 * (tm * a.dtype.itemsize + tn * b.dtype.itemsize)
    tk = _pick(k_unit, [t for t in (4096, 2048, 1536, 1152, 1024, 768, 640, 512, 384, 256, 128, 64, 32, 16)
                        if t * step_bytes <= MM_INPUT_BYTES])
    nk = K // tk

    if mode == "tn":
        assert m_first % tm == 0
        a_spec = pl.BlockSpec((tk, tm), lambda i, j, k: (k, i + m_first // tm))
    else:
        a_spec = pl.BlockSpec((tm, tk), lambda i, j, k: (i, k))
    if b_slots and mode == "nn":
        per = slot_w // tn
        b_spec = pl.BlockSpec((None, tk, tn), lambda i, j, k: (j // per, k, j % per))
    elif b_slots and mode == "nt":
        per = slot_w // tk
        b_spec = pl.BlockSpec((None, tn, tk), lambda i, j, k: (k // per, j, k % per))
    elif mode == "nt":
        b_spec = pl.BlockSpec((tn, tk), lambda i, j, k: (j, k))
    else:
        b_spec = pl.BlockSpec((tk, tn), lambda i, j, k: (k, j))
    if out_slots:
        per_o = slot_w // tn
        o_spec = pl.BlockSpec((None, tm, tn), lambda i, j, k: (j // per_o, i, j % per_o))
        o_shape = jax.ShapeDtypeStruct((N_DEV, M, slot_w), out_dtype)
    else:
        o_spec = pl.BlockSpec((tm, tn), lambda i, j, k: (i, j))
        o_shape = jax.ShapeDtypeStruct((M, N), out_dtype)
    dims = _DIMS[mode]

    def body(a_ref, b_ref, o_ref, acc_ref):
        k = pl.program_id(2)

        @pl.when(k == 0)
        def _():
            acc_ref[...] = jnp.zeros_like(acc_ref)

        acc_ref[...] += lax.dot_general(a_ref[...].astype(BF16), b_ref[...].astype(BF16), dims,
                                        preferred_element_type=F32)

        @pl.when(k == nk - 1)
        def _():
            o_ref[...] = acc_ref[...].astype(out_dtype)

    grid = (M // tm, N // tn, nk)
    j_ins, j_in_specs, j_out_specs, j_outs, j_sems = _carry_call(jobs)
    res = pl.pallas_call(
        _carry(body, jobs, 2, 1, grid), name=name, grid=grid, in_specs=[a_spec, b_spec] + j_in_specs,
        out_specs=[o_spec] + j_out_specs, out_shape=[o_shape] + j_outs,
        scratch_shapes=[pltpu.VMEM((tm, tn), F32)] + j_sems,
        compiler_params=_params(("arbitrary",) * 3 if jobs else ("parallel", "parallel", "arbitrary"),
                                has_side_effects=bool(jobs)),
    )(a, b, *j_ins)
    return tuple(res) if jobs else res[0]


def _rowwise(fn, rows, vecs, outs, accs, *, n_rows, tr, name, nc=1, jobs=()):
    nr = n_rows // tr
    n_in, n_vec, n_out, n_acc = len(rows), len(vecs), len(outs), len(accs)

    def cmap(cb0, follow):
        if follow:
            return lambda j, i: (i, cb0 + j)
        return lambda j, i: (i, cb0)

    def vmap_(cb0, follow):
        if follow:
            return lambda j, i: (0, cb0 + j)
        return lambda j, i: (0, cb0)

    in_specs = [pl.BlockSpec((tr, w), cmap(cb0, fol)) for (_, w, cb0, fol) in rows]
    in_specs += [pl.BlockSpec((1, w), vmap_(cb0, fol)) for (_, w, cb0, fol) in vecs]
    out_specs = [pl.BlockSpec((tr, bw), lambda j, i: (i, j)) for (_, _, bw) in outs]
    out_specs += [pl.BlockSpec((1, w), lambda j, i: (0, j)) for w in accs]
    out_shape = [jax.ShapeDtypeStruct((n_rows, tw), dt) for (dt, tw, _) in outs]
    out_shape += [jax.ShapeDtypeStruct((1, w * nc), F32) for w in accs]

    def body(*refs):
        ins = [r[...] for r in refs[:n_in]]
        vs = [r[...] for r in refs[n_in:n_in + n_vec]]
        o_refs = refs[n_in + n_vec:n_in + n_vec + n_out]
        a_refs = refs[n_in + n_vec + n_out:]
        o_vals, a_vals = fn(ins, vs)
        for r, v in zip(o_refs, o_vals):
            r[...] = v.astype(r.dtype)
        if n_acc:
            i = pl.program_id(1)

            @pl.when(i == 0)
            def _():
                for r in a_refs:
                    r[...] = jnp.zeros_like(r)

            for r, v in zip(a_refs, a_vals):
                r[...] += v

    res = _grid_call(
        body, [r[0] for r in rows] + [v[0] for v in vecs], name=name, grid=(nc, nr), in_specs=in_specs,
        out_specs=out_specs, out_shape=out_shape, scratch_shapes=[], sem=("parallel", "arbitrary"), jobs=jobs)
    outs, accs_out, extra = res[:n_out], res[n_out:n_out + n_acc], res[n_out + n_acc:]
    return (outs, accs_out, extra) if jobs else (outs, accs_out)


def _colsum(x):
    return jnp.sum(x, axis=0, keepdims=True)


def _ln_stats(x):
    mu = jnp.mean(x, axis=-1, keepdims=True)
    xc = x - mu
    var = jnp.mean(xc * xc, axis=-1, keepdims=True)
    rstd = lax.rsqrt(var + LN_EPS)
    return xc * rstd, rstd


def _ln_bwd(dxhat, xhat, rstd):
    return rstd * (dxhat - jnp.mean(dxhat, axis=-1, keepdims=True)
                   - xhat * jnp.mean(dxhat * xhat, axis=-1, keepdims=True))


def _rms_stats(x):
    r = lax.rsqrt(jnp.mean(x * x, axis=-1, keepdims=True) + RMS_EPS)
    return x * r, r


def _rms_bwd(dxhat, xhat, r):
    return r * (dxhat - xhat * jnp.mean(dxhat * xhat, axis=-1, keepdims=True))


def _silu(x):
    return x * jax.nn.sigmoid(x)


def _dsilu(x):
    s = jax.nn.sigmoid(x)
    return s * (1.0 + x * (1.0 - s))


def _rope_fwd(x, cos_t, sin_a, sin_b):
    return x * cos_t + pltpu.roll(x, LANES - ROPE // 2, 1) * sin_a + pltpu.roll(x, ROPE // 2, 1) * sin_b


def _rope_bwd(dy, cos_t, sin_a, sin_b):
    return dy * cos_t + pltpu.roll(dy * sin_a, ROPE // 2, 1) + pltpu.roll(dy * sin_b, LANES - ROPE // 2, 1)


def _rope_tables(seq):
    half = ROPE // 2
    inv_freq = ROPE_BASE ** (-jnp.arange(half, dtype=F32) / half)
    ang = jnp.arange(seq, dtype=jnp.int32).astype(F32)[:, None] * inv_freq[None, :]
    cos, sin = jnp.cos(ang), jnp.sin(ang)
    z = jnp.zeros_like(cos)
    cos_t = jnp.concatenate([cos, cos, z, z], axis=1)
    sin_a = jnp.concatenate([-sin, z, z, z], axis=1)
    sin_b = jnp.concatenate([z, sin, z, z], axis=1)
    return cos_t, sin_a, sin_b


def _shifted(win, n):
    return [win] + [pltpu.roll(win, n - j, 0) for j in range(1, SUBLANES)]


def _conv_fwd(u0, w_dw, b_dw, seq, width, name):
    tc = LANES
    nct = width // tc
    rows = CONV_ROWS
    n_win = rows + CONV_TOP

    def body(a_ref, g_ref, w_ref, b_ref, o_ref, vpad):
        vpad[0:CONV_TOP, :] = jnp.zeros((CONV_TOP, tc), F32)
        vpad[CONV_TOP:, :] = a_ref[...] * jax.nn.sigmoid(g_ref[...])

        def chunk(i, carry):
            t0 = pl.multiple_of(i * rows, rows)
            rot = _shifted(vpad[pl.ds(t0, n_win), :], n_win)
            acc = jnp.zeros((rows, tc), F32) + b_ref[...]
            for k in range(CONV_KERNEL):
                s = k + CONV_TOP - (CONV_KERNEL - 1)
                q, j = divmod(s, SUBLANES)
                acc = acc + w_ref[k:k + 1, :] * rot[j][SUBLANES * q:SUBLANES * q + rows]
            o_ref[pl.ds(t0, rows), :] = acc
            return carry

        lax.fori_loop(0, seq // rows, chunk, 0)

    return pl.pallas_call(
        body, name=name, grid=(nct,),
        in_specs=[pl.BlockSpec((seq, tc), lambda j: (0, j)), pl.BlockSpec((seq, tc), lambda j: (0, nct + j)),
                  pl.BlockSpec((CONV_KERNEL, tc), lambda j: (0, j)), pl.BlockSpec((1, tc), lambda j: (0, j))],
        out_specs=pl.BlockSpec((seq, tc), lambda j: (0, j)),
        out_shape=jax.ShapeDtypeStruct((seq, width), F32),
        scratch_shapes=[pltpu.VMEM((seq + CONV_TOP, tc), F32)],
        compiler_params=_params(("parallel",)),
    )(u0, u0, w_dw, b_dw)


def _conv_bwd(u0, dv2, w_dw, seq, width, name, jobs=()):
    tc = LANES
    nct = width // tc
    rows = CONV_ROWS
    n_win = rows + CONV_TOP
    nsh = CONV_KERNEL * SUBLANES

    def body(a_ref, g_ref, d_ref, w_ref, da_ref, dg_ref, dw_ref, vpad, dpad, dwacc):
        vpad[0:CONV_TOP, :] = jnp.zeros((CONV_TOP, tc), F32)
        vpad[CONV_TOP:, :] = a_ref[...] * jax.nn.sigmoid(g_ref[...])
        dpad[0:seq, :] = d_ref[...]
        dpad[seq:, :] = jnp.zeros((CONV_TOP, tc), F32)
        dwacc[...] = jnp.zeros_like(dwacc)

        def chunk(i, carry):
            t0 = pl.multiple_of(i * rows, rows)
            vrot = _shifted(vpad[pl.ds(t0, n_win), :], n_win)
            drot = _shifted(dpad[pl.ds(t0, n_win), :], n_win)
            dcur = drot[0][0:rows]
            dv1 = jnp.zeros((rows, tc), F32)
            for k in range(CONV_KERNEL):
                q, j = divmod(CONV_KERNEL - 1 - k, SUBLANES)
                dv1 = dv1 + w_ref[k:k + 1, :] * drot[j][SUBLANES * q:SUBLANES * q + rows]
                q, j = divmod(k + CONV_TOP - (CONV_KERNEL - 1), SUBLANES)
                prod = dcur * vrot[j][SUBLANES * q:SUBLANES * q + rows]
                part = prod[0:SUBLANES]
                for r in range(1, rows // SUBLANES):
                    part = part + prod[SUBLANES * r:SUBLANES * (r + 1)]
                dwacc[SUBLANES * k:SUBLANES * (k + 1), :] += part
            a = a_ref[pl.ds(t0, rows), :]
            sg = jax.nn.sigmoid(g_ref[pl.ds(t0, rows), :])
            da_ref[pl.ds(t0, rows), :] = (dv1 * sg).astype(BF16)
            dg_ref[pl.ds(t0, rows), :] = (dv1 * a * sg * (1.0 - sg)).astype(BF16)
            return carry

        lax.fori_loop(0, seq // rows, chunk, 0)
        for k in range(CONV_KERNEL):
            dw_ref[k:k + 1, :] = jnp.sum(dwacc[SUBLANES * k:SUBLANES * (k + 1), :], axis=0, keepdims=True)

    return _grid_call(
        body, (u0, u0, dv2, w_dw), name=name, grid=(nct,),
        in_specs=[pl.BlockSpec((seq, tc), lambda j: (0, j)), pl.BlockSpec((seq, tc), lambda j: (0, nct + j)),
                  pl.BlockSpec((seq, tc), lambda j: (0, j)), pl.BlockSpec((CONV_KERNEL, tc), lambda j: (0, j))],
        out_specs=[pl.BlockSpec((seq, tc), lambda j: (0, j)), pl.BlockSpec((seq, tc), lambda j: (0, j)),
                   pl.BlockSpec((CONV_KERNEL, tc), lambda j: (0, j))],
        out_shape=[jax.ShapeDtypeStruct((seq, width), BF16), jax.ShapeDtypeStruct((seq, width), BF16),
                   jax.ShapeDtypeStruct((CONV_KERNEL, width), F32)],
        scratch_shapes=[pltpu.VMEM((seq + CONV_TOP, tc), F32), pltpu.VMEM((seq + CONV_TOP, tc), F32),
                        pltpu.VMEM((nsh, tc), F32)],
        sem=("parallel",), jobs=jobs)


def _lanes(x, n):
    return jnp.tile(x, (1, n // LANES))


def _diag_mask(r0, rows, t, transposed=False):
    r = lax.broadcasted_iota(jnp.int32, (rows, t), 0) + r0
    c = lax.broadcasted_iota(jnp.int32, (rows, t), 1)
    return (c >= r) if transposed else (r >= c)


def _attn_fwd(q, kv, kr, seq, heads, name, jobs=()):
    t = min(ATTN_BLOCK, seq)
    nb = seq // t
    ch = ATTN_CHUNK

    def body(q_ref, kv_ref, kr_ref, o_ref, lse_ref, kf_s, s_s, p_s, m_s, l_s, c_s, acc_s):
        qi = pl.program_id(1)

        @pl.when(qi == 0)
        def _():
            kf_s[:, 0:NOPE] = kv_ref[:, 0:NOPE]
            kf_s[:, NOPE:] = kr_ref[...]

        m_s[...] = jnp.full_like(m_s, -jnp.inf)
        l_s[...] = jnp.zeros_like(l_s)
        acc_s[...] = jnp.zeros_like(acc_s)

        def block(ki, masked):
            k0 = pl.multiple_of(ki * t, t)
            s_s[...] = lax.dot_general(q_ref[...], kf_s[pl.ds(k0, t), :], _DIMS["nt"], preferred_element_type=F32)

            def chunk(c, carry):
                r0 = pl.multiple_of(c * ch, ch)
                rows = pl.ds(r0, ch)
                s_c = s_s[rows, :]
                if masked:
                    s_c = jnp.where(_diag_mask(r0, ch, t), s_c, -jnp.inf)
                m_prev = m_s[rows, :]
                m_new = jnp.maximum(m_prev, jnp.max(s_c, axis=1, keepdims=True))
                p = jnp.exp(s_c - _lanes(m_new, t))
                corr = jnp.exp(m_prev - m_new)
                l_s[rows, :] = corr * l_s[rows, :] + jnp.sum(p, axis=1, keepdims=True)
                m_s[rows, :] = m_new
                c_s[rows, :] = corr
                p_s[rows, :] = p.astype(BF16)
                return carry

            lax.fori_loop(0, t // ch, chunk, 0, unroll=True)
            acc_s[...] = c_s[...] * acc_s[...] + jnp.dot(p_s[...], kv_ref[pl.ds(k0, t), pl.ds(NOPE, VDIM)],
                                                         preferred_element_type=F32)

        def below(ki, carry):
            block(ki, False)
            return carry

        lax.fori_loop(0, qi, below, 0)
        block(qi, True)
        o_ref[...] = acc_s[...] / l_s[...]
        lse_ref[...] = m_s[...] + jnp.log(l_s[...])

    return _grid_call(
        body, (q, kv, kr), name=name, grid=(heads, nb),
        in_specs=[pl.BlockSpec((t, QPAD), lambda h, i: (i, h)),
                  pl.BlockSpec((seq, QPAD), lambda h, i: (0, h)),
                  pl.BlockSpec((seq, LANES), lambda h, i: (0, 0))],
        out_specs=[pl.BlockSpec((t, VDIM), lambda h, i: (i, h)),
                   pl.BlockSpec((t, LANES), lambda h, i: (i, h))],
        out_shape=[jax.ShapeDtypeStruct((seq, heads * VDIM), F32), jax.ShapeDtypeStruct((seq, heads * LANES), F32)],
        scratch_shapes=[pltpu.VMEM((seq, QPAD), BF16), pltpu.VMEM((t, t), F32), pltpu.VMEM((t, t), BF16),
                        pltpu.VMEM((t, LANES), F32), pltpu.VMEM((t, LANES), F32), pltpu.VMEM((t, LANES), F32),
                        pltpu.VMEM((t, VDIM), F32)],
        sem=("parallel", "arbitrary"), jobs=jobs)


def _attn_bwd(q, kv, kr, do, o, lse, seq, heads, name, jobs=()):
    t = min(ATTN_BLOCK, seq)
    nb = seq // t
    ch = ATTN_CHUNK

    def body(q_ref, kv_ref, kr_ref, do_ref, o_ref, lse_ref, dq_ref, dkv_ref, dkr_ref,
             kf_s, qt_s, dot_s, dl_s, s_s, dp_s, p_s, ds_s, dkt_s, dvt_s):
        @pl.when(pl.program_id(0) == 0)
        def _():
            dkr_ref[...] = jnp.zeros_like(dkr_ref)

        kf_s[:, 0:NOPE] = kv_ref[:, 0:NOPE]
        kf_s[:, NOPE:] = kr_ref[...]
        dq_ref[...] = jnp.zeros_like(dq_ref)
        for b in range(nb):
            rows = pl.ds(b * t, t)
            dob = do_ref[rows, :].astype(F32)
            qt_s[b] = jnp.transpose(q_ref[rows, :].astype(F32)).astype(BF16)
            dot_s[b] = jnp.transpose(dob).astype(BF16)
            dl_s[rows, :] = jnp.broadcast_to(jnp.sum(dob * o_ref[rows, :], axis=1, keepdims=True), (t, LANES))

        def key_block(ki, carry):
            k0 = pl.multiple_of(ki * t, t)
            kfb = kf_s[pl.ds(k0, t), :]
            dkt_s[...] = jnp.zeros_like(dkt_s)
            dvt_s[...] = jnp.zeros_like(dvt_s)

            def block(qi, masked):
                q0 = pl.multiple_of(qi * t, t)
                s_s[...] = lax.dot_general(q_ref[pl.ds(q0, t), :], kfb, _DIMS["nt"], preferred_element_type=F32)
                dp_s[...] = lax.dot_general(do_ref[pl.ds(q0, t), :], kv_ref[pl.ds(k0, t), pl.ds(NOPE, VDIM)],
                                            _DIMS["nt"], preferred_element_type=F32)

                def chunk(c, carry):
                    r0 = pl.multiple_of(c * ch, ch)
                    rows = pl.ds(r0, ch)
                    grows = pl.ds(pl.multiple_of(q0 + r0, ch), ch)
                    p = jnp.exp(s_s[rows, :] - _lanes(lse_ref[grows, :], t))
                    if masked:
                        p = jnp.where(_diag_mask(r0, ch, t), p, 0.0)
                    p_s[rows, :] = p.astype(BF16)
                    ds_s[rows, :] = (p * (dp_s[rows, :] - _lanes(dl_s[grows, :], t))).astype(BF16)
                    return carry

                lax.fori_loop(0, t // ch, chunk, 0, unroll=True)
                dq_ref[pl.ds(q0, t), :] += jnp.dot(ds_s[...], kfb, preferred_element_type=F32)
                dvt_s[...] += jnp.dot(dot_s[qi], p_s[...], preferred_element_type=F32)
                dkt_s[...] += jnp.dot(qt_s[qi], ds_s[...], preferred_element_type=F32)

            def above(qi, carry):
                block(qi, False)
                return carry

            block(ki, True)
            lax.fori_loop(ki + 1, nb, above, 0)
            dk = jnp.transpose(dkt_s[...])
            dkv_ref[pl.ds(k0, t), :] = jnp.concatenate([dk[:, :NOPE], jnp.transpose(dvt_s[...])], axis=1).astype(BF16)
            dkr_ref[pl.ds(k0, t), :] += dk[:, NOPE:]
            return carry

        lax.fori_loop(0, nb, key_block, 0)

    head = lambda w: pl.BlockSpec((seq, w), lambda h: (0, h))
    shared = pl.BlockSpec((seq, LANES), lambda h: (0, 0))
    return _grid_call(
        body, (q, kv, kr, do, o, lse), name=name, grid=(heads,),
        in_specs=[head(QPAD), head(QPAD), shared, head(VDIM), head(VDIM), head(LANES)],
        out_specs=[head(QPAD), head(QPAD), shared],
        out_shape=[jax.ShapeDtypeStruct((seq, heads * QPAD), F32), jax.ShapeDtypeStruct((seq, heads * QPAD), BF16),
                   jax.ShapeDtypeStruct((seq, LANES), F32)],
        scratch_shapes=[pltpu.VMEM((seq, QPAD), BF16), pltpu.VMEM((nb, QPAD, t), BF16), pltpu.VMEM((nb, VDIM, t), BF16),
                        pltpu.VMEM((seq, LANES), F32), pltpu.VMEM((t, t), F32), pltpu.VMEM((t, t), F32),
                        pltpu.VMEM((t, t), BF16), pltpu.VMEM((t, t), BF16), pltpu.VMEM((QPAD, t), F32),
                        pltpu.VMEM((VDIM, t), F32)],
        sem=("arbitrary",), jobs=jobs)


def _place():
    x, y, c = lax.axis_index("x"), lax.axis_index("y"), lax.axis_index("c")
    return x, y, c


class _Gather:
    def __init__(self, shard):
        r, n = shard.shape
        self.ins = [shard]
        self.outs = [jax.ShapeDtypeStruct((N_DEV, r, n), shard.dtype)]
        self.sems = [pltpu.SemaphoreType.DMA((7,)), pltpu.SemaphoreType.DMA((7,)), pltpu.SemaphoreType.DMA]

    def bind(self, ins, outs, sems):
        (self.x_ref,), (self.out_ref,) = ins, outs
        self.send, self.recv, self.local = sems

    def _plan(self):
        x, y, c = _place()
        chips = [(1 - x, y), (x, 1 - y), (1 - x, 1 - y)]

        def copy(k, block, to, from_shard=False):
            px, py, pc = block
            slot = self.out_ref.at[4 * px + 2 * py + pc]
            return pltpu.make_async_remote_copy(
                src_ref=self.x_ref if from_shard else slot, dst_ref=slot, send_sem=self.send.at[k],
                recv_sem=self.recv.at[k], device_id=to, device_id_type=MESH)

        me, sibling = (x, y, c), (x, y, 1 - c)
        mine = pltpu.make_async_copy(self.x_ref, self.out_ref.at[4 * x + 2 * y + c], self.local)
        first = [copy(0, me, sibling, True)] + [copy(1 + j, me, (*chip, c), True) for j, chip in enumerate(chips)]
        landed = [copy(1 + j, (*chip, c), me) for j, chip in enumerate(chips)]
        passed = [copy(4 + j, (*chip, c), sibling) for j, chip in enumerate(chips)]
        from_sibling = [copy(0, sibling, me)] + [copy(4 + j, (*chip, 1 - c), me) for j, chip in enumerate(chips)]
        return mine, first, landed, passed, from_sibling

    def start(self):
        mine, first, _, _, _ = self._plan()
        mine.start()
        for cp in first:
            cp.start()

    def mid(self):
        _, _, landed, passed, _ = self._plan()
        for got, fwd in zip(landed, passed):
            got.wait_recv()
            fwd.start()

    def finish(self):
        mine, first, _, passed, from_sibling = self._plan()
        for cp in from_sibling:
            cp.wait_recv()
        for cp in first + passed:
            cp.wait_send()
        mine.wait()


class _PairExchange:
    def __init__(self, g8):
        _, r, n = g8.shape
        self.ins = [g8]
        self.outs = [jax.ShapeDtypeStruct((N_CHIP, r, n), g8.dtype)]
        self.sems = [pltpu.SemaphoreType.DMA((N_CHIP,)), pltpu.SemaphoreType.DMA((N_CHIP,))]

    def bind(self, ins, outs, sems):
        (self.g_ref,), (self.out_ref,) = ins, outs
        self.send, self.recv = sems

    def _plan(self):
        x, y, c = _place()
        return [pltpu.make_async_remote_copy(
            src_ref=self.g_ref.at[2 * p + (1 - c)], dst_ref=self.out_ref.at[p], send_sem=self.send.at[p],
            recv_sem=self.recv.at[p], device_id=(x, y, 1 - c), device_id_type=MESH) for p in range(N_CHIP)]

    def start(self):
        for cp in self._plan():
            cp.start()

    def mid(self):
        pass

    def finish(self):
        copies = self._plan()
        for cp in copies:
            cp.wait_recv()
        for cp in copies:
            cp.wait_send()


class _ChipExchange:
    def __init__(self, t4):
        _, r, n = t4.shape
        self.ins = [t4]
        self.outs = [jax.ShapeDtypeStruct((N_CHIP, r, n), t4.dtype)]
        self.sems = [pltpu.SemaphoreType.DMA((3,)), pltpu.SemaphoreType.DMA((3,)), pltpu.SemaphoreType.DMA]

    def bind(self, ins, outs, sems):
        (self.t_ref,), (self.out_ref,) = ins, outs
        self.send, self.recv, self.local_sem = sems

    def _plan(self):
        x, y, c = _place()
        mine = 2 * x + y
        chips = [(1 - x, y), (x, 1 - y), (1 - x, 1 - y)]
        local = pltpu.make_async_copy(self.t_ref.at[mine], self.out_ref.at[mine], self.local_sem)
        sends = [pltpu.make_async_remote_copy(
            src_ref=self.t_ref.at[2 * px + py], dst_ref=self.out_ref.at[mine], send_sem=self.send.at[k],
            recv_sem=self.recv.at[k], device_id=(px, py, c), device_id_type=MESH) for k, (px, py) in enumerate(chips)]
        arrivals = [pltpu.make_async_remote_copy(
            src_ref=self.t_ref.at[mine], dst_ref=self.out_ref.at[2 * px + py], send_sem=self.send.at[k],
            recv_sem=self.recv.at[k], device_id=(px, py, c), device_id_type=MESH) for k, (px, py) in enumerate(chips)]
        return local, sends, arrivals

    def start(self):
        local, sends, _ = self._plan()
        local.start()
        for cp in sends:
            cp.start()

    def mid(self):
        pass

    def finish(self):
        local, sends, arrivals = self._plan()
        for cp in arrivals:
            cp.wait_recv()
        for cp in sends:
            cp.wait_send()
        local.wait()


def _job_specs(jobs):
    ins = [a for j in jobs for a in j.ins]
    outs = [o for j in jobs for o in j.outs]
    sems = [s for j in jobs for s in j.sems]
    return ins, outs, sems


def _bind_jobs(jobs, ins, outs, sems):
    for j in jobs:
        ni, no, ns = len(j.ins), len(j.outs), len(j.sems)
        j.bind(ins[:ni], outs[:no], sems[:ns])
        ins, outs, sems = ins[ni:], outs[no:], sems[ns:]


def _carry(body, jobs, n_in, n_out, grid):
    if not jobs:
        return body
    j_in, j_out, j_sem = (len(v) for v in _job_specs(jobs))
    total = math.prod(grid)
    mid_step = (3 * total) // 4

    def wrapped(*refs):
        ins, rest = refs[:n_in], refs[n_in:]
        jin, rest = rest[:j_in], rest[j_in:]
        outs, rest = rest[:n_out], rest[n_out:]
        jout, rest = rest[:j_out], rest[j_out:]
        scratch, jsem = rest[:len(rest) - j_sem], rest[len(rest) - j_sem:]
        _bind_jobs(jobs, jin, jout, jsem)
        lin = 0
        for d, size in enumerate(grid):
            lin = lin * size + pl.program_id(d)

        @pl.when(lin == 0)
        def _():
            for j in jobs:
                j.start()

        body(*ins, *outs, *scratch)

        @pl.when(lin == mid_step)
        def _():
            for j in jobs:
                j.mid()

        @pl.when(lin == total - 1)
        def _():
            for j in jobs:
                j.finish()

    return wrapped


def _carry_call(jobs):
    ins, outs, sems = _job_specs(jobs)
    return ins, [ANY] * len(ins), [ANY] * len(outs), outs, sems


def _grid_call(body, operands, *, name, grid, in_specs, out_specs, out_shape, scratch_shapes, sem, jobs=()):
    j_ins, j_in_specs, j_out_specs, j_outs, j_sems = _carry_call(jobs)
    return pl.pallas_call(
        _carry(body, jobs, len(in_specs), len(out_specs), grid), name=name, grid=grid,
        in_specs=list(in_specs) + j_in_specs, out_specs=list(out_specs) + j_out_specs,
        out_shape=list(out_shape) + j_outs, scratch_shapes=list(scratch_shapes) + j_sems,
        compiler_params=_params(("arbitrary",) * len(grid) if jobs else sem, has_side_effects=bool(jobs)),
    )(*operands, *j_ins)


def _comm(jobs, name):
    ins, outs, sems = _job_specs(jobs)

    def body(*refs):
        _bind_jobs(jobs, refs[:len(ins)], refs[len(ins):len(ins) + len(outs)], refs[len(ins) + len(outs):])
        for j in jobs:
            j.start()
        for j in jobs:
            j.mid()
        for j in jobs:
            j.finish()

    return pl.pallas_call(
        body, name=name, in_specs=[ANY] * len(ins), out_specs=[ANY] * len(outs), out_shape=outs,
        scratch_shapes=sems, compiler_params=pltpu.CompilerParams(has_side_effects=True),
    )(*ins)


def _all_gather(shard, name):
    return _comm([_Gather(shard)], name)[0]


def _pair_add(g8, got, core, name):
    _, r, n = g8.shape
    tr = _pick(r, (256, 128, 64, 32, 16, 8))

    def body(c_ref, g_ref, r_ref, o_ref):
        o_ref[...] = (g_ref[...].astype(F32) + r_ref[...].astype(F32)).astype(o_ref.dtype)

    grid_spec = pltpu.PrefetchScalarGridSpec(
        num_scalar_prefetch=1, grid=(N_CHIP, r // tr),
        in_specs=[pl.BlockSpec((None, tr, n), lambda p, i, c: (2 * p + c[0], i, 0)),
                  pl.BlockSpec((None, tr, n), lambda p, i, c: (p, i, 0))],
        out_specs=pl.BlockSpec((None, tr, n), lambda p, i, c: (p, i, 0)))
    return pl.pallas_call(
        body, name=name, grid_spec=grid_spec, out_shape=jax.ShapeDtypeStruct((N_CHIP, r, n), g8.dtype),
        compiler_params=_params(("parallel", "parallel")),
    )(core, g8, got)


def _slot_sum(g, name):
    k, r, n = g.shape

    def body(g_ref, o_ref):
        acc = g_ref[0].astype(F32)
        for s in range(1, k):
            acc = acc + g_ref[s].astype(F32)
        o_ref[...] = acc

    return pl.pallas_call(body, name=name, out_shape=jax.ShapeDtypeStruct((r, n), F32),
                          compiler_params=pltpu.CompilerParams(vmem_limit_bytes=VMEM_LIMIT))(g)


def _pack_rows(parts, n_rows, width, name):
    starts = [r0 for _, r0 in parts]

    def body(*refs):
        o_ref = refs[-1]
        o_ref[...] = jnp.zeros_like(o_ref)
        for ref, r0 in zip(refs[:-1], starts):
            k, w = ref.shape
            o_ref[r0:r0 + k, 0:w] = ref[...]

    return pl.pallas_call(body, name=name, out_shape=jax.ShapeDtypeStruct((n_rows, width), F32),
                          compiler_params=pltpu.CompilerParams(vmem_limit_bytes=VMEM_LIMIT))(*[a for a, _ in parts])


def _adamw(w, gparts, m, v, name):
    r, n = w.shape
    k = gparts.shape[0]
    tr = _pick(r, [t for t in (512, 256, 128, 64, 32, 16) if t * n <= ADAMW_TILE_ELEMS])
    c1 = 1.0 - ADAM_B1 ** ADAM_STEP
    c2 = 1.0 - ADAM_B2 ** ADAM_STEP

    def body(w_ref, g_ref, m_ref, v_ref, go_ref, d_ref, mo_ref, vo_ref):
        g = g_ref[0].astype(F32)
        for s in range(1, k):
            g = g + g_ref[s].astype(F32)
        mn = ADAM_B1 * m_ref[...] + (1.0 - ADAM_B1) * g
        vn = ADAM_B2 * v_ref[...] + (1.0 - ADAM_B2) * (g * g)
        go_ref[...] = g
        mo_ref[...] = mn
        vo_ref[...] = vn
        d_ref[...] = -ADAM_LR * ((mn / c1) / (jnp.sqrt(vn / c2) + ADAM_EPS) + ADAM_WD * w_ref[...])

    spec = pl.BlockSpec((tr, n), lambda i: (i, 0))
    return pl.pallas_call(
        body, name=name, grid=(r // tr,),
        in_specs=[spec, pl.BlockSpec((k, tr, n), lambda i: (0, i, 0)), spec, spec],
        out_specs=[spec] * 4, out_shape=[jax.ShapeDtypeStruct((r, n), F32)] * 4,
        compiler_params=_params(("parallel",)),
    )(w, gparts, m, v)


def kernel(x, c, w_ada, b_ada, ln_g, ln_b, a_w_in, a_w_dw, a_b_dw, a_norm_g, a_norm_b, a_w_out, b_w_in, b_q_norm_g, b_w_qb, b_w_out, kv_w_a, kv_norm_g, kv_w_b, loss_target, m_w_ada, m_b_ada, m_ln_g, m_ln_b, m_a_w_in, m_a_w_dw, m_a_b_dw, m_a_norm_g, m_a_norm_b, m_a_w_out, m_b_w_in, m_b_q_norm_g, m_b_w_qb, m_b_w_out, m_kv_w_a, m_kv_norm_g, m_kv_w_b, v_w_ada, v_b_ada, v_ln_g, v_ln_b, v_a_w_in, v_a_w_dw, v_a_b_dw, v_a_norm_g, v_a_norm_b, v_a_w_out, v_b_w_in, v_b_q_norm_g, v_b_w_qb, v_b_w_out, v_kv_w_a, v_kv_norm_g, v_kv_w_b):
    S, D = x.shape[1], x.shape[2]
    C = a_w_out.shape[1] * N_DEV
    QL = b_q_norm_g.shape[1]
    KL = kv_norm_g.shape[0]
    H = kv_w_b.shape[1] * N_DEV // (NOPE + VDIM)
    HL = H // N_DEV
    W = H * VDIM
    assert a_w_dw.shape[1] == CONV_KERNEL and kv_w_a.shape[1] == KL + ROPE and b_w_qb.shape[2] == HL * (NOPE + ROPE)
    assert b_w_in.shape[2] * N_DEV == QL + W and D % (N_DEV * LANES) == 0 and S % CONV_ROWS == 0
    TW = 1024 if W % 1024 == 0 and QL % 1024 == 0 else LANES

    xi, yi, ci = _place()
    idx = 4 * xi + 2 * yi + ci
    core = jnp.reshape(ci, (1,)).astype(jnp.int32)
    x2, tgt = x[0], loss_target[0]
    cos_t, sin_a, sin_b = _rope_tables(S)
    rope_rows = [(cos_t, LANES, 0, False), (sin_a, LANES, 0, False), (sin_b, LANES, 0, False)]

    win0 = _all_gather(a_w_in[0].astype(BF16), "ag_a_w_in")
    sh_wout0 = a_w_out[0].astype(BF16)
    sh_wbin = b_w_in[0].astype(BF16)
    wqb_loc = jnp.pad(b_w_qb[0].reshape(QL, HL, NOPE + ROPE), ((0, 0), (0, 0), (0, QPAD - NOPE - ROPE)))
    sh_wqb = wqb_loc.reshape(QL, HL * QPAD).astype(BF16)
    sh_wbout = b_w_out[0].astype(BF16)
    kva_w = KL + LANES
    sh_wkva = jnp.pad(kv_w_a, ((0, 0), (0, kva_w - KL - ROPE))).astype(BF16)
    sh_wkvb = kv_w_b.astype(BF16)

    cl = C // N_DEV
    v0 = CONV_TOP
    small_loc = _pack_rows([(a_w_dw[0], 0), (a_b_dw, v0), (a_norm_g, v0 + 1), (a_norm_b, v0 + 2)], 40, cl, "pack_conv_small")
    small = _all_gather(small_loc, "ag_conv_small")
    small = jnp.transpose(small, (1, 0, 2)).reshape(40, C)
    w_dw, b_dw, g_cn, b_cn = small[:CONV_KERNEL], small[v0:v0 + 1], small[v0 + 1:v0 + 2], small[v0 + 2:v0 + 3]

    c_all = _all_gather(jnp.broadcast_to(c, (SUBLANES, D)), "ag_c")[:, 0, :]
    c_pad = jnp.concatenate([c_all, jnp.zeros((16 - N_DEV, D), F32)], axis=0)
    (sc,), _ = _rowwise(lambda r, v: ([_silu(r[0])], []), [(c_pad, D, 0, False)], [], [(BF16, D, D)], [],
                        n_rows=16, tr=16, name="silu_c")
    nl = 3 * D // N_DEV
    mod_loc = jnp.concatenate([_mm(sc, w_ada[l], "nn", F32, f"mod{l}") for l in range(DEPTH)], axis=0)
    mod_all = jnp.transpose(_all_gather(mod_loc, "ag_mod"), (1, 0, 2)).reshape(DEPTH, 16, 3 * D)
    mods = []
    for l in range(DEPTH):
        row = lax.dynamic_slice(mod_all[l], (idx, 0), (1, 3 * D)) + b_ada[l][None, :]
        mods.append((row[:, :D], row[:, D:2 * D], row[:, 2 * D:]))
    (shift0, scale0, gate0), (shift1, scale1, gate1) = mods
    vec = lambda a, w=None: (a, a.shape[1] if w is None else w, 0, False)

    def f_mod0(r, v):
        xh, _ = _ln_stats(r[0])
        return [xh * (1.0 + v[0]) + v[1]], []
    (h0,), _ = _rowwise(f_mod0, [(x2, D, 0, False)], [vec(scale0), vec(shift0)], [(BF16, D, D)], [],
                        n_rows=S, tr=128, name="l0_modulate")
    u0, wout0, wbin = _mm(h0, win0, "nn", F32, "l0_in", b_slots=True,
                          jobs=[_Gather(sh_wout0), _Gather(sh_wbin)])
    wout0 = wout0.reshape(C, D)
    v2 = _conv_fwd(u0, w_dw, b_dw, S, C, "l0_conv")

    def f_gate0(r, v):
        xh, _ = _ln_stats(r[0])
        return [_silu(xh * v[0] + v[1]) * _silu(r[1])], []
    (p0,), _ = _rowwise(f_gate0, [(v2, C, 0, False), (u0, C, 2, False)], [vec(g_cn), vec(b_cn)], [(BF16, C, C)], [],
                        n_rows=S, tr=128, name="l0_gate")
    o0, wkva, wkvb = _mm(p0, wout0, "nn", F32, "l0_out", jobs=[_Gather(sh_wkva), _Gather(sh_wkvb)])
    wkva = wkva.reshape(D, kva_w)

    def f_res0(r, v):
        gate, g0, b0, sc1, sh1 = v
        xh, _ = _ln_stats(ALPHA * r[0] + (1.0 + gate) * r[1])
        x1 = xh * g0 + b0
        xh1, _ = _ln_stats(x1)
        return [x1, x1, xh1 * (1.0 + sc1) + sh1], []
    (x1, x1b, h1), _ = _rowwise(f_res0, [(x2, D, 0, False), (o0, D, 0, False)],
                                [vec(gate0), vec(ln_g[0:1]), vec(ln_b[0:1]), vec(scale1), vec(shift1)],
                                [(F32, D, D), (BF16, D, D), (BF16, D, D)], [], n_rows=S, tr=128, name="l0_residual")

    kva = _mm(x1b, wkva, "nn", F32, "kv_a")
    g_kv = kv_norm_g[None, :]

    def f_kvn(r, v):
        xh, _ = _rms_stats(r[0][:, :KL])
        return [xh * v[0], _rope_fwd(r[0][:, KL:], r[1], r[2], r[3])], []
    (ckv, krp), _ = _rowwise(f_kvn, [(kva, kva_w, 0, False)] + rope_rows, [vec(g_kv)],
                             [(BF16, KL, KL), (BF16, LANES, LANES)], [], n_rows=S, tr=256, name="kv_norm_rope")
    kvh = _mm(ckv, wkvb, "nn", BF16, "kv_b", b_slots=True)

    u1, wqb = _mm(h1, wbin, "nn", F32, "l1_in", b_slots=True, jobs=[_Gather(sh_wqb)])
    g_q = b_q_norm_g[0:1]

    def f_qn(r, v):
        xh, _ = _rms_stats(r[0])
        return [xh * v[0]], []
    (cqn,), _ = _rowwise(f_qn, [(u1, QL, 0, False)], [vec(g_q)], [(BF16, QL, QL)], [], n_rows=S, tr=256, name="q_norm")
    qf = _mm(cqn, wqb, "nn", F32, "q_b", b_slots=True)

    gh = ROPE_HEADS if H % ROPE_HEADS == 0 else 1

    def rope_heads(rope):
        def fn(r, v):
            out = []
            for g in range(gh):
                out.append(r[0][:, g * QPAD:g * QPAD + NOPE] * ATTN_SCALE)
                out.append(rope(r[0][:, g * QPAD + NOPE:(g + 1) * QPAD], r[1], r[2], r[3]) * ATTN_SCALE)
            return [jnp.concatenate(out, axis=1)], []
        return fn
    (qb,), _ = _rowwise(rope_heads(_rope_fwd), [(qf, gh * QPAD, 0, True)] + rope_rows, [], [(BF16, H * QPAD, gh * QPAD)], [],
                        n_rows=S, tr=256, name="q_rope", nc=H // gh)
    oa, lse, wbout = _attn_fwd(qb, kvh, krp, S, H, "attn_fwd", jobs=[_Gather(sh_wbout)])
    wbout = wbout.reshape(W, D)
    zc0 = QL // TW

    def f_gate1(r, v):
        return [r[0] * _silu(r[1])], []
    (p1,), _ = _rowwise(f_gate1, [(oa, TW, 0, True), (u1, TW, zc0, True)], [], [(BF16, W, TW)], [],
                        n_rows=S, tr=512 if S % 512 == 0 else S, name="l1_gate", nc=W // TW)
    o1 = _mm(p1, wbout, "nn", F32, "l1_out")

    def f_head(r, v):
        gate, g1, b1 = v
        xh, rstd = _ln_stats(ALPHA * r[0] + (1.0 + gate) * r[1])
        err = xh * g1 + b1 - r[2]
        dy = err * (1.0 / D)
        dr = _ln_bwd(dy * g1, xh, rstd)
        return [dr, (1.0 + gate) * dr], [_colsum(err * err), _colsum(dy * xh), _colsum(dy), _colsum(dr * r[1])]
    (dr1, do1b), (loss_row, dlng1, dlnb1, dgate1) = _rowwise(
        f_head, [(x1, D, 0, False), (o1, D, 0, False), (tgt, D, 0, False)], [vec(gate1), vec(ln_g[1:2]), vec(ln_b[1:2])],
        [(F32, D, D), (BF16, D, D)], [D, D, D, D], n_rows=S, tr=64, name="head_loss")
    loss = lax.psum(0.5 / D * jnp.sum(loss_row), ("x", "y", "c"))

    dp1 = _mm(do1b, wbout, "nt", F32, "l1_out_dx")
    g_wbout = _mm(p1, do1b, "tn", BF16, "l1_out_dw").reshape(N_DEV, W // N_DEV, D)

    def f_dgate1(r, v):
        dp, o, z = r
        return [dp * _silu(z), dp * o * _dsilu(z)], []
    (doa, dz1), _, (got,) = _rowwise(f_dgate1, [(dp1, TW, 0, True), (oa, TW, 0, True), (u1, TW, zc0, True)], [],
                                     [(BF16, W, TW), (BF16, W, TW)], [], n_rows=S, tr=512 if S % 512 == 0 else S,
                                     name="l1_gate_bwd", nc=W // TW, jobs=[_PairExchange(g_wbout)])
    t_wbout = _pair_add(g_wbout, got, core, "rs_b_w_out_add")
    dqf, dkvh, dkrp, r_wbout = _attn_bwd(qb, kvh, krp, doa, oa, lse, S, H, "attn_bwd", jobs=[_ChipExchange(t_wbout)])

    (dqb,), _ = _rowwise(rope_heads(_rope_bwd), [(dqf, gh * QPAD, 0, True)] + rope_rows, [], [(BF16, H * QPAD, gh * QPAD)], [],
                         n_rows=S, tr=256, name="q_rope_bwd", nc=H // gh)
    g_wqb = _mm(cqn, dqb, "tn", BF16, "q_b_dw", out_slots=True)
    dcqn, got = _mm(dqb, wqb, "nt", F32, "q_b_dx", b_slots=True, jobs=[_PairExchange(g_wqb)])
    t_wqb = _pair_add(g_wqb, got, core, "rs_b_w_qb_add")

    def f_qn_bwd(r, v):
        xh, rr = _rms_stats(r[0])
        return [_rms_bwd(r[1] * v[0], xh, rr)], [_colsum(r[1] * xh)]
    (dcq,), (dg_q,) = _rowwise(f_qn_bwd, [(u1, QL, 0, False), (dcqn, QL, 0, False)], [vec(g_q)], [(BF16, QL, QL)], [QL],
                               n_rows=S, tr=256, name="q_norm_bwd")
    du1 = jnp.concatenate([dcq, dz1], axis=1)
    g_wbin, r_wqb = _mm(h1, du1, "tn", BF16, "l1_in_dw", out_slots=True, jobs=[_ChipExchange(t_wqb)])
    dh1, got = _mm(du1, wbin, "nt", F32, "l1_in_dx", b_slots=True, jobs=[_PairExchange(g_wbin)])
    t_wbin = _pair_add(g_wbin, got, core, "rs_b_w_in_add")

    g_wkvb = _mm(ckv, dkvh, "tn", BF16, "kv_b_dw", out_slots=True)
    dckv, got = _mm(dkvh, wkvb, "nt", F32, "kv_b_dx", b_slots=True, jobs=[_PairExchange(g_wkvb)])
    t_wkvb = _pair_add(g_wkvb, got, core, "rs_kv_w_b_add")

    def f_kvn_bwd(r, v):
        xh, rr = _rms_stats(r[0][:, :KL])
        dck = _rms_bwd(r[1] * v[0], xh, rr)
        dkr = _rope_bwd(r[2], r[3], r[4], r[5])
        return [jnp.concatenate([dck, dkr], axis=1)], [_colsum(r[1] * xh)]
    (dkva,), (dg_kv,) = _rowwise(f_kvn_bwd, [(kva, kva_w, 0, False), (dckv, KL, 0, False), (dkrp, LANES, 0, False)] + rope_rows,
                                 [vec(g_kv)], [(BF16, kva_w, kva_w)], [KL], n_rows=S, tr=256, name="kv_norm_rope_bwd")
    g_wkva = _mm(x1b, dkva, "tn", BF16, "kv_a_dw").reshape(N_DEV, D // N_DEV, kva_w)
    dx1_kv, got = _mm(dkva, wkva, "nt", F32, "kv_a_dx", jobs=[_PairExchange(g_wkva)])
    t_wkva = _pair_add(g_wkva, got, core, "rs_kv_w_a_add")

    def f_res0_bwd(r, v):
        dr1_, dxkv, dh, x1_, x_, o_ = r
        sc1, gate, g0 = v
        xh1, rstd1 = _ln_stats(x1_)
        dx1 = ALPHA * dr1_ + dxkv + _ln_bwd(dh * (1.0 + sc1), xh1, rstd1)
        xh0, rstd0 = _ln_stats(ALPHA * x_ + (1.0 + gate) * o_)
        dr0 = _ln_bwd(dx1 * g0, xh0, rstd0)
        return ([dr0, (1.0 + gate) * dr0],
                [_colsum(dh * xh1), _colsum(dh), _colsum(dx1 * xh0), _colsum(dx1), _colsum(dr0 * o_)])
    (dr0, do0b), (dscale1, dshift1, dlng0, dlnb0, dgate0) = _rowwise(
        f_res0_bwd, [(a, D, 0, False) for a in (dr1, dx1_kv, dh1, x1, x2, o0)], [vec(scale1), vec(gate0), vec(ln_g[0:1])],
        [(F32, D, D), (BF16, D, D)], [D] * 5, n_rows=S, tr=64, name="l0_residual_bwd")

    dp0, r_wkvb, r_wkva = _mm(do0b, wout0, "nt", F32, "l0_out_dx",
                              jobs=[_ChipExchange(t_wkvb), _ChipExchange(t_wkva)])
    g_wout0 = _mm(p0, do0b, "tn", BF16, "l0_out_dw").reshape(N_DEV, C // N_DEV, D)

    def f_gate0_bwd(r, v):
        dp, v2_, z = r
        g, b = v
        xh, rstd = _ln_stats(v2_)
        v3 = xh * g + b
        dv4 = dp * _silu(z)
        dz = dp * _silu(v3) * _dsilu(z)
        dv3 = dv4 * _dsilu(v3)
        dv2_ = _ln_bwd(dv3 * g, xh, rstd)
        return [dz, dv2_], [_colsum(dv3 * xh), _colsum(dv3), _colsum(dv2_)]
    (dz0, dv2), (dg_cn, db_cn, db_dw) = _rowwise(
        f_gate0_bwd, [(dp0, C, 0, False), (v2, C, 0, False), (u0, C, 2, False)], [vec(g_cn), vec(b_cn)],
        [(BF16, C, C), (F32, C, C)], [C, C, C], n_rows=S, tr=64, name="l0_gate_bwd")
    da0, dg0, dw_dw, r_wbin, got = _conv_bwd(u0, dv2, w_dw, S, C, "l0_conv_bwd",
                                             jobs=[_ChipExchange(t_wbin), _PairExchange(g_wout0)])
    t_wout0 = _pair_add(g_wout0, got, core, "rs_a_w_out_add")
    du0 = jnp.concatenate([da0, dg0, dz0], axis=1)
    g_win0_a, r_wout0 = _mm(h0, du0, "tn", BF16, "l0_in_dw_a", out_slots=True, m_rows=(0, D // 2),
                            jobs=[_ChipExchange(t_wout0)])
    g_win0_b, got = _mm(h0, du0, "tn", BF16, "l0_in_dw_b", out_slots=True, m_rows=(D // 2, D // 2),
                        jobs=[_PairExchange(g_win0_a)])
    t_win0_a = _pair_add(g_win0_a, got, core, "rs_a_w_in_add_a")
    dh0, r_win0_a, got = _mm(du0, win0, "nt", F32, "l0_in_dx", b_slots=True,
                             jobs=[_ChipExchange(t_win0_a), _PairExchange(g_win0_b)])
    t_win0_b = _pair_add(g_win0_b, got, core, "rs_a_w_in_add_b")

    def f_mod0_bwd(r, v):
        xh, rstd = _ln_stats(r[2])
        return [ALPHA * r[0] + _ln_bwd(r[1] * (1.0 + v[0]), xh, rstd)], [_colsum(r[1] * xh), _colsum(r[1])]
    (grad_x,), (dscale0, dshift0), (r_win0_b,) = _rowwise(
        f_mod0_bwd, [(dr0, D, 0, False), (dh0, D, 0, False), (x2, D, 0, False)], [vec(scale0)], [(F32, D, D)], [D, D],
        n_rows=S, tr=128, name="l0_modulate_bwd", jobs=[_ChipExchange(t_win0_b)])
    r_win0 = jnp.concatenate([r_win0_a, r_win0_b], axis=1)

    assert C == D and QL <= D and KL <= D
    n_small = 48
    singles = [dlng0, dlng1, dlnb0, dlnb1, dshift0, dscale0, dgate0, dshift1, dscale1, dgate1, db_dw, dg_cn, db_cn, dg_q, dg_kv]
    rows_loc = _pack_rows([(dw_dw, 0)] + [(a, v0 + i) for i, a in enumerate(singles)], n_small, D, "pack_small_grads")
    parts = _all_gather(rows_loc, "ag_small_grads")
    tot = _slot_sum(parts, "small_grads_sum")
    g_ln_g, g_ln_b = tot[v0:v0 + 2], tot[v0 + 2:v0 + 4]
    g_b_ada = tot[v0 + 4:v0 + 10].reshape(DEPTH, 3 * D)
    loc_cols = lambda a: lax.dynamic_slice(a, (0, idx * cl), (a.shape[0], cl))
    g_a_w_dw = loc_cols(tot[0:CONV_KERNEL])
    g_a_b_dw = loc_cols(tot[v0 + 10:v0 + 11])
    g_a_norm_g = loc_cols(tot[v0 + 11:v0 + 12])
    g_a_norm_b = loc_cols(tot[v0 + 12:v0 + 13])
    g_q_norm = tot[v0 + 13:v0 + 14, :QL]
    g_kv_norm = tot[v0 + 14:v0 + 15, :KL]

    g_w_ada = []
    for l in range(DEPTH):
        dmod = parts[:, v0 + 4 + 3 * l:v0 + 7 + 3 * l, :].reshape(N_DEV, 3 * D)
        dmod = lax.dynamic_slice(dmod, (0, idx * nl), (N_DEV, nl))
        dmod = jnp.concatenate([dmod, jnp.zeros((16 - N_DEV, nl), F32)], axis=0)
        g_w_ada.append(_mm(sc, dmod, "tn", F32, f"mod{l}_dw"))
    g_w_ada = jnp.stack(g_w_ada)

    r_wqb = r_wqb.reshape(N_CHIP, QL, HL, QPAD)[..., :NOPE + ROPE].reshape(N_CHIP, QL, HL * (NOPE + ROPE))
    r_wkva = r_wkva[..., :KL + ROPE]

    def upd(name, w, gparts, m, v):
        shp = w.shape
        r2 = (1, shp[0]) if len(shp) == 1 else (math.prod(shp[:-1]), shp[-1])
        gp = gparts.reshape((gparts.shape[0],) + r2)
        res = _adamw(w.reshape(r2), gp, m.reshape(r2), v.reshape(r2), "adamw_" + name)
        return [a.reshape(shp) for a in res]

    one = lambda g: g[None]
    results = [
        upd("w_ada", w_ada, one(g_w_ada), m_w_ada, v_w_ada),
        upd("b_ada", b_ada, one(g_b_ada), m_b_ada, v_b_ada),
        upd("ln_g", ln_g, one(g_ln_g), m_ln_g, v_ln_g),
        upd("ln_b", ln_b, one(g_ln_b), m_ln_b, v_ln_b),
        upd("a_w_in", a_w_in, r_win0[:, None], m_a_w_in, v_a_w_in),
        upd("a_w_dw", a_w_dw, one(g_a_w_dw[None]), m_a_w_dw, v_a_w_dw),
        upd("a_b_dw", a_b_dw, one(g_a_b_dw), m_a_b_dw, v_a_b_dw),
        upd("a_norm_g", a_norm_g, one(g_a_norm_g), m_a_norm_g, v_a_norm_g),
        upd("a_norm_b", a_norm_b, one(g_a_norm_b), m_a_norm_b, v_a_norm_b),
        upd("a_w_out", a_w_out, r_wout0[:, None], m_a_w_out, v_a_w_out),
        upd("b_w_in", b_w_in, r_wbin[:, None], m_b_w_in, v_b_w_in),
        upd("b_q_norm_g", b_q_norm_g, one(g_q_norm), m_b_q_norm_g, v_b_q_norm_g),
        upd("b_w_qb", b_w_qb, r_wqb[:, None], m_b_w_qb, v_b_w_qb),
        upd("b_w_out", b_w_out, r_wbout[:, None], m_b_w_out, v_b_w_out),
        upd("kv_w_a", kv_w_a, r_wkva, m_kv_w_a, v_kv_w_a),
        upd("kv_norm_g", kv_norm_g, one(g_kv_norm[0]), m_kv_norm_g, v_kv_norm_g),
        upd("kv_w_b", kv_w_b, r_wkvb, m_kv_w_b, v_kv_w_b),
    ]
    grads, deltas, new_m, new_v = zip(*results)
    return (loss, grad_x[None], *grads, *deltas, *new_m, *new_v)
```

```python
import functools
import math

import jax
import jax.numpy as jnp
from jax import lax
from jax.experimental import pallas as pl
from jax.experimental.pallas import tpu as pltpu

F32 = jnp.float32
BF16 = jnp.bfloat16
MESH = pl.DeviceIdType.MESH
ANY = pl.BlockSpec(memory_space=pl.ANY)

N_DEV = 8
N_CHIP = 4
V7X_VMEM_BYTES = 64 * 1024 * 1024
VMEM_LIMIT = V7X_VMEM_BYTES - 8 * 1024 * 1024
LANES = 128
SUBLANES = 8

DEPTH = 2
CONV_KERNEL = 31
NOPE = 128
ROPE = 64
VDIM = 128
QPAD = NOPE + LANES
ROPE_BASE = 10000.0
LN_EPS = 1e-5
RMS_EPS = 1e-6
ALPHA = (2.0 * DEPTH) ** 0.25
ATTN_SCALE = (NOPE + ROPE) ** -0.5
ADAM_LR = 0.001
ADAM_B1 = 0.9
ADAM_B2 = 0.999
ADAM_EPS = 1e-08
ADAM_WD = 0.01
ADAM_STEP = 10

CONV_TOP = 32
CONV_ROWS = 64
ATTN_BLOCK = 512
ATTN_CHUNK = 32
MM_INPUT_BYTES = 32 * 1024 * 1024
ADAMW_TILE_ELEMS = 256 * 1024


def _pick(dim, prefs):
    for p in prefs:
        if dim % p == 0:
            return p
    return dim


def _params(sem, **kw):
    return pltpu.CompilerParams(dimension_semantics=sem, vmem_limit_bytes=VMEM_LIMIT, **kw)


_DIMS = {"nn": (((1,), (0,)), ((), ())), "nt": (((1,), (1,)), ((), ())), "tn": (((0,), (0,)), ((), ()))}


def _mm(a, b, mode, out_dtype, name, *, b_slots=False, out_slots=False, m_rows=None, jobs=()):
    m_first = 0
    if mode == "tn":
        K, M = a.shape
        if m_rows is not None:
            m_first, M = m_rows
    else:
        M, K = a.shape
    if b_slots:
        assert mode in ("nn", "nt")
        if mode == "nn":
            _, kb, n_loc = b.shape
            N, slot_w = N_DEV * n_loc, n_loc
        else:
            _, N, k_loc = b.shape
            kb, slot_w = N_DEV * k_loc, k_loc
    else:
        if mode == "nt":
            N, kb = b.shape
        else:
            kb, N = b.shape
        slot_w = N // N_DEV if out_slots else None
    assert kb == K, (name, a.shape, b.shape)
    tm = _pick(M, (1024, 512, 256, 128, 64, 32, 16))
    n_unit = slot_w if (mode == "nn" and b_slots) or out_slots else N
    k_unit = slot_w if (mode == "nt" and b_slots) else K
    tn = _pick(n_unit, (1024, 1152, 768, 640, 512, 384, 256, 128))
    step_bytes = 2 * (tm * a.dtype.itemsize + tn * b.dtype.itemsize)
    tk = _pick(k_unit, [t for t in (4096, 2048, 1536, 1152, 1024, 768, 640, 512, 384, 256, 128, 64, 32, 16)
                        if t * step_bytes <= MM_INPUT_BYTES])
    nk = K // tk

    if mode == "tn":
        assert m_first % tm == 0
        a_spec = pl.BlockSpec((tk, tm), lambda i, j, k: (k, i + m_first // tm))
    else:
        a_spec = pl.BlockSpec((tm, tk), lambda i, j, k: (i, k))
    if b_slots and mode == "nn":
        per = slot_w // tn
        b_spec = pl.BlockSpec((None, tk, tn), lambda i, j, k: (j // per, k, j % per))
    elif b_slots and mode == "nt":
        per = slot_w // tk
        b_spec = pl.BlockSpec((None, tn, tk), lambda i, j, k: (k // per, j, k % per))
    elif mode == "nt":
        b_spec = pl.BlockSpec((tn, tk), lambda i, j, k: (j, k))
    else:
        b_spec = pl.BlockSpec((tk, tn), lambda i, j, k: (k, j))
    if out_slots:
        per_o = slot_w // tn
        o_spec = pl.BlockSpec((None, tm, tn), lambda i, j, k: (j // per_o, i, j % per_o))
        o_shape = jax.ShapeDtypeStruct((N_DEV, M, slot_w), out_dtype)
    else:
        o_spec = pl.BlockSpec((tm, tn), lambda i, j, k: (i, j))
        o_shape = jax.ShapeDtypeStruct((M, N), out_dtype)
    dims = _DIMS[mode]

    def step(a_ref, b_ref):
        return lax.dot_general(a_ref[...].astype(BF16), b_ref[...].astype(BF16), dims, preferred_element_type=F32)

    if nk == 1:
        def body(a_ref, b_ref, o_ref):
            o_ref[...] = step(a_ref, b_ref).astype(out_dtype)
    else:
        def body(a_ref, b_ref, o_ref, acc_ref):
            k = pl.program_id(2)

            @pl.when(k == 0)
            def _():
                acc_ref[...] = jnp.zeros_like(acc_ref)

            acc_ref[...] += step(a_ref, b_ref)

            @pl.when(k == nk - 1)
            def _():
                o_ref[...] = acc_ref[...].astype(out_dtype)

    res = _grid_call(body, (a, b), name=name, grid=(M // tm, N // tn, nk), in_specs=[a_spec, b_spec],
                     out_specs=[o_spec], out_shape=[o_shape],
                     scratch_shapes=[] if nk == 1 else [pltpu.VMEM((tm, tn), F32)],
                     sem=("parallel", "parallel", "arbitrary"), jobs=jobs)
    return tuple(res) if jobs else res[0]


def _rowwise(fn, rows, vecs, outs, accs, *, n_rows, tr, name, nc=1, jobs=()):
    nr = n_rows // tr
    n_in, n_vec, n_out, n_acc = len(rows), len(vecs), len(outs), len(accs)

    def cmap(cb0, follow):
        if follow:
            return lambda j, i: (i, cb0 + j)
        return lambda j, i: (i, cb0)

    def vmap_(cb0, follow):
        if follow:
            return lambda j, i: (0, cb0 + j)
        return lambda j, i: (0, cb0)

    in_specs = [pl.BlockSpec((tr, w), cmap(cb0, fol)) for (_, w, cb0, fol) in rows]
    in_specs += [pl.BlockSpec((1, w), vmap_(cb0, fol)) for (_, w, cb0, fol) in vecs]
    out_specs = [pl.BlockSpec((tr, bw), lambda j, i: (i, j)) for (_, _, bw) in outs]
    out_specs += [pl.BlockSpec((1, w), lambda j, i: (0, j)) for w in accs]
    out_shape = [jax.ShapeDtypeStruct((n_rows, tw), dt) for (dt, tw, _) in outs]
    out_shape += [jax.ShapeDtypeStruct((1, w * nc), F32) for w in accs]

    def body(*refs):
        ins = [r[...] for r in refs[:n_in]]
        vs = [r[...] for r in refs[n_in:n_in + n_vec]]
        o_refs = refs[n_in + n_vec:n_in + n_vec + n_out]
        a_refs = refs[n_in + n_vec + n_out:]
        o_vals, a_vals = fn(ins, vs)
        for r, v in zip(o_refs, o_vals):
            r[...] = v.astype(r.dtype)
        if n_acc:
            i = pl.program_id(1)

            @pl.when(i == 0)
            def _():
                for r in a_refs:
                    r[...] = jnp.zeros_like(r)

            for r, v in zip(a_refs, a_vals):
                r[...] += v

    res = _grid_call(
        body, [r[0] for r in rows] + [v[0] for v in vecs], name=name, grid=(nc, nr), in_specs=in_specs,
        out_specs=out_specs, out_shape=out_shape, scratch_shapes=[], sem=("parallel", "arbitrary"), jobs=jobs)
    outs, accs_out, extra = res[:n_out], res[n_out:n_out + n_acc], res[n_out + n_acc:]
    return (outs, accs_out, extra) if jobs else (outs, accs_out)


def _colsum(x):
    return jnp.sum(x, axis=0, keepdims=True)


def _ln_stats(x):
    mu = jnp.mean(x, axis=-1, keepdims=True)
    xc = x - mu
    var = jnp.mean(xc * xc, axis=-1, keepdims=True)
    rstd = lax.rsqrt(var + LN_EPS)
    return xc * rstd, rstd


def _ln_bwd(dxhat, xhat, rstd):
    return rstd * (dxhat - jnp.mean(dxhat, axis=-1, keepdims=True)
                   - xhat * jnp.mean(dxhat * xhat, axis=-1, keepdims=True))


def _rms_stats(x):
    r = lax.rsqrt(jnp.mean(x * x, axis=-1, keepdims=True) + RMS_EPS)
    return x * r, r


def _rms_bwd(dxhat, xhat, r):
    return r * (dxhat - xhat * jnp.mean(dxhat * xhat, axis=-1, keepdims=True))


def _silu(x):
    return x * jax.nn.sigmoid(x)


def _dsilu(x):
    s = jax.nn.sigmoid(x)
    return s * (1.0 + x * (1.0 - s))


def _rope_fwd(x, cos_t, sin_a, sin_b):
    return x * cos_t + pltpu.roll(x, LANES - ROPE // 2, 1) * sin_a + pltpu.roll(x, ROPE // 2, 1) * sin_b


def _rope_bwd(dy, cos_t, sin_a, sin_b):
    return dy * cos_t + pltpu.roll(dy * sin_a, ROPE // 2, 1) + pltpu.roll(dy * sin_b, LANES - ROPE // 2, 1)


def _rope1(x, tab):
    lane = lax.broadcasted_iota(jnp.int32, x.shape, 1)
    x = jnp.where(lane < ROPE, x, 0.0)
    swapped = pltpu.roll(x, LANES - ROPE // 2, 1) + pltpu.roll(x, ROPE // 2, 1)
    return jnp.where(lane < ROPE, x * tab + swapped * pltpu.roll(tab, ROPE, 1), 0.0)


def _rope1_bwd(dy, tab):
    lane = lax.broadcasted_iota(jnp.int32, dy.shape, 1)
    dy = jnp.where(lane < ROPE, dy, 0.0)
    g = dy * pltpu.roll(tab, ROPE, 1)
    return jnp.where(lane < ROPE, dy * tab + pltpu.roll(g, LANES - ROPE // 2, 1) + pltpu.roll(g, ROPE // 2, 1), 0.0)


def _rope_tables(seq):
    half = ROPE // 2
    inv_freq = ROPE_BASE ** (-jnp.arange(half, dtype=F32) / half)
    ang = jnp.arange(seq, dtype=jnp.int32).astype(F32)[:, None] * inv_freq[None, :]
    cos, sin = jnp.cos(ang), jnp.sin(ang)
    z = jnp.zeros_like(cos)
    cos_t = jnp.concatenate([cos, cos, z, z], axis=1)
    sin_a = jnp.concatenate([-sin, z, z, z], axis=1)
    sin_b = jnp.concatenate([z, sin, z, z], axis=1)
    return cos_t, sin_a, sin_b


def _shifted(win, n):
    return [win] + [pltpu.roll(win, n - j, 0) for j in range(1, SUBLANES)]


def _conv_fwd(u0, w_dw, b_dw, seq, width, name):
    tc = LANES
    nct = width // tc
    rows = CONV_ROWS
    n_win = rows + CONV_TOP

    def body(a_ref, g_ref, w_ref, b_ref, o_ref, vpad):
        vpad[0:CONV_TOP, :] = jnp.zeros((CONV_TOP, tc), F32)
        vpad[CONV_TOP:, :] = a_ref[...] * jax.nn.sigmoid(g_ref[...])

        def chunk(i, carry):
            t0 = pl.multiple_of(i * rows, rows)
            rot = _shifted(vpad[pl.ds(t0, n_win), :], n_win)
            acc = jnp.zeros((rows, tc), F32) + b_ref[...]
            for k in range(CONV_KERNEL):
                s = k + CONV_TOP - (CONV_KERNEL - 1)
                q, j = divmod(s, SUBLANES)
                acc = acc + w_ref[k:k + 1, :] * rot[j][SUBLANES * q:SUBLANES * q + rows]
            o_ref[pl.ds(t0, rows), :] = acc
            return carry

        lax.fori_loop(0, seq // rows, chunk, 0)

    return pl.pallas_call(
        body, name=name, grid=(nct,),
        in_specs=[pl.BlockSpec((seq, tc), lambda j: (0, j)), pl.BlockSpec((seq, tc), lambda j: (0, nct + j)),
                  pl.BlockSpec((CONV_KERNEL, tc), lambda j: (0, j)), pl.BlockSpec((1, tc), lambda j: (0, j))],
        out_specs=pl.BlockSpec((seq, tc), lambda j: (0, j)),
        out_shape=jax.ShapeDtypeStruct((seq, width), F32),
        scratch_shapes=[pltpu.VMEM((seq + CONV_TOP, tc), F32)],
        compiler_params=_params(("parallel",)),
    )(u0, u0, w_dw, b_dw)


def _conv_bwd(u0, dv2, w_dw, seq, width, name, jobs=()):
    tc = LANES
    nct = width // tc
    rows = CONV_ROWS
    n_win = rows + CONV_TOP
    nsh = CONV_KERNEL * SUBLANES

    def body(a_ref, g_ref, d_ref, w_ref, da_ref, dg_ref, dw_ref, vpad, dpad, dwacc):
        vpad[0:CONV_TOP, :] = jnp.zeros((CONV_TOP, tc), F32)
        vpad[CONV_TOP:, :] = a_ref[...] * jax.nn.sigmoid(g_ref[...])
        dpad[0:seq, :] = d_ref[...]
        dpad[seq:, :] = jnp.zeros((CONV_TOP, tc), F32)
        dwacc[...] = jnp.zeros_like(dwacc)

        def chunk(i, carry):
            t0 = pl.multiple_of(i * rows, rows)
            vrot = _shifted(vpad[pl.ds(t0, n_win), :], n_win)
            drot = _shifted(dpad[pl.ds(t0, n_win), :], n_win)
            dcur = drot[0][0:rows]
            dv1 = jnp.zeros((rows, tc), F32)
            for k in range(CONV_KERNEL):
                q, j = divmod(CONV_KERNEL - 1 - k, SUBLANES)
                dv1 = dv1 + w_ref[k:k + 1, :] * drot[j][SUBLANES * q:SUBLANES * q + rows]
                q, j = divmod(k + CONV_TOP - (CONV_KERNEL - 1), SUBLANES)
                prod = dcur * vrot[j][SUBLANES * q:SUBLANES * q + rows]
                part = prod[0:SUBLANES]
                for r in range(1, rows // SUBLANES):
                    part = part + prod[SUBLANES * r:SUBLANES * (r + 1)]
                dwacc[SUBLANES * k:SUBLANES * (k + 1), :] += part
            a = a_ref[pl.ds(t0, rows), :]
            sg = jax.nn.sigmoid(g_ref[pl.ds(t0, rows), :])
            da_ref[pl.ds(t0, rows), :] = (dv1 * sg).astype(BF16)
            dg_ref[pl.ds(t0, rows), :] = (dv1 * a * sg * (1.0 - sg)).astype(BF16)
            return carry

        lax.fori_loop(0, seq // rows, chunk, 0)
        for k in range(CONV_KERNEL):
            dw_ref[k:k + 1, :] = jnp.sum(dwacc[SUBLANES * k:SUBLANES * (k + 1), :], axis=0, keepdims=True)

    return _grid_call(
        body, (u0, u0, dv2, w_dw), name=name, grid=(nct,),
        in_specs=[pl.BlockSpec((seq, tc), lambda j: (0, j)), pl.BlockSpec((seq, tc), lambda j: (0, nct + j)),
                  pl.BlockSpec((seq, tc), lambda j: (0, j)), pl.BlockSpec((CONV_KERNEL, tc), lambda j: (0, j))],
        out_specs=[pl.BlockSpec((seq, tc), lambda j: (0, j)), pl.BlockSpec((seq, tc), lambda j: (0, j)),
                   pl.BlockSpec((CONV_KERNEL, tc), lambda j: (0, j))],
        out_shape=[jax.ShapeDtypeStruct((seq, width), BF16), jax.ShapeDtypeStruct((seq, width), BF16),
                   jax.ShapeDtypeStruct((CONV_KERNEL, width), F32)],
        scratch_shapes=[pltpu.VMEM((seq + CONV_TOP, tc), F32), pltpu.VMEM((seq + CONV_TOP, tc), F32),
                        pltpu.VMEM((nsh, tc), F32)],
        sem=("parallel",), jobs=jobs)


def _lanes(x, n):
    return jnp.tile(x, (1, n // LANES))


def _diag_mask(r0, rows, t, transposed=False):
    r = lax.broadcasted_iota(jnp.int32, (rows, t), 0) + r0
    c = lax.broadcasted_iota(jnp.int32, (rows, t), 1)
    return (c >= r) if transposed else (r >= c)


def _attn_fwd(qf, tab, kv, kr, u1, z_block0, seq, heads, name, jobs=()):
    t = min(ATTN_BLOCK, seq)
    nb = seq // t
    ch = ATTN_CHUNK

    def body(qf_ref, tab_ref, kv_ref, kr_ref, z_ref, o_ref, lse_ref, qb_ref, g_ref, kf_s, q_s, s_s, p_s, m_s, l_s, c_s, acc_s):
        qi = pl.program_id(1)

        @pl.when(qi == 0)
        def _():
            kf_s[:, 0:NOPE] = kv_ref[:, 0:NOPE]
            kf_s[:, NOPE:] = kr_ref[...]

        q_s[:, 0:NOPE] = (qf_ref[:, 0:NOPE] * ATTN_SCALE).astype(BF16)
        q_s[:, NOPE:] = (_rope1(qf_ref[:, NOPE:], tab_ref[...]) * ATTN_SCALE).astype(BF16)
        qb_ref[...] = q_s[...]
        m_s[...] = jnp.full_like(m_s, -jnp.inf)
        l_s[...] = jnp.zeros_like(l_s)
        acc_s[...] = jnp.zeros_like(acc_s)

        def block(ki, masked):
            k0 = pl.multiple_of(ki * t, t)
            s_s[...] = lax.dot_general(q_s[...], kf_s[pl.ds(k0, t), :], _DIMS["nt"], preferred_element_type=F32)

            def chunk(c, carry):
                r0 = pl.multiple_of(c * ch, ch)
                rows = pl.ds(r0, ch)
                s_c = s_s[rows, :]
                if masked:
                    s_c = jnp.where(_diag_mask(r0, ch, t), s_c, -jnp.inf)
                m_prev = m_s[rows, :]
                m_new = jnp.maximum(m_prev, jnp.max(s_c, axis=1, keepdims=True))
                p = jnp.exp(s_c - _lanes(m_new, t))
                corr = jnp.exp(m_prev - m_new)
                l_s[rows, :] = corr * l_s[rows, :] + jnp.sum(p, axis=1, keepdims=True)
                m_s[rows, :] = m_new
                c_s[rows, :] = corr
                p_s[rows, :] = p.astype(BF16)
                return carry

            lax.fori_loop(0, t // ch, chunk, 0, unroll=True)
            acc_s[...] = c_s[...] * acc_s[...] + jnp.dot(p_s[...], kv_ref[pl.ds(k0, t), pl.ds(NOPE, VDIM)],
                                                         preferred_element_type=F32)

        def below(ki, carry):
            block(ki, False)
            return carry

        lax.fori_loop(0, qi, below, 0)
        block(qi, True)
        o = acc_s[...] / l_s[...]
        o_ref[...] = o
        g_ref[...] = (o * _silu(z_ref[...])).astype(BF16)
        lse_ref[...] = m_s[...] + jnp.log(l_s[...])

    return _grid_call(
        body, (qf, tab, kv, kr, u1), name=name, grid=(heads, nb),
        in_specs=[pl.BlockSpec((t, QPAD), lambda h, i: (i, h)),
                  pl.BlockSpec((t, LANES), lambda h, i: (i, 0)),
                  pl.BlockSpec((seq, QPAD), lambda h, i: (0, h)),
                  pl.BlockSpec((seq, LANES), lambda h, i: (0, 0)),
                  pl.BlockSpec((t, VDIM), lambda h, i: (i, z_block0 + h))],
        out_specs=[pl.BlockSpec((t, VDIM), lambda h, i: (i, h)),
                   pl.BlockSpec((t, LANES), lambda h, i: (i, h)),
                   pl.BlockSpec((t, QPAD), lambda h, i: (i, h)),
                   pl.BlockSpec((t, VDIM), lambda h, i: (i, h))],
        out_shape=[jax.ShapeDtypeStruct((seq, heads * VDIM), F32), jax.ShapeDtypeStruct((seq, heads * LANES), F32),
                   jax.ShapeDtypeStruct((seq, heads * QPAD), BF16), jax.ShapeDtypeStruct((seq, heads * VDIM), BF16)],
        scratch_shapes=[pltpu.VMEM((seq, QPAD), BF16), pltpu.VMEM((t, QPAD), BF16), pltpu.VMEM((t, t), F32),
                        pltpu.VMEM((t, t), BF16), pltpu.VMEM((t, LANES), F32), pltpu.VMEM((t, LANES), F32),
                        pltpu.VMEM((t, LANES), F32), pltpu.VMEM((t, VDIM), F32)],
        sem=("parallel", "arbitrary"), jobs=jobs)


def _attn_bwd(q, tab, kv, kr, do, o, lse, seq, heads, name, jobs=()):
    t = min(ATTN_BLOCK, seq)
    nb = seq // t
    ch = ATTN_CHUNK

    def body(q_ref, tab_ref, kv_ref, kr_ref, do_ref, o_ref, lse_ref, dqb_ref, dkv_ref, dkr_ref,
             dq_ref, kf_s, qt_s, dot_s, dl_s, s_s, dp_s, p_s, ds_s, dkt_s, dvt_s):
        @pl.when(pl.program_id(0) == 0)
        def _():
            dkr_ref[...] = jnp.zeros_like(dkr_ref)

        kf_s[:, 0:NOPE] = kv_ref[:, 0:NOPE]
        kf_s[:, NOPE:] = kr_ref[...]
        dq_ref[...] = jnp.zeros_like(dq_ref)
        for b in range(nb):
            rows = pl.ds(b * t, t)
            dob = do_ref[rows, :].astype(F32)
            qt_s[b] = jnp.transpose(q_ref[rows, :].astype(F32)).astype(BF16)
            dot_s[b] = jnp.transpose(dob).astype(BF16)
            dl_s[rows, :] = jnp.broadcast_to(jnp.sum(dob * o_ref[rows, :], axis=1, keepdims=True), (t, LANES))

        def key_block(ki, carry):
            k0 = pl.multiple_of(ki * t, t)
            kfb = kf_s[pl.ds(k0, t), :]
            dkt_s[...] = jnp.zeros_like(dkt_s)
            dvt_s[...] = jnp.zeros_like(dvt_s)

            def block(qi, masked):
                q0 = pl.multiple_of(qi * t, t)
                s_s[...] = lax.dot_general(q_ref[pl.ds(q0, t), :], kfb, _DIMS["nt"], preferred_element_type=F32)
                dp_s[...] = lax.dot_general(do_ref[pl.ds(q0, t), :], kv_ref[pl.ds(k0, t), pl.ds(NOPE, VDIM)],
                                            _DIMS["nt"], preferred_element_type=F32)

                def chunk(c, carry):
                    r0 = pl.multiple_of(c * ch, ch)
                    rows = pl.ds(r0, ch)
                    grows = pl.ds(pl.multiple_of(q0 + r0, ch), ch)
                    p = jnp.exp(s_s[rows, :] - _lanes(lse_ref[grows, :], t))
                    if masked:
                        p = jnp.where(_diag_mask(r0, ch, t), p, 0.0)
                    p_s[rows, :] = p.astype(BF16)
                    ds_s[rows, :] = (p * (dp_s[rows, :] - _lanes(dl_s[grows, :], t))).astype(BF16)
                    return carry

                lax.fori_loop(0, t // ch, chunk, 0, unroll=True)
                dq_ref[pl.ds(q0, t), :] += jnp.dot(ds_s[...], kfb, preferred_element_type=F32)
                dvt_s[...] += jnp.dot(dot_s[qi], p_s[...], preferred_element_type=F32)
                dkt_s[...] += jnp.dot(qt_s[qi], ds_s[...], preferred_element_type=F32)

            def above(qi, carry):
                block(qi, False)
                return carry

            block(ki, True)
            lax.fori_loop(ki + 1, nb, above, 0)
            dk = jnp.transpose(dkt_s[...])
            dkv_ref[pl.ds(k0, t), :] = jnp.concatenate([dk[:, :NOPE], jnp.transpose(dvt_s[...])], axis=1).astype(BF16)
            dkr_ref[pl.ds(k0, t), :] += dk[:, NOPE:]
            return carry

        lax.fori_loop(0, nb, key_block, 0)
        for b in range(nb):
            rows = pl.ds(b * t, t)
            dqb_ref[rows, 0:NOPE] = (dq_ref[rows, 0:NOPE] * ATTN_SCALE).astype(BF16)
            dqb_ref[rows, NOPE:] = (_rope1_bwd(dq_ref[rows, NOPE:], tab_ref[rows, :]) * ATTN_SCALE).astype(BF16)

    head = lambda w: pl.BlockSpec((seq, w), lambda h: (0, h))
    shared = pl.BlockSpec((seq, LANES), lambda h: (0, 0))
    return _grid_call(
        body, (q, tab, kv, kr, do, o, lse), name=name, grid=(heads,),
        in_specs=[head(QPAD), shared, head(QPAD), shared, head(VDIM), head(VDIM), head(LANES)],
        out_specs=[head(QPAD), head(QPAD), shared],
        out_shape=[jax.ShapeDtypeStruct((seq, heads * QPAD), BF16), jax.ShapeDtypeStruct((seq, heads * QPAD), BF16),
                   jax.ShapeDtypeStruct((seq, LANES), F32)],
        scratch_shapes=[pltpu.VMEM((seq, QPAD), F32),
                        pltpu.VMEM((seq, QPAD), BF16), pltpu.VMEM((nb, QPAD, t), BF16), pltpu.VMEM((nb, VDIM, t), BF16),
                        pltpu.VMEM((seq, LANES), F32), pltpu.VMEM((t, t), F32), pltpu.VMEM((t, t), F32),
                        pltpu.VMEM((t, t), BF16), pltpu.VMEM((t, t), BF16), pltpu.VMEM((QPAD, t), F32),
                        pltpu.VMEM((VDIM, t), F32)],
        sem=("arbitrary",), jobs=jobs)


def _place():
    x, y, c = lax.axis_index("x"), lax.axis_index("y"), lax.axis_index("c")
    return x, y, c


class _Gather:
    def __init__(self, shard):
        r, n = shard.shape
        self.ins = [shard]
        self.outs = [jax.ShapeDtypeStruct((N_DEV, r, n), shard.dtype)]
        self.sems = [pltpu.SemaphoreType.DMA((7,)), pltpu.SemaphoreType.DMA((7,)), pltpu.SemaphoreType.DMA]

    def bind(self, ins, outs, sems):
        (self.x_ref,), (self.out_ref,) = ins, outs
        self.send, self.recv, self.local = sems

    def _plan(self):
        x, y, c = _place()
        chips = [(1 - x, y), (x, 1 - y), (1 - x, 1 - y)]

        def copy(k, block, to, from_shard=False):
            px, py, pc = block
            slot = self.out_ref.at[4 * px + 2 * py + pc]
            return pltpu.make_async_remote_copy(
                src_ref=self.x_ref if from_shard else slot, dst_ref=slot, send_sem=self.send.at[k],
                recv_sem=self.recv.at[k], device_id=to, device_id_type=MESH)

        me, sibling = (x, y, c), (x, y, 1 - c)
        mine = pltpu.make_async_copy(self.x_ref, self.out_ref.at[4 * x + 2 * y + c], self.local)
        first = [copy(0, me, sibling, True)] + [copy(1 + j, me, (*chip, c), True) for j, chip in enumerate(chips)]
        landed = [copy(1 + j, (*chip, c), me) for j, chip in enumerate(chips)]
        passed = [copy(4 + j, (*chip, c), sibling) for j, chip in enumerate(chips)]
        from_sibling = [copy(0, sibling, me)] + [copy(4 + j, (*chip, 1 - c), me) for j, chip in enumerate(chips)]
        return mine, first, landed, passed, from_sibling

    def start(self):
        mine, first, _, _, _ = self._plan()
        mine.start()
        for cp in first:
            cp.start()

    def mid(self):
        _, _, landed, passed, _ = self._plan()
        for got, fwd in zip(landed, passed):
            got.wait_recv()
            fwd.start()

    def finish(self):
        mine, first, _, passed, from_sibling = self._plan()
        for cp in from_sibling:
            cp.wait_recv()
        for cp in first + passed:
            cp.wait_send()
        mine.wait()


class _PairExchange:
    def __init__(self, g8):
        _, r, n = g8.shape
        self.ins = [g8]
        self.outs = [jax.ShapeDtypeStruct((N_CHIP, r, n), g8.dtype)]
        self.sems = [pltpu.SemaphoreType.DMA((N_CHIP,)), pltpu.SemaphoreType.DMA((N_CHIP,))]

    def bind(self, ins, outs, sems):
        (self.g_ref,), (self.out_ref,) = ins, outs
        self.send, self.recv = sems

    def _plan(self):
        x, y, c = _place()
        return [pltpu.make_async_remote_copy(
            src_ref=self.g_ref.at[2 * p + (1 - c)], dst_ref=self.out_ref.at[p], send_sem=self.send.at[p],
            recv_sem=self.recv.at[p], device_id=(x, y, 1 - c), device_id_type=MESH) for p in range(N_CHIP)]

    def start(self):
        for cp in self._plan():
            cp.start()

    def mid(self):
        pass

    def finish(self):
        copies = self._plan()
        for cp in copies:
            cp.wait_recv()
        for cp in copies:
            cp.wait_send()


class _ChipExchange:
    def __init__(self, t4):
        _, r, n = t4.shape
        self.ins = [t4]
        self.outs = [jax.ShapeDtypeStruct((N_CHIP, r, n), t4.dtype)]
        self.sems = [pltpu.SemaphoreType.DMA((3,)), pltpu.SemaphoreType.DMA((3,)), pltpu.SemaphoreType.DMA]

    def bind(self, ins, outs, sems):
        (self.t_ref,), (self.out_ref,) = ins, outs
        self.send, self.recv, self.local_sem = sems

    def _plan(self):
        x, y, c = _place()
        mine = 2 * x + y
        chips = [(1 - x, y), (x, 1 - y), (1 - x, 1 - y)]
        local = pltpu.make_async_copy(self.t_ref.at[mine], self.out_ref.at[mine], self.local_sem)
        sends = [pltpu.make_async_remote_copy(
            src_ref=self.t_ref.at[2 * px + py], dst_ref=self.out_ref.at[mine], send_sem=self.send.at[k],
            recv_sem=self.recv.at[k], device_id=(px, py, c), device_id_type=MESH) for k, (px, py) in enumerate(chips)]
        arrivals = [pltpu.make_async_remote_copy(
            src_ref=self.t_ref.at[mine], dst_ref=self.out_ref.at[2 * px + py], send_sem=self.send.at[k],
            recv_sem=self.recv.at[k], device_id=(px, py, c), device_id_type=MESH) for k, (px, py) in enumerate(chips)]
        return local, sends, arrivals

    def start(self):
        local, sends, _ = self._plan()
        local.start()
        for cp in sends:
            cp.start()

    def mid(self):
        pass

    def finish(self):
        local, sends, arrivals = self._plan()
        for cp in arrivals:
            cp.wait_recv()
        for cp in sends:
            cp.wait_send()
        local.wait()


def _job_specs(jobs):
    ins = [a for j in jobs for a in j.ins]
    outs = [o for j in jobs for o in j.outs]
    sems = [s for j in jobs for s in j.sems]
    return ins, outs, sems


def _bind_jobs(jobs, ins, outs, sems):
    for j in jobs:
        ni, no, ns = len(j.ins), len(j.outs), len(j.sems)
        j.bind(ins[:ni], outs[:no], sems[:ns])
        ins, outs, sems = ins[ni:], outs[no:], sems[ns:]


def _carry(body, jobs, n_in, n_out, grid):
    if not jobs:
        return body
    j_in, j_out, j_sem = (len(v) for v in _job_specs(jobs))
    total = math.prod(grid)
    mid_step = (3 * total) // 4

    def wrapped(*refs):
        ins, rest = refs[:n_in], refs[n_in:]
        jin, rest = rest[:j_in], rest[j_in:]
        outs, rest = rest[:n_out], rest[n_out:]
        jout, rest = rest[:j_out], rest[j_out:]
        scratch, jsem = rest[:len(rest) - j_sem], rest[len(rest) - j_sem:]
        _bind_jobs(jobs, jin, jout, jsem)
        lin = 0
        for d, size in enumerate(grid):
            lin = lin * size + pl.program_id(d)

        @pl.when(lin == 0)
        def _():
            for j in jobs:
                j.start()

        body(*ins, *outs, *scratch)

        @pl.when(lin == mid_step)
        def _():
            for j in jobs:
                j.mid()

        @pl.when(lin == total - 1)
        def _():
            for j in jobs:
                j.finish()

    return wrapped


def _carry_call(jobs):
    ins, outs, sems = _job_specs(jobs)
    return ins, [ANY] * len(ins), [ANY] * len(outs), outs, sems


def _grid_call(body, operands, *, name, grid, in_specs, out_specs, out_shape, scratch_shapes, sem, jobs=()):
    j_ins, j_in_specs, j_out_specs, j_outs, j_sems = _carry_call(jobs)
    return pl.pallas_call(
        _carry(body, jobs, len(in_specs), len(out_specs), grid), name=name, grid=grid,
        in_specs=list(in_specs) + j_in_specs, out_specs=list(out_specs) + j_out_specs,
        out_shape=list(out_shape) + j_outs, scratch_shapes=list(scratch_shapes) + j_sems,
        compiler_params=_params(("arbitrary",) * len(grid) if jobs else sem, has_side_effects=bool(jobs)),
    )(*operands, *j_ins)


def _comm(jobs, name):
    ins, outs, sems = _job_specs(jobs)

    def body(*refs):
        _bind_jobs(jobs, refs[:len(ins)], refs[len(ins):len(ins) + len(outs)], refs[len(ins) + len(outs):])
        for j in jobs:
            j.start()
        for j in jobs:
            j.mid()
        for j in jobs:
            j.finish()

    return pl.pallas_call(
        body, name=name, in_specs=[ANY] * len(ins), out_specs=[ANY] * len(outs), out_shape=outs,
        scratch_shapes=sems, compiler_params=pltpu.CompilerParams(has_side_effects=True),
    )(*ins)


def _all_gather(shard, name):
    return _comm([_Gather(shard)], name)[0]


def _pair_add(g8, got, core, name):
    _, r, n = g8.shape
    tr = _pick(r, (256, 128, 64, 32, 16, 8))

    def body(c_ref, g_ref, r_ref, o_ref):
        o_ref[...] = (g_ref[...].astype(F32) + r_ref[...].astype(F32)).astype(o_ref.dtype)

    grid_spec = pltpu.PrefetchScalarGridSpec(
        num_scalar_prefetch=1, grid=(N_CHIP, r // tr),
        in_specs=[pl.BlockSpec((None, tr, n), lambda p, i, c: (2 * p + c[0], i, 0)),
                  pl.BlockSpec((None, tr, n), lambda p, i, c: (p, i, 0))],
        out_specs=pl.BlockSpec((None, tr, n), lambda p, i, c: (p, i, 0)))
    return pl.pallas_call(
        body, name=name, grid_spec=grid_spec, out_shape=jax.ShapeDtypeStruct((N_CHIP, r, n), g8.dtype),
        compiler_params=_params(("parallel", "parallel")),
    )(core, g8, got)


def _slot_sum(g, name):
    k, r, n = g.shape

    def body(g_ref, o_ref):
        acc = g_ref[0].astype(F32)
        for s in range(1, k):
            acc = acc + g_ref[s].astype(F32)
        o_ref[...] = acc

    return pl.pallas_call(body, name=name, out_shape=jax.ShapeDtypeStruct((r, n), F32),
                          compiler_params=pltpu.CompilerParams(vmem_limit_bytes=VMEM_LIMIT))(g)


def _pack_rows(parts, n_rows, width, name):
    starts = [r0 for _, r0 in parts]

    def body(*refs):
        o_ref = refs[-1]
        o_ref[...] = jnp.zeros_like(o_ref)
        for ref, r0 in zip(refs[:-1], starts):
            k, w = ref.shape
            o_ref[r0:r0 + k, 0:w] = ref[...]

    return pl.pallas_call(body, name=name, out_shape=jax.ShapeDtypeStruct((n_rows, width), F32),
                          compiler_params=pltpu.CompilerParams(vmem_limit_bytes=VMEM_LIMIT))(*[a for a, _ in parts])


def _adamw(w, gparts, m, v, name):
    r, n = w.shape
    k = gparts.shape[0]
    tr = _pick(r, [t for t in (512, 256, 128, 64, 32, 16) if t * n <= ADAMW_TILE_ELEMS])
    c1 = 1.0 - ADAM_B1 ** ADAM_STEP
    c2 = 1.0 - ADAM_B2 ** ADAM_STEP

    def body(w_ref, g_ref, m_ref, v_ref, go_ref, d_ref, mo_ref, vo_ref):
        g = g_ref[0].astype(F32)
        for s in range(1, k):
            g = g + g_ref[s].astype(F32)
        mn = ADAM_B1 * m_ref[...] + (1.0 - ADAM_B1) * g
        vn = ADAM_B2 * v_ref[...] + (1.0 - ADAM_B2) * (g * g)
        go_ref[...] = g
        mo_ref[...] = mn
        vo_ref[...] = vn
        d_ref[...] = -ADAM_LR * ((mn / c1) / (jnp.sqrt(vn / c2) + ADAM_EPS) + ADAM_WD * w_ref[...])

    spec = pl.BlockSpec((tr, n), lambda i: (i, 0))
    return pl.pallas_call(
        body, name=name, grid=(r // tr,),
        in_specs=[spec, pl.BlockSpec((k, tr, n), lambda i: (0, i, 0)), spec, spec],
        out_specs=[spec] * 4, out_shape=[jax.ShapeDtypeStruct((r, n), F32)] * 4,
        compiler_params=_params(("parallel",)),
    )(w, gparts, m, v)


def kernel(x, c, w_ada, b_ada, ln_g, ln_b, a_w_in, a_w_dw, a_b_dw, a_norm_g, a_norm_b, a_w_out, b_w_in, b_q_norm_g, b_w_qb, b_w_out, kv_w_a, kv_norm_g, kv_w_b, loss_target, m_w_ada, m_b_ada, m_ln_g, m_ln_b, m_a_w_in, m_a_w_dw, m_a_b_dw, m_a_norm_g, m_a_norm_b, m_a_w_out, m_b_w_in, m_b_q_norm_g, m_b_w_qb, m_b_w_out, m_kv_w_a, m_kv_norm_g, m_kv_w_b, v_w_ada, v_b_ada, v_ln_g, v_ln_b, v_a_w_in, v_a_w_dw, v_a_b_dw, v_a_norm_g, v_a_norm_b, v_a_w_out, v_b_w_in, v_b_q_norm_g, v_b_w_qb, v_b_w_out, v_kv_w_a, v_kv_norm_g, v_kv_w_b):
    S, D = x.shape[1], x.shape[2]
    C = a_w_out.shape[1] * N_DEV
    QL = b_q_norm_g.shape[1]
    KL = kv_norm_g.shape[0]
    H = kv_w_b.shape[1] * N_DEV // (NOPE + VDIM)
    HL = H // N_DEV
    W = H * VDIM
    assert a_w_dw.shape[1] == CONV_KERNEL and kv_w_a.shape[1] == KL + ROPE and b_w_qb.shape[2] == HL * (NOPE + ROPE)
    assert b_w_in.shape[2] * N_DEV == QL + W and D % (N_DEV * LANES) == 0 and S % CONV_ROWS == 0
    TW = 1024 if W % 1024 == 0 and QL % 1024 == 0 else LANES

    xi, yi, ci = _place()
    idx = 4 * xi + 2 * yi + ci
    core = jnp.reshape(ci, (1,)).astype(jnp.int32)
    x2, tgt = x[0], loss_target[0]
    cos_t, sin_a, sin_b = _rope_tables(S)
    rope_rows = [(cos_t, LANES, 0, False), (sin_a, LANES, 0, False), (sin_b, LANES, 0, False)]

    win0 = _all_gather(a_w_in[0].astype(BF16), "ag_a_w_in")
    sh_wout0 = a_w_out[0].astype(BF16)
    sh_wbin = b_w_in[0].astype(BF16)
    wqb_loc = jnp.pad(b_w_qb[0].reshape(QL, HL, NOPE + ROPE), ((0, 0), (0, 0), (0, QPAD - NOPE - ROPE)))
    sh_wqb = wqb_loc.reshape(QL, HL * QPAD).astype(BF16)
    sh_wbout = b_w_out[0].astype(BF16)
    kva_w = KL + LANES
    sh_wkva = jnp.pad(kv_w_a, ((0, 0), (0, kva_w - KL - ROPE))).astype(BF16)
    sh_wkvb = kv_w_b.astype(BF16)

    cl = C // N_DEV
    v0 = CONV_TOP
    small_loc = _pack_rows([(a_w_dw[0], 0), (a_b_dw, v0), (a_norm_g, v0 + 1), (a_norm_b, v0 + 2)], 40, cl, "pack_conv_small")
    small = _all_gather(small_loc, "ag_conv_small")
    small = jnp.transpose(small, (1, 0, 2)).reshape(40, C)
    w_dw, b_dw, g_cn, b_cn = small[:CONV_KERNEL], small[v0:v0 + 1], small[v0 + 1:v0 + 2], small[v0 + 2:v0 + 3]

    c_all = _all_gather(jnp.broadcast_to(c, (SUBLANES, D)), "ag_c")[:, 0, :]
    c_pad = jnp.concatenate([c_all, jnp.zeros((16 - N_DEV, D), F32)], axis=0)
    (sc,), _ = _rowwise(lambda r, v: ([_silu(r[0])], []), [(c_pad, D, 0, False)], [], [(BF16, D, D)], [],
                        n_rows=16, tr=16, name="silu_c")
    nl = 3 * D // N_DEV
    zsc = jnp.zeros_like(sc)
    sc2 = jnp.concatenate([jnp.concatenate([sc, zsc], axis=1), jnp.concatenate([zsc, sc], axis=1)], axis=0)
    mod_loc = _mm(sc2, w_ada.reshape(DEPTH * D, nl), "nn", F32, "mod")
    mod_all = jnp.transpose(_all_gather(mod_loc, "ag_mod"), (1, 0, 2)).reshape(DEPTH, 16, 3 * D)
    mods = []
    for l in range(DEPTH):
        row = lax.dynamic_slice(mod_all[l], (idx, 0), (1, 3 * D)) + b_ada[l][None, :]
        mods.append((row[:, :D], row[:, D:2 * D], row[:, 2 * D:]))
    (shift0, scale0, gate0), (shift1, scale1, gate1) = mods
    vec = lambda a, w=None: (a, a.shape[1] if w is None else w, 0, False)

    def f_mod0(r, v):
        xh, _ = _ln_stats(r[0])
        return [xh * (1.0 + v[0]) + v[1]], []
    (h0,), _ = _rowwise(f_mod0, [(x2, D, 0, False)], [vec(scale0), vec(shift0)], [(BF16, D, D)], [],
                        n_rows=S, tr=128, name="l0_modulate")
    u0, wout0, wbin = _mm(h0, win0, "nn", F32, "l0_in", b_slots=True,
                          jobs=[_Gather(sh_wout0), _Gather(sh_wbin)])
    wout0 = wout0.reshape(C, D)
    v2 = _conv_fwd(u0, w_dw, b_dw, S, C, "l0_conv")

    def f_gate0(r, v):
        xh, _ = _ln_stats(r[0])
        return [_silu(xh * v[0] + v[1]) * _silu(r[1])], []
    (p0,), _ = _rowwise(f_gate0, [(v2, C, 0, False), (u0, C, 2, False)], [vec(g_cn), vec(b_cn)], [(BF16, C, C)], [],
                        n_rows=S, tr=128, name="l0_gate")
    o0, wkva, wkvb = _mm(p0, wout0, "nn", F32, "l0_out", jobs=[_Gather(sh_wkva), _Gather(sh_wkvb)])
    wkva = wkva.reshape(D, kva_w)

    def f_res0(r, v):
        gate, g0, b0, sc1, sh1 = v
        xh, _ = _ln_stats(ALPHA * r[0] + (1.0 + gate) * r[1])
        x1 = xh * g0 + b0
        xh1, _ = _ln_stats(x1)
        return [x1, x1, xh1 * (1.0 + sc1) + sh1], []
    (x1, x1b, h1), _ = _rowwise(f_res0, [(x2, D, 0, False), (o0, D, 0, False)],
                                [vec(gate0), vec(ln_g[0:1]), vec(ln_b[0:1]), vec(scale1), vec(shift1)],
                                [(F32, D, D), (BF16, D, D), (BF16, D, D)], [], n_rows=S, tr=128, name="l0_residual")

    kva = _mm(x1b, wkva, "nn", F32, "kv_a")
    g_kv = kv_norm_g[None, :]

    def f_kvn(r, v):
        xh, _ = _rms_stats(r[0][:, :KL])
        return [xh * v[0], _rope_fwd(r[0][:, KL:], r[1], r[2], r[3])], []
    (ckv, krp), _ = _rowwise(f_kvn, [(kva, kva_w, 0, False)] + rope_rows, [vec(g_kv)],
                             [(BF16, KL, KL), (BF16, LANES, LANES)], [], n_rows=S, tr=256, name="kv_norm_rope")
    kvh = _mm(ckv, wkvb, "nn", BF16, "kv_b", b_slots=True)

    u1, wqb = _mm(h1, wbin, "nn", F32, "l1_in", b_slots=True, jobs=[_Gather(sh_wqb)])
    g_q = b_q_norm_g[0:1]

    def f_qn(r, v):
        xh, _ = _rms_stats(r[0])
        return [xh * v[0]], []
    (cqn,), _ = _rowwise(f_qn, [(u1, QL, 0, False)], [vec(g_q)], [(BF16, QL, QL)], [], n_rows=S, tr=256, name="q_norm")
    qf = _mm(cqn, wqb, "nn", F32, "q_b", b_slots=True)

    rope_tab = jnp.concatenate([cos_t[:, :ROPE], sin_a[:, :ROPE // 2], sin_b[:, ROPE // 2:ROPE]], axis=1)
    oa, lse, qb, p1, wbout = _attn_fwd(qf, rope_tab, kvh, krp, u1, QL // VDIM, S, H, "attn_fwd", jobs=[_Gather(sh_wbout)])
    wbout = wbout.reshape(W, D)
    zc0 = QL // TW
    o1 = _mm(p1, wbout, "nn", F32, "l1_out")

    def f_head(r, v):
        gate, g1, b1 = v
        xh, rstd = _ln_stats(ALPHA * r[0] + (1.0 + gate) * r[1])
        err = xh * g1 + b1 - r[2]
        dy = err * (1.0 / D)
        dr = _ln_bwd(dy * g1, xh, rstd)
        return [dr, (1.0 + gate) * dr], [_colsum(err * err), _colsum(dy * xh), _colsum(dy), _colsum(dr * r[1])]
    (dr1, do1b), (loss_row, dlng1, dlnb1, dgate1) = _rowwise(
        f_head, [(x1, D, 0, False), (o1, D, 0, False), (tgt, D, 0, False)], [vec(gate1), vec(ln_g[1:2]), vec(ln_b[1:2])],
        [(F32, D, D), (BF16, D, D)], [D, D, D, D], n_rows=S, tr=64, name="head_loss")
    loss = lax.psum(0.5 / D * jnp.sum(loss_row), ("x", "y", "c"))

    dp1 = _mm(do1b, wbout, "nt", F32, "l1_out_dx")
    g_wbout = _mm(p1, do1b, "tn", BF16, "l1_out_dw").reshape(N_DEV, W // N_DEV, D)

    def f_dgate1(r, v):
        dp, o, z = r
        return [dp * _silu(z), dp * o * _dsilu(z)], []
    (doa, dz1), _, (got,) = _rowwise(f_dgate1, [(dp1, TW, 0, True), (oa, TW, 0, True), (u1, TW, zc0, True)], [],
                                     [(BF16, W, TW), (BF16, W, TW)], [], n_rows=S, tr=512 if S % 512 == 0 else S,
                                     name="l1_gate_bwd", nc=W // TW, jobs=[_PairExchange(g_wbout)])
    t_wbout = _pair_add(g_wbout, got, core, "rs_b_w_out_add")
    dqb, dkvh, dkrp, r_wbout = _attn_bwd(qb, rope_tab, kvh, krp, doa, oa, lse, S, H, "attn_bwd",
                                         jobs=[_ChipExchange(t_wbout)])
    g_wqb = _mm(cqn, dqb, "tn", BF16, "q_b_dw", out_slots=True)
    dcqn, got = _mm(dqb, wqb, "nt", F32, "q_b_dx", b_slots=True, jobs=[_PairExchange(g_wqb)])
    t_wqb = _pair_add(g_wqb, got, core, "rs_b_w_qb_add")

    def f_qn_bwd(r, v):
        xh, rr = _rms_stats(r[0])
        return [_rms_bwd(r[1] * v[0], xh, rr)], [_colsum(r[1] * xh)]
    (dcq,), (dg_q,) = _rowwise(f_qn_bwd, [(u1, QL, 0, False), (dcqn, QL, 0, False)], [vec(g_q)], [(BF16, QL, QL)], [QL],
                               n_rows=S, tr=256, name="q_norm_bwd")
    du1 = jnp.concatenate([dcq, dz1], axis=1)
    g_wbin, r_wqb = _mm(h1, du1, "tn", BF16, "l1_in_dw", out_slots=True, jobs=[_ChipExchange(t_wqb)])
    dh1, got = _mm(du1, wbin, "nt", F32, "l1_in_dx", b_slots=True, jobs=[_PairExchange(g_wbin)])
    t_wbin = _pair_add(g_wbin, got, core, "rs_b_w_in_add")

    g_wkvb = _mm(ckv, dkvh, "tn", BF16, "kv_b_dw", out_slots=True)
    dckv, got = _mm(dkvh, wkvb, "nt", F32, "kv_b_dx", b_slots=True, jobs=[_PairExchange(g_wkvb)])
    t_wkvb = _pair_add(g_wkvb, got, core, "rs_kv_w_b_add")

    def f_kvn_bwd(r, v):
        xh, rr = _rms_stats(r[0][:, :KL])
        dck = _rms_bwd(r[1] * v[0], xh, rr)
        dkr = _rope_bwd(r[2], r[3], r[4], r[5])
        return [jnp.concatenate([dck, dkr], axis=1)], [_colsum(r[1] * xh)]
    (dkva,), (dg_kv,) = _rowwise(f_kvn_bwd, [(kva, kva_w, 0, False), (dckv, KL, 0, False), (dkrp, LANES, 0, False)] + rope_rows,
                                 [vec(g_kv)], [(BF16, kva_w, kva_w)], [KL], n_rows=S, tr=256, name="kv_norm_rope_bwd")
    g_wkva = _mm(x1b, dkva, "tn", BF16, "kv_a_dw").reshape(N_DEV, D // N_DEV, kva_w)
    dx1_kv, got = _mm(dkva, wkva, "nt", F32, "kv_a_dx", jobs=[_PairExchange(g_wkva)])
    t_wkva = _pair_add(g_wkva, got, core, "rs_kv_w_a_add")

    def f_res0_bwd(r, v):
        dr1_, dxkv, dh, x1_, x_, o_ = r
        sc1, gate, g0 = v
        xh1, rstd1 = _ln_stats(x1_)
        dx1 = ALPHA * dr1_ + dxkv + _ln_bwd(dh * (1.0 + sc1), xh1, rstd1)
        xh0, rstd0 = _ln_stats(ALPHA * x_ + (1.0 + gate) * o_)
        dr0 = _ln_bwd(dx1 * g0, xh0, rstd0)
        return ([dr0, (1.0 + gate) * dr0],
                [_colsum(dh * xh1), _colsum(dh), _colsum(dx1 * xh0), _colsum(dx1), _colsum(dr0 * o_)])
    (dr0, do0b), (dscale1, dshift1, dlng0, dlnb0, dgate0) = _rowwise(
        f_res0_bwd, [(a, D, 0, False) for a in (dr1, dx1_kv, dh1, x1, x2, o0)], [vec(scale1), vec(gate0), vec(ln_g[0:1])],
        [(F32, D, D), (BF16, D, D)], [D] * 5, n_rows=S, tr=64, name="l0_residual_bwd")

    dp0, r_wkvb, r_wkva = _mm(do0b, wout0, "nt", F32, "l0_out_dx",
                              jobs=[_ChipExchange(t_wkvb), _ChipExchange(t_wkva)])
    g_wout0 = _mm(p0, do0b, "tn", BF16, "l0_out_dw").reshape(N_DEV, C // N_DEV, D)

    def f_gate0_bwd(r, v):
        dp, v2_, z = r
        g, b = v
        xh, rstd = _ln_stats(v2_)
        v3 = xh * g + b
        dv4 = dp * _silu(z)
        dz = dp * _silu(v3) * _dsilu(z)
        dv3 = dv4 * _dsilu(v3)
        dv2_ = _ln_bwd(dv3 * g, xh, rstd)
        return [dz, dv2_], [_colsum(dv3 * xh), _colsum(dv3), _colsum(dv2_)]
    (dz0, dv2), (dg_cn, db_cn, db_dw) = _rowwise(
        f_gate0_bwd, [(dp0, C, 0, False), (v2, C, 0, False), (u0, C, 2, False)], [vec(g_cn), vec(b_cn)],
        [(BF16, C, C), (F32, C, C)], [C, C, C], n_rows=S, tr=64, name="l0_gate_bwd")
    da0, dg0, dw_dw, r_wbin, got = _conv_bwd(u0, dv2, w_dw, S, C, "l0_conv_bwd",
                                             jobs=[_ChipExchange(t_wbin), _PairExchange(g_wout0)])
    t_wout0 = _pair_add(g_wout0, got, core, "rs_a_w_out_add")
    du0 = jnp.concatenate([da0, dg0, dz0], axis=1)
    g_win0_a, r_wout0 = _mm(h0, du0, "tn", BF16, "l0_in_dw_a", out_slots=True, m_rows=(0, D // 2),
                            jobs=[_ChipExchange(t_wout0)])
    g_win0_b, got = _mm(h0, du0, "tn", BF16, "l0_in_dw_b", out_slots=True, m_rows=(D // 2, D // 2),
                        jobs=[_PairExchange(g_win0_a)])
    t_win0_a = _pair_add(g_win0_a, got, core, "rs_a_w_in_add_a")
    dh0, r_win0_a, got = _mm(du0, win0, "nt", F32, "l0_in_dx", b_slots=True,
                             jobs=[_ChipExchange(t_win0_a), _PairExchange(g_win0_b)])
    t_win0_b = _pair_add(g_win0_b, got, core, "rs_a_w_in_add_b")

    def f_mod0_bwd(r, v):
        xh, rstd = _ln_stats(r[2])
        return [ALPHA * r[0] + _ln_bwd(r[1] * (1.0 + v[0]), xh, rstd)], [_colsum(r[1] * xh), _colsum(r[1])]
    (grad_x,), (dscale0, dshift0), (r_win0_b,) = _rowwise(
        f_mod0_bwd, [(dr0, D, 0, False), (dh0, D, 0, False), (x2, D, 0, False)], [vec(scale0)], [(F32, D, D)], [D, D],
        n_rows=S, tr=128, name="l0_modulate_bwd", jobs=[_ChipExchange(t_win0_b)])
    r_win0 = jnp.concatenate([r_win0_a, r_win0_b], axis=1)

    assert C == D and QL <= D and KL <= D
    n_small = 48
    singles = [dlng0, dlng1, dlnb0, dlnb1, dshift0, dscale0, dgate0, dshift1, dscale1, dgate1, db_dw, dg_cn, db_cn, dg_q, dg_kv]
    rows_loc = _pack_rows([(dw_dw, 0)] + [(a, v0 + i) for i, a in enumerate(singles)], n_small, D, "pack_small_grads")
    parts = _all_gather(rows_loc, "ag_small_grads")
    tot = _slot_sum(parts, "small_grads_sum")
    g_ln_g, g_ln_b = tot[v0:v0 + 2], tot[v0 + 2:v0 + 4]
    g_b_ada = tot[v0 + 4:v0 + 10].reshape(DEPTH, 3 * D)
    loc_cols = lambda a: lax.dynamic_slice(a, (0, idx * cl), (a.shape[0], cl))
    g_a_w_dw = loc_cols(tot[0:CONV_KERNEL])
    g_a_b_dw = loc_cols(tot[v0 + 10:v0 + 11])
    g_a_norm_g = loc_cols(tot[v0 + 11:v0 + 12])
    g_a_norm_b = loc_cols(tot[v0 + 12:v0 + 13])
    g_q_norm = tot[v0 + 13:v0 + 14, :QL]
    g_kv_norm = tot[v0 + 14:v0 + 15, :KL]

    dmods = []
    for l in range(DEPTH):
        dmod = parts[:, v0 + 4 + 3 * l:v0 + 7 + 3 * l, :].reshape(N_DEV, 3 * D)
        dmods += [lax.dynamic_slice(dmod, (0, idx * nl), (N_DEV, nl)), jnp.zeros((16 - N_DEV, nl), F32)]
    g_w_ada = _mm(sc2, jnp.concatenate(dmods, axis=0), "tn", F32, "mod_dw").reshape(DEPTH, D, nl)

    r_wqb = r_wqb.reshape(N_CHIP, QL, HL, QPAD)[..., :NOPE + ROPE].reshape(N_CHIP, QL, HL * (NOPE + ROPE))
    r_wkva = r_wkva[..., :KL + ROPE]

    def upd(name, w, gparts, m, v):
        shp = w.shape
        r2 = (1, shp[0]) if len(shp) == 1 else (math.prod(shp[:-1]), shp[-1])
        gp = gparts.reshape((gparts.shape[0],) + r2)
        res = _adamw(w.reshape(r2), gp, m.reshape(r2), v.reshape(r2), "adamw_" + name)
        return [a.reshape(shp) for a in res]

    one = lambda g: g[None]
    results = [
        upd("w_ada", w_ada, one(g_w_ada), m_w_ada, v_w_ada),
        upd("b_ada", b_ada, one(g_b_ada), m_b_ada, v_b_ada),
        upd("ln_g", ln_g, one(g_ln_g), m_ln_g, v_ln_g),
        upd("ln_b", ln_b, one(g_ln_b), m_ln_b, v_ln_b),
        upd("a_w_in", a_w_in, r_win0[:, None], m_a_w_in, v_a_w_in),
        upd("a_w_dw", a_w_dw, one(g_a_w_dw[None]), m_a_w_dw, v_a_w_dw),
        upd("a_b_dw", a_b_dw, one(g_a_b_dw), m_a_b_dw, v_a_b_dw),
        upd("a_norm_g", a_norm_g, one(g_a_norm_g), m_a_norm_g, v_a_norm_g),
        upd("a_norm_b", a_norm_b, one(g_a_norm_b), m_a_norm_b, v_a_norm_b),
        upd("a_w_out", a_w_out, r_wout0[:, None], m_a_w_out, v_a_w_out),
        upd("b_w_in", b_w_in, r_wbin[:, None], m_b_w_in, v_b_w_in),
        upd("b_q_norm_g", b_q_norm_g, one(g_q_norm), m_b_q_norm_g, v_b_q_norm_g),
        upd("b_w_qb", b_w_qb, r_wqb[:, None], m_b_w_qb, v_b_w_qb),
        upd("b_w_out", b_w_out, r_wbout[:, None], m_b_w_out, v_b_w_out),
        upd("kv_w_a", kv_w_a, r_wkva, m_kv_w_a, v_kv_w_a),
        upd("kv_norm_g", kv_norm_g, one(g_kv_norm[0]), m_kv_norm_g, v_kv_norm_g),
        upd("kv_w_b", kv_w_b, r_wkvb, m_kv_w_b, v_kv_w_b),
    ]
    grads, deltas, new_m, new_v = zip(*results)
    return (loss, grad_x[None], *grads, *deltas, *new_m, *new_v)
```

```python
import functools
import math

import jax
import jax.numpy as jnp
from jax import lax
from jax.experimental import pallas as pl
from jax.experimental.pallas import tpu as pltpu

F32 = jnp.float32
BF16 = jnp.bfloat16
MESH = pl.DeviceIdType.MESH
ANY = pl.BlockSpec(memory_space=pl.ANY)

N_DEV = 8
N_CHIP = 4
V7X_VMEM_BYTES = 64 * 1024 * 1024
VMEM_LIMIT = V7X_VMEM_BYTES - 8 * 1024 * 1024
LANES = 128
SUBLANES = 8

DEPTH = 2
CONV_KERNEL = 31
NOPE = 128
ROPE = 64
VDIM = 128
QPAD = NOPE + LANES
ROPE_BASE = 10000.0
LN_EPS = 1e-5
RMS_EPS = 1e-6
ALPHA = (2.0 * DEPTH) ** 0.25
ATTN_SCALE = (NOPE + ROPE) ** -0.5
ADAM_LR = 0.001
ADAM_B1 = 0.9
ADAM_B2 = 0.999
ADAM_EPS = 1e-08
ADAM_WD = 0.01
ADAM_STEP = 10

CONV_TOP = 32
CONV_ROWS = 64
ATTN_BLOCK = 512
ATTN_CHUNK = 32
MM_INPUT_BYTES = 32 * 1024 * 1024
ADAMW_TILE_ELEMS = 256 * 1024


def _pick(dim, prefs):
    for p in prefs:
        if dim % p == 0:
            return p
    return dim


def _params(sem, **kw):
    return pltpu.CompilerParams(dimension_semantics=sem, vmem_limit_bytes=VMEM_LIMIT, **kw)


_DIMS = {"nn": (((1,), (0,)), ((), ())), "nt": (((1,), (1,)), ((), ())), "tn": (((0,), (0,)), ((), ()))}


def _mm(a, b, mode, out_dtype, name, *, b_slots=False, out_slots=False, m_rows=None, jobs=()):
    m_first = 0
    if mode == "tn":
        K, M = a.shape
        if m_rows is not None:
            m_first, M = m_rows
    else:
        M, K = a.shape
    if b_slots:
        assert mode in ("nn", "nt")
        if mode == "nn":
            _, kb, n_loc = b.shape
            N, slot_w = N_DEV * n_loc, n_loc
        else:
            _, N, k_loc = b.shape
            kb, slot_w = N_DEV * k_loc, k_loc
    else:
        if mode == "nt":
            N, kb = b.shape
        else:
            kb, N = b.shape
        slot_w = N // N_DEV if out_slots else None
    assert kb == K, (name, a.shape, b.shape)
    tm = _pick(M, (1024, 512, 256, 128, 64, 32, 16))
    n_unit = slot_w if (mode == "nn" and b_slots) or out_slots else N
    k_unit = slot_w if (mode == "nt" and b_slots) else K
    tn = _pick(n_unit, (1024, 1152, 768, 640, 512, 384, 256, 128))
    step_bytes = 2 * (tm * a.dtype.itemsize + tn * b.dtype.itemsize)
    tk = _pick(k_unit, [t for t in (4096, 2048, 1536, 1152, 1024, 768, 640, 512, 384, 256, 128, 64, 32, 16)
                        if t * step_bytes <= MM_INPUT_BYTES])
    nk = K // tk

    if mode == "tn":
        assert m_first % tm == 0
        a_spec = pl.BlockSpec((tk, tm), lambda i, j, k: (k, i + m_first // tm))
    else:
        a_spec = pl.BlockSpec((tm, tk), lambda i, j, k: (i, k))
    if b_slots and mode == "nn":
        per = slot_w // tn
        b_spec = pl.BlockSpec((None, tk, tn), lambda i, j, k: (j // per, k, j % per))
    elif b_slots and mode == "nt":
        per = slot_w // tk
        b_spec = pl.BlockSpec((None, tn, tk), lambda i, j, k: (k // per, j, k % per))
    elif mode == "nt":
        b_spec = pl.BlockSpec((tn, tk), lambda i, j, k: (j, k))
    else:
        b_spec = pl.BlockSpec((tk, tn), lambda i, j, k: (k, j))
    if out_slots:
        per_o = slot_w // tn
        o_spec = pl.BlockSpec((None, tm, tn), lambda i, j, k: (j // per_o, i, j % per_o))
        o_shape = jax.ShapeDtypeStruct((N_DEV, M, slot_w), out_dtype)
    else:
        o_spec = pl.BlockSpec((tm, tn), lambda i, j, k: (i, j))
        o_shape = jax.ShapeDtypeStruct((M, N), out_dtype)
    dims = _DIMS[mode]

    def step(a_ref, b_ref):
        return lax.dot_general(a_ref[...].astype(BF16), b_ref[...].astype(BF16), dims, preferred_element_type=F32)

    if nk == 1:
        def body(a_ref, b_ref, o_ref):
            o_ref[...] = step(a_ref, b_ref).astype(out_dtype)
    else:
        def body(a_ref, b_ref, o_ref, acc_ref):
            k = pl.program_id(2)

            @pl.when(k == 0)
            def _():
                acc_ref[...] = jnp.zeros_like(acc_ref)

            acc_ref[...] += step(a_ref, b_ref)

            @pl.when(k == nk - 1)
            def _():
                o_ref[...] = acc_ref[...].astype(out_dtype)

    res = _grid_call(body, (a, b), name=name, grid=(M // tm, N // tn, nk), in_specs=[a_spec, b_spec],
                     out_specs=[o_spec], out_shape=[o_shape],
                     scratch_shapes=[] if nk == 1 else [pltpu.VMEM((tm, tn), F32)],
                     sem=("parallel", "parallel", "arbitrary"), jobs=jobs)
    return tuple(res) if jobs else res[0]


def _rowwise(fn, rows, vecs, outs, accs, *, n_rows, tr, name, nc=1, jobs=()):
    nr = n_rows // tr
    n_in, n_vec, n_out, n_acc = len(rows), len(vecs), len(outs), len(accs)

    def cmap(cb0, follow):
        if follow:
            return lambda j, i: (i, cb0 + j)
        return lambda j, i: (i, cb0)

    def vmap_(cb0, follow):
        if follow:
            return lambda j, i: (0, cb0 + j)
        return lambda j, i: (0, cb0)

    in_specs = [pl.BlockSpec((tr, w), cmap(cb0, fol)) for (_, w, cb0, fol) in rows]
    in_specs += [pl.BlockSpec((1, w), vmap_(cb0, fol)) for (_, w, cb0, fol) in vecs]
    out_specs = [pl.BlockSpec((tr, bw), lambda j, i: (i, j)) for (_, _, bw) in outs]
    out_specs += [pl.BlockSpec((1, w), lambda j, i: (0, j)) for w in accs]
    out_shape = [jax.ShapeDtypeStruct((n_rows, tw), dt) for (dt, tw, _) in outs]
    out_shape += [jax.ShapeDtypeStruct((1, w * nc), F32) for w in accs]

    def body(*refs):
        ins = [r[...] for r in refs[:n_in]]
        vs = [r[...] for r in refs[n_in:n_in + n_vec]]
        o_refs = refs[n_in + n_vec:n_in + n_vec + n_out]
        a_refs = refs[n_in + n_vec + n_out:]
        o_vals, a_vals = fn(ins, vs)
        for r, v in zip(o_refs, o_vals):
            r[...] = v.astype(r.dtype)
        if n_acc:
            i = pl.program_id(1)

            @pl.when(i == 0)
            def _():
                for r in a_refs:
                    r[...] = jnp.zeros_like(r)

            for r, v in zip(a_refs, a_vals):
                r[...] += v

    res = _grid_call(
        body, [r[0] for r in rows] + [v[0] for v in vecs], name=name, grid=(nc, nr), in_specs=in_specs,
        out_specs=out_specs, out_shape=out_shape, scratch_shapes=[], sem=("parallel", "arbitrary"), jobs=jobs)
    outs, accs_out, extra = res[:n_out], res[n_out:n_out + n_acc], res[n_out + n_acc:]
    return (outs, accs_out, extra) if jobs else (outs, accs_out)


def _colsum(x):
    return jnp.sum(x, axis=0, keepdims=True)


def _ln_stats(x):
    mu = jnp.mean(x, axis=-1, keepdims=True)
    xc = x - mu
    var = jnp.mean(xc * xc, axis=-1, keepdims=True)
    rstd = lax.rsqrt(var + LN_EPS)
    return xc * rstd, rstd


def _ln_bwd(dxhat, xhat, rstd):
    return rstd * (dxhat - jnp.mean(dxhat, axis=-1, keepdims=True)
                   - xhat * jnp.mean(dxhat * xhat, axis=-1, keepdims=True))


def _rms_stats(x):
    r = lax.rsqrt(jnp.mean(x * x, axis=-1, keepdims=True) + RMS_EPS)
    return x * r, r


def _rms_bwd(dxhat, xhat, r):
    return r * (dxhat - xhat * jnp.mean(dxhat * xhat, axis=-1, keepdims=True))


def _silu(x):
    return x * jax.nn.sigmoid(x)


def _dsilu(x):
    s = jax.nn.sigmoid(x)
    return s * (1.0 + x * (1.0 - s))


def _rope_fwd(x, cos_t, sin_a, sin_b):
    return x * cos_t + pltpu.roll(x, LANES - ROPE // 2, 1) * sin_a + pltpu.roll(x, ROPE // 2, 1) * sin_b


def _rope_bwd(dy, cos_t, sin_a, sin_b):
    return dy * cos_t + pltpu.roll(dy * sin_a, ROPE // 2, 1) + pltpu.roll(dy * sin_b, LANES - ROPE // 2, 1)


def _rope1(x, tab):
    lane = lax.broadcasted_iota(jnp.int32, x.shape, 1)
    x = jnp.where(lane < ROPE, x, 0.0)
    swapped = pltpu.roll(x, LANES - ROPE // 2, 1) + pltpu.roll(x, ROPE // 2, 1)
    return jnp.where(lane < ROPE, x * tab + swapped * pltpu.roll(tab, ROPE, 1), 0.0)


def _rope1_bwd(dy, tab):
    lane = lax.broadcasted_iota(jnp.int32, dy.shape, 1)
    dy = jnp.where(lane < ROPE, dy, 0.0)
    g = dy * pltpu.roll(tab, ROPE, 1)
    return jnp.where(lane < ROPE, dy * tab + pltpu.roll(g, LANES - ROPE // 2, 1) + pltpu.roll(g, ROPE // 2, 1), 0.0)


def _rope_tables(seq):
    half = ROPE // 2
    inv_freq = ROPE_BASE ** (-jnp.arange(half, dtype=F32) / half)
    ang = jnp.arange(seq, dtype=jnp.int32).astype(F32)[:, None] * inv_freq[None, :]
    cos, sin = jnp.cos(ang), jnp.sin(ang)
    z = jnp.zeros_like(cos)
    cos_t = jnp.concatenate([cos, cos, z, z], axis=1)
    sin_a = jnp.concatenate([-sin, z, z, z], axis=1)
    sin_b = jnp.concatenate([z, sin, z, z], axis=1)
    return cos_t, sin_a, sin_b


def _shifted(win, n):
    return [win] + [pltpu.roll(win, n - j, 0) for j in range(1, SUBLANES)]


def _conv_fwd(u0, w_dw, b_dw, seq, width, name, jobs=()):
    tc = LANES
    nct = width // tc
    rows = CONV_ROWS
    n_win = rows + CONV_TOP

    def body(a_ref, g_ref, w_ref, b_ref, o_ref, vpad):
        vpad[0:CONV_TOP, :] = jnp.zeros((CONV_TOP, tc), F32)
        vpad[CONV_TOP:, :] = a_ref[...] * jax.nn.sigmoid(g_ref[...])

        def chunk(i, carry):
            t0 = pl.multiple_of(i * rows, rows)
            rot = _shifted(vpad[pl.ds(t0, n_win), :], n_win)
            acc = jnp.zeros((rows, tc), F32) + b_ref[...]
            for k in range(CONV_KERNEL):
                s = k + CONV_TOP - (CONV_KERNEL - 1)
                q, j = divmod(s, SUBLANES)
                acc = acc + w_ref[k:k + 1, :] * rot[j][SUBLANES * q:SUBLANES * q + rows]
            o_ref[pl.ds(t0, rows), :] = acc
            return carry

        lax.fori_loop(0, seq // rows, chunk, 0)

    return _grid_call(
        body, (u0, u0, w_dw, b_dw), name=name, grid=(nct,),
        in_specs=[pl.BlockSpec((seq, tc), lambda j: (0, j)), pl.BlockSpec((seq, tc), lambda j: (0, nct + j)),
                  pl.BlockSpec((CONV_KERNEL, tc), lambda j: (0, j)), pl.BlockSpec((1, tc), lambda j: (0, j))],
        out_specs=[pl.BlockSpec((seq, tc), lambda j: (0, j))],
        out_shape=[jax.ShapeDtypeStruct((seq, width), F32)],
        scratch_shapes=[pltpu.VMEM((seq + CONV_TOP, tc), F32)],
        sem=("parallel",), jobs=jobs)


def _conv_bwd(u0, dv2, w_dw, seq, width, name, jobs=()):
    tc = LANES
    nct = width // tc
    rows = CONV_ROWS
    n_win = rows + CONV_TOP
    nsh = CONV_KERNEL * SUBLANES

    def body(a_ref, g_ref, d_ref, w_ref, da_ref, dg_ref, dw_ref, vpad, dpad, dwacc):
        vpad[0:CONV_TOP, :] = jnp.zeros((CONV_TOP, tc), F32)
        vpad[CONV_TOP:, :] = a_ref[...] * jax.nn.sigmoid(g_ref[...])
        dpad[0:seq, :] = d_ref[...]
        dpad[seq:, :] = jnp.zeros((CONV_TOP, tc), F32)
        dwacc[...] = jnp.zeros_like(dwacc)

        def chunk(i, carry):
            t0 = pl.multiple_of(i * rows, rows)
            vrot = _shifted(vpad[pl.ds(t0, n_win), :], n_win)
            drot = _shifted(dpad[pl.ds(t0, n_win), :], n_win)
            dcur = drot[0][0:rows]
            dv1 = jnp.zeros((rows, tc), F32)
            for k in range(CONV_KERNEL):
                q, j = divmod(CONV_KERNEL - 1 - k, SUBLANES)
                dv1 = dv1 + w_ref[k:k + 1, :] * drot[j][SUBLANES * q:SUBLANES * q + rows]
                q, j = divmod(k + CONV_TOP - (CONV_KERNEL - 1), SUBLANES)
                prod = dcur * vrot[j][SUBLANES * q:SUBLANES * q + rows]
                part = prod[0:SUBLANES]
                for r in range(1, rows // SUBLANES):
                    part = part + prod[SUBLANES * r:SUBLANES * (r + 1)]
                dwacc[SUBLANES * k:SUBLANES * (k + 1), :] += part
            a = a_ref[pl.ds(t0, rows), :]
            sg = jax.nn.sigmoid(g_ref[pl.ds(t0, rows), :])
            da_ref[pl.ds(t0, rows), :] = (dv1 * sg).astype(BF16)
            dg_ref[pl.ds(t0, rows), :] = (dv1 * a * sg * (1.0 - sg)).astype(BF16)
            return carry

        lax.fori_loop(0, seq // rows, chunk, 0)
        for k in range(CONV_KERNEL):
            dw_ref[k:k + 1, :] = jnp.sum(dwacc[SUBLANES * k:SUBLANES * (k + 1), :], axis=0, keepdims=True)

    return _grid_call(
        body, (u0, u0, dv2, w_dw), name=name, grid=(nct,),
        in_specs=[pl.BlockSpec((seq, tc), lambda j: (0, j)), pl.BlockSpec((seq, tc), lambda j: (0, nct + j)),
                  pl.BlockSpec((seq, tc), lambda j: (0, j)), pl.BlockSpec((CONV_KERNEL, tc), lambda j: (0, j))],
        out_specs=[pl.BlockSpec((seq, tc), lambda j: (0, j)), pl.BlockSpec((seq, tc), lambda j: (0, j)),
                   pl.BlockSpec((CONV_KERNEL, tc), lambda j: (0, j))],
        out_shape=[jax.ShapeDtypeStruct((seq, width), BF16), jax.ShapeDtypeStruct((seq, width), BF16),
                   jax.ShapeDtypeStruct((CONV_KERNEL, width), F32)],
        scratch_shapes=[pltpu.VMEM((seq + CONV_TOP, tc), F32), pltpu.VMEM((seq + CONV_TOP, tc), F32),
                        pltpu.VMEM((nsh, tc), F32)],
        sem=("parallel",), jobs=jobs)


def _lanes(x, n):
    return jnp.tile(x, (1, n // LANES))


def _diag_mask(r0, rows, t, transposed=False):
    r = lax.broadcasted_iota(jnp.int32, (rows, t), 0) + r0
    c = lax.broadcasted_iota(jnp.int32, (rows, t), 1)
    return (c >= r) if transposed else (r >= c)


def _attn_fwd(qf, tab, kv, kr, u1, z_block0, seq, heads, name, jobs=()):
    t = min(ATTN_BLOCK, seq)
    nb = seq // t
    ch = ATTN_CHUNK

    def body(qf_ref, tab_ref, kv_ref, kr_ref, z_ref, o_ref, lse_ref, qb_ref, g_ref, kf_s, q_s, sa_s, sb_s, p_s, m_s, l_s, c_s,
             acc_s):
        qi = pl.program_id(1)

        @pl.when(qi == 0)
        def _():
            kf_s[:, 0:NOPE] = kv_ref[:, 0:NOPE]
            kf_s[:, NOPE:] = kr_ref[...]

        q_s[:, 0:NOPE] = (qf_ref[:, 0:NOPE] * ATTN_SCALE).astype(BF16)
        q_s[:, NOPE:] = (_rope1(qf_ref[:, NOPE:], tab_ref[...]) * ATTN_SCALE).astype(BF16)
        qb_ref[...] = q_s[...]
        m_s[...] = jnp.full_like(m_s, -jnp.inf)
        l_s[...] = jnp.zeros_like(l_s)
        acc_s[...] = jnp.zeros_like(acc_s)

        def scores(ki, s_s):
            k0 = pl.multiple_of(ki * t, t)
            s_s[...] = lax.dot_general(q_s[...], kf_s[pl.ds(k0, t), :], _DIMS["nt"], preferred_element_type=F32)

        def consume(ki, s_s, masked):
            k0 = pl.multiple_of(ki * t, t)

            def chunk(c, carry):
                r0 = pl.multiple_of(c * ch, ch)
                rows = pl.ds(r0, ch)
                s_c = s_s[rows, :]
                if masked:
                    s_c = jnp.where(_diag_mask(r0, ch, t), s_c, -jnp.inf)
                m_prev = m_s[rows, :]
                m_new = jnp.maximum(m_prev, jnp.max(s_c, axis=1, keepdims=True))
                p = jnp.exp(s_c - _lanes(m_new, t))
                corr = jnp.exp(m_prev - m_new)
                l_s[rows, :] = corr * l_s[rows, :] + jnp.sum(p, axis=1, keepdims=True)
                m_s[rows, :] = m_new
                c_s[rows, :] = corr
                p_s[rows, :] = p.astype(BF16)
                return carry

            lax.fori_loop(0, t // ch, chunk, 0, unroll=True)
            acc_s[...] = c_s[...] * acc_s[...] + jnp.dot(p_s[...], kv_ref[pl.ds(k0, t), pl.ds(NOPE, VDIM)],
                                                         preferred_element_type=F32)

        def pair(j, carry):
            scores(2 * j + 1, sb_s)
            consume(2 * j, sa_s, False)
            scores(2 * j + 2, sa_s)
            consume(2 * j + 1, sb_s, False)
            return carry

        scores(0, sa_s)
        lax.fori_loop(0, qi // 2, pair, 0)

        @pl.when(qi % 2 == 0)
        def _():
            consume(qi, sa_s, True)

        @pl.when(qi % 2 == 1)
        def _():
            scores(qi, sb_s)
            consume(qi - 1, sa_s, False)
            consume(qi, sb_s, True)

        o = acc_s[...] / l_s[...]
        o_ref[...] = o
        g_ref[...] = (o * _silu(z_ref[...])).astype(BF16)
        lse_ref[...] = m_s[...] + jnp.log(l_s[...])

    return _grid_call(
        body, (qf, tab, kv, kr, u1), name=name, grid=(heads, nb),
        in_specs=[pl.BlockSpec((t, QPAD), lambda h, i: (i, h)),
                  pl.BlockSpec((t, LANES), lambda h, i: (i, 0)),
                  pl.BlockSpec((seq, QPAD), lambda h, i: (0, h)),
                  pl.BlockSpec((seq, LANES), lambda h, i: (0, 0)),
                  pl.BlockSpec((t, VDIM), lambda h, i: (i, z_block0 + h))],
        out_specs=[pl.BlockSpec((t, VDIM), lambda h, i: (i, h)),
                   pl.BlockSpec((t, LANES), lambda h, i: (i, h)),
                   pl.BlockSpec((t, QPAD), lambda h, i: (i, h)),
                   pl.BlockSpec((t, VDIM), lambda h, i: (i, h))],
        out_shape=[jax.ShapeDtypeStruct((seq, heads * VDIM), F32), jax.ShapeDtypeStruct((seq, heads * LANES), F32),
                   jax.ShapeDtypeStruct((seq, heads * QPAD), BF16), jax.ShapeDtypeStruct((seq, heads * VDIM), BF16)],
        scratch_shapes=[pltpu.VMEM((seq, QPAD), BF16), pltpu.VMEM((t, QPAD), BF16), pltpu.VMEM((t, t), F32),
                        pltpu.VMEM((t, t), F32), pltpu.VMEM((t, t), BF16), pltpu.VMEM((t, LANES), F32),
                        pltpu.VMEM((t, LANES), F32), pltpu.VMEM((t, LANES), F32), pltpu.VMEM((t, VDIM), F32)],
        sem=("parallel", "arbitrary"), jobs=jobs)


def _attn_bwd(q, tab, kv, kr, do, o, lse, seq, heads, name, jobs=()):
    t = min(ATTN_BLOCK, seq)
    nb = seq // t
    ch = ATTN_CHUNK

    def body(q_ref, tab_ref, kv_ref, kr_ref, do_ref, o_ref, lse_ref, dqb_ref, dkv_ref, dkr_ref,
             dq_ref, kf_s, qt_s, dot_s, dl_s, s_s, dp_s, p_s, ds_s, dkt_s, dvt_s):
        @pl.when(pl.program_id(0) == 0)
        def _():
            dkr_ref[...] = jnp.zeros_like(dkr_ref)

        kf_s[:, 0:NOPE] = kv_ref[:, 0:NOPE]
        kf_s[:, NOPE:] = kr_ref[...]
        dq_ref[...] = jnp.zeros_like(dq_ref)
        for b in range(nb):
            rows = pl.ds(b * t, t)
            dob = do_ref[rows, :].astype(F32)
            qt_s[b] = jnp.transpose(q_ref[rows, :].astype(F32)).astype(BF16)
            dot_s[b] = jnp.transpose(dob).astype(BF16)
            dl_s[rows, :] = jnp.broadcast_to(jnp.sum(dob * o_ref[rows, :], axis=1, keepdims=True), (t, LANES))

        def key_block(ki, carry):
            k0 = pl.multiple_of(ki * t, t)
            kfb = kf_s[pl.ds(k0, t), :]
            dkt_s[...] = jnp.zeros_like(dkt_s)
            dvt_s[...] = jnp.zeros_like(dvt_s)

            def block(qi, masked):
                q0 = pl.multiple_of(qi * t, t)
                s_s[...] = lax.dot_general(q_ref[pl.ds(q0, t), :], kfb, _DIMS["nt"], preferred_element_type=F32)
                dp_s[...] = lax.dot_general(do_ref[pl.ds(q0, t), :], kv_ref[pl.ds(k0, t), pl.ds(NOPE, VDIM)],
                                            _DIMS["nt"], preferred_element_type=F32)

                def chunk(c, carry):
                    r0 = pl.multiple_of(c * ch, ch)
                    rows = pl.ds(r0, ch)
                    grows = pl.ds(pl.multiple_of(q0 + r0, ch), ch)
                    p = jnp.exp(s_s[rows, :] - _lanes(lse_ref[grows, :], t))
                    if masked:
                        p = jnp.where(_diag_mask(r0, ch, t), p, 0.0)
                    p_s[rows, :] = p.astype(BF16)
                    ds_s[rows, :] = (p * (dp_s[rows, :] - _lanes(dl_s[grows, :], t))).astype(BF16)
                    return carry

                lax.fori_loop(0, t // ch, chunk, 0, unroll=True)
                dq_ref[pl.ds(q0, t), :] += jnp.dot(ds_s[...], kfb, preferred_element_type=F32)
                dvt_s[...] += jnp.dot(dot_s[qi], p_s[...], preferred_element_type=F32)
                dkt_s[...] += jnp.dot(qt_s[qi], ds_s[...], preferred_element_type=F32)

            def above(qi, carry):
                block(qi, False)
                return carry

            block(ki, True)
            lax.fori_loop(ki + 1, nb, above, 0)
            dk = jnp.transpose(dkt_s[...])
            dkv_ref[pl.ds(k0, t), :] = jnp.concatenate([dk[:, :NOPE], jnp.transpose(dvt_s[...])], axis=1).astype(BF16)
            dkr_ref[pl.ds(k0, t), :] += dk[:, NOPE:]
            return carry

        lax.fori_loop(0, nb, key_block, 0)
        for b in range(nb):
            rows = pl.ds(b * t, t)
            dqb_ref[rows, 0:NOPE] = (dq_ref[rows, 0:NOPE] * ATTN_SCALE).astype(BF16)
            dqb_ref[rows, NOPE:] = (_rope1_bwd(dq_ref[rows, NOPE:], tab_ref[rows, :]) * ATTN_SCALE).astype(BF16)

    head = lambda w: pl.BlockSpec((seq, w), lambda h: (0, h))
    shared = pl.BlockSpec((seq, LANES), lambda h: (0, 0))
    return _grid_call(
        body, (q, tab, kv, kr, do, o, lse), name=name, grid=(heads,),
        in_specs=[head(QPAD), shared, head(QPAD), shared, head(VDIM), head(VDIM), head(LANES)],
        out_specs=[head(QPAD), head(QPAD), shared],
        out_shape=[jax.ShapeDtypeStruct((seq, heads * QPAD), BF16), jax.ShapeDtypeStruct((seq, heads * QPAD), BF16),
                   jax.ShapeDtypeStruct((seq, LANES), F32)],
        scratch_shapes=[pltpu.VMEM((seq, QPAD), F32),
                        pltpu.VMEM((seq, QPAD), BF16), pltpu.VMEM((nb, QPAD, t), BF16), pltpu.VMEM((nb, VDIM, t), BF16),
                        pltpu.VMEM((seq, LANES), F32), pltpu.VMEM((t, t), F32), pltpu.VMEM((t, t), F32),
                        pltpu.VMEM((t, t), BF16), pltpu.VMEM((t, t), BF16), pltpu.VMEM((QPAD, t), F32),
                        pltpu.VMEM((VDIM, t), F32)],
        sem=("arbitrary",), jobs=jobs)


def _place():
    x, y, c = lax.axis_index("x"), lax.axis_index("y"), lax.axis_index("c")
    return x, y, c


class _Gather:
    def __init__(self, shard):
        r, n = shard.shape
        self.ins = [shard]
        self.outs = [jax.ShapeDtypeStruct((N_DEV, r, n), shard.dtype)]
        self.sems = [pltpu.SemaphoreType.DMA((7,)), pltpu.SemaphoreType.DMA((7,)), pltpu.SemaphoreType.DMA]

    def bind(self, ins, outs, sems):
        (self.x_ref,), (self.out_ref,) = ins, outs
        self.send, self.recv, self.local = sems

    def _plan(self):
        x, y, c = _place()
        chips = [(1 - x, y), (x, 1 - y), (1 - x, 1 - y)]

        def copy(k, block, to, from_shard=False):
            px, py, pc = block
            slot = self.out_ref.at[4 * px + 2 * py + pc]
            return pltpu.make_async_remote_copy(
                src_ref=self.x_ref if from_shard else slot, dst_ref=slot, send_sem=self.send.at[k],
                recv_sem=self.recv.at[k], device_id=to, device_id_type=MESH)

        me, sibling = (x, y, c), (x, y, 1 - c)
        mine = pltpu.make_async_copy(self.x_ref, self.out_ref.at[4 * x + 2 * y + c], self.local)
        first = [copy(0, me, sibling, True)] + [copy(1 + j, me, (*chip, c), True) for j, chip in enumerate(chips)]
        landed = [copy(1 + j, (*chip, c), me) for j, chip in enumerate(chips)]
        passed = [copy(4 + j, (*chip, c), sibling) for j, chip in enumerate(chips)]
        from_sibling = [copy(0, sibling, me)] + [copy(4 + j, (*chip, 1 - c), me) for j, chip in enumerate(chips)]
        return mine, first, landed, passed, from_sibling

    def start(self):
        mine, first, _, _, _ = self._plan()
        mine.start()
        for cp in first:
            cp.start()

    def mid(self):
        _, _, landed, passed, _ = self._plan()
        for got, fwd in zip(landed, passed):
            got.wait_recv()
            fwd.start()

    def finish(self):
        mine, first, _, passed, from_sibling = self._plan()
        for cp in from_sibling:
            cp.wait_recv()
        for cp in first + passed:
            cp.wait_send()
        mine.wait()


class _PairExchange:
    def __init__(self, g8):
        _, r, n = g8.shape
        self.ins = [g8]
        self.outs = [jax.ShapeDtypeStruct((N_CHIP, r, n), g8.dtype)]
        self.sems = [pltpu.SemaphoreType.DMA((N_CHIP,)), pltpu.SemaphoreType.DMA((N_CHIP,))]

    def bind(self, ins, outs, sems):
        (self.g_ref,), (self.out_ref,) = ins, outs
        self.send, self.recv = sems

    def _plan(self):
        x, y, c = _place()
        return [pltpu.make_async_remote_copy(
            src_ref=self.g_ref.at[2 * p + (1 - c)], dst_ref=self.out_ref.at[p], send_sem=self.send.at[p],
            recv_sem=self.recv.at[p], device_id=(x, y, 1 - c), device_id_type=MESH) for p in range(N_CHIP)]

    def start(self):
        for cp in self._plan():
            cp.start()

    def mid(self):
        pass

    def finish(self):
        copies = self._plan()
        for cp in copies:
            cp.wait_recv()
        for cp in copies:
            cp.wait_send()


class _ChipExchange:
    def __init__(self, t4):
        _, r, n = t4.shape
        self.ins = [t4]
        self.outs = [jax.ShapeDtypeStruct((N_CHIP, r, n), t4.dtype)]
        self.sems = [pltpu.SemaphoreType.DMA((3,)), pltpu.SemaphoreType.DMA((3,)), pltpu.SemaphoreType.DMA]

    def bind(self, ins, outs, sems):
        (self.t_ref,), (self.out_ref,) = ins, outs
        self.send, self.recv, self.local_sem = sems

    def _plan(self):
        x, y, c = _place()
        mine = 2 * x + y
        chips = [(1 - x, y), (x, 1 - y), (1 - x, 1 - y)]
        local = pltpu.make_async_copy(self.t_ref.at[mine], self.out_ref.at[mine], self.local_sem)
        sends = [pltpu.make_async_remote_copy(
            src_ref=self.t_ref.at[2 * px + py], dst_ref=self.out_ref.at[mine], send_sem=self.send.at[k],
            recv_sem=self.recv.at[k], device_id=(px, py, c), device_id_type=MESH) for k, (px, py) in enumerate(chips)]
        arrivals = [pltpu.make_async_remote_copy(
            src_ref=self.t_ref.at[mine], dst_ref=self.out_ref.at[2 * px + py], send_sem=self.send.at[k],
            recv_sem=self.recv.at[k], device_id=(px, py, c), device_id_type=MESH) for k, (px, py) in enumerate(chips)]
        return local, sends, arrivals

    def start(self):
        local, sends, _ = self._plan()
        local.start()
        for cp in sends:
            cp.start()

    def mid(self):
        pass

    def finish(self):
        local, sends, arrivals = self._plan()
        for cp in arrivals:
            cp.wait_recv()
        for cp in sends:
            cp.wait_send()
        local.wait()


def _job_specs(jobs):
    ins = [a for j in jobs for a in j.ins]
    outs = [o for j in jobs for o in j.outs]
    sems = [s for j in jobs for s in j.sems]
    return ins, outs, sems


def _bind_jobs(jobs, ins, outs, sems):
    for j in jobs:
        ni, no, ns = len(j.ins), len(j.outs), len(j.sems)
        j.bind(ins[:ni], outs[:no], sems[:ns])
        ins, outs, sems = ins[ni:], outs[no:], sems[ns:]


def _carry(body, jobs, n_in, n_out, grid):
    if not jobs:
        return body
    j_in, j_out, j_sem = (len(v) for v in _job_specs(jobs))
    total = math.prod(grid)
    mid_step = (3 * total) // 4

    def wrapped(*refs):
        ins, rest = refs[:n_in], refs[n_in:]
        jin, rest = rest[:j_in], rest[j_in:]
        outs, rest = rest[:n_out], rest[n_out:]
        jout, rest = rest[:j_out], rest[j_out:]
        scratch, jsem = rest[:len(rest) - j_sem], rest[len(rest) - j_sem:]
        _bind_jobs(jobs, jin, jout, jsem)
        lin = 0
        for d, size in enumerate(grid):
            lin = lin * size + pl.program_id(d)

        @pl.when(lin == 0)
        def _():
            for j in jobs:
                j.start()

        body(*ins, *outs, *scratch)

        @pl.when(lin == mid_step)
        def _():
            for j in jobs:
                j.mid()

        @pl.when(lin == total - 1)
        def _():
            for j in jobs:
                j.finish()

    return wrapped


def _carry_call(jobs):
    ins, outs, sems = _job_specs(jobs)
    return ins, [ANY] * len(ins), [ANY] * len(outs), outs, sems


def _grid_call(body, operands, *, name, grid, in_specs, out_specs, out_shape, scratch_shapes, sem, jobs=()):
    j_ins, j_in_specs, j_out_specs, j_outs, j_sems = _carry_call(jobs)
    return pl.pallas_call(
        _carry(body, jobs, len(in_specs), len(out_specs), grid), name=name, grid=grid,
        in_specs=list(in_specs) + j_in_specs, out_specs=list(out_specs) + j_out_specs,
        out_shape=list(out_shape) + j_outs, scratch_shapes=list(scratch_shapes) + j_sems,
        compiler_params=_params(("arbitrary",) * len(grid) if jobs else sem, has_side_effects=bool(jobs)),
    )(*operands, *j_ins)


def _comm(jobs, name):
    ins, outs, sems = _job_specs(jobs)

    def body(*refs):
        _bind_jobs(jobs, refs[:len(ins)], refs[len(ins):len(ins) + len(outs)], refs[len(ins) + len(outs):])
        for j in jobs:
            j.start()
        for j in jobs:
            j.mid()
        for j in jobs:
            j.finish()

    return pl.pallas_call(
        body, name=name, in_specs=[ANY] * len(ins), out_specs=[ANY] * len(outs), out_shape=outs,
        scratch_shapes=sems, compiler_params=pltpu.CompilerParams(has_side_effects=True),
    )(*ins)


def _all_gather(shard, name):
    return _comm([_Gather(shard)], name)[0]


def _pair_add(g8, got, core, name):
    _, r, n = g8.shape
    tr = _pick(r, (256, 128, 64, 32, 16, 8))

    def body(c_ref, g_ref, r_ref, o_ref):
        o_ref[...] = (g_ref[...].astype(F32) + r_ref[...].astype(F32)).astype(o_ref.dtype)

    grid_spec = pltpu.PrefetchScalarGridSpec(
        num_scalar_prefetch=1, grid=(N_CHIP, r // tr),
        in_specs=[pl.BlockSpec((None, tr, n), lambda p, i, c: (2 * p + c[0], i, 0)),
                  pl.BlockSpec((None, tr, n), lambda p, i, c: (p, i, 0))],
        out_specs=pl.BlockSpec((None, tr, n), lambda p, i, c: (p, i, 0)))
    return pl.pallas_call(
        body, name=name, grid_spec=grid_spec, out_shape=jax.ShapeDtypeStruct((N_CHIP, r, n), g8.dtype),
        compiler_params=_params(("parallel", "parallel")),
    )(core, g8, got)


def _slot_sum(g, name):
    k, r, n = g.shape

    def body(g_ref, o_ref):
        acc = g_ref[0].astype(F32)
        for s in range(1, k):
            acc = acc + g_ref[s].astype(F32)
        o_ref[...] = acc

    return pl.pallas_call(body, name=name, out_shape=jax.ShapeDtypeStruct((r, n), F32),
                          compiler_params=pltpu.CompilerParams(vmem_limit_bytes=VMEM_LIMIT))(g)


def _pack_rows(parts, n_rows, width, name):
    starts = [r0 for _, r0 in parts]

    def body(*refs):
        o_ref = refs[-1]
        o_ref[...] = jnp.zeros_like(o_ref)
        for ref, r0 in zip(refs[:-1], starts):
            k, w = ref.shape
            o_ref[r0:r0 + k, 0:w] = ref[...]

    return pl.pallas_call(body, name=name, out_shape=jax.ShapeDtypeStruct((n_rows, width), F32),
                          compiler_params=pltpu.CompilerParams(vmem_limit_bytes=VMEM_LIMIT))(*[a for a, _ in parts])


def _adamw(w, gparts, m, v, name):
    r, n = w.shape
    k = gparts.shape[0]
    tr = _pick(r, [t for t in (512, 256, 128, 64, 32, 16) if t * n <= ADAMW_TILE_ELEMS])
    c1 = 1.0 - ADAM_B1 ** ADAM_STEP
    c2 = 1.0 - ADAM_B2 ** ADAM_STEP

    def body(w_ref, g_ref, m_ref, v_ref, go_ref, d_ref, mo_ref, vo_ref):
        g = g_ref[0].astype(F32)
        for s in range(1, k):
            g = g + g_ref[s].astype(F32)
        mn = ADAM_B1 * m_ref[...] + (1.0 - ADAM_B1) * g
        vn = ADAM_B2 * v_ref[...] + (1.0 - ADAM_B2) * (g * g)
        go_ref[...] = g
        mo_ref[...] = mn
        vo_ref[...] = vn
        d_ref[...] = -ADAM_LR * ((mn / c1) / (jnp.sqrt(vn / c2) + ADAM_EPS) + ADAM_WD * w_ref[...])

    spec = pl.BlockSpec((tr, n), lambda i: (i, 0))
    return pl.pallas_call(
        body, name=name, grid=(r // tr,),
        in_specs=[spec, pl.BlockSpec((k, tr, n), lambda i: (0, i, 0)), spec, spec],
        out_specs=[spec] * 4, out_shape=[jax.ShapeDtypeStruct((r, n), F32)] * 4,
        compiler_params=_params(("parallel",)),
    )(w, gparts, m, v)


def kernel(x, c, w_ada, b_ada, ln_g, ln_b, a_w_in, a_w_dw, a_b_dw, a_norm_g, a_norm_b, a_w_out, b_w_in, b_q_norm_g, b_w_qb, b_w_out, kv_w_a, kv_norm_g, kv_w_b, loss_target, m_w_ada, m_b_ada, m_ln_g, m_ln_b, m_a_w_in, m_a_w_dw, m_a_b_dw, m_a_norm_g, m_a_norm_b, m_a_w_out, m_b_w_in, m_b_q_norm_g, m_b_w_qb, m_b_w_out, m_kv_w_a, m_kv_norm_g, m_kv_w_b, v_w_ada, v_b_ada, v_ln_g, v_ln_b, v_a_w_in, v_a_w_dw, v_a_b_dw, v_a_norm_g, v_a_norm_b, v_a_w_out, v_b_w_in, v_b_q_norm_g, v_b_w_qb, v_b_w_out, v_kv_w_a, v_kv_norm_g, v_kv_w_b):
    S, D = x.shape[1], x.shape[2]
    C = a_w_out.shape[1] * N_DEV
    QL = b_q_norm_g.shape[1]
    KL = kv_norm_g.shape[0]
    H = kv_w_b.shape[1] * N_DEV // (NOPE + VDIM)
    HL = H // N_DEV
    W = H * VDIM
    assert a_w_dw.shape[1] == CONV_KERNEL and kv_w_a.shape[1] == KL + ROPE and b_w_qb.shape[2] == HL * (NOPE + ROPE)
    assert b_w_in.shape[2] * N_DEV == QL + W and D % (N_DEV * LANES) == 0 and S % CONV_ROWS == 0
    TW = 1024 if W % 1024 == 0 and QL % 1024 == 0 else LANES

    xi, yi, ci = _place()
    idx = 4 * xi + 2 * yi + ci
    core = jnp.reshape(ci, (1,)).astype(jnp.int32)
    x2, tgt = x[0], loss_target[0]
    cos_t, sin_a, sin_b = _rope_tables(S)
    rope_rows = [(cos_t, LANES, 0, False), (sin_a, LANES, 0, False), (sin_b, LANES, 0, False)]

    win0 = _all_gather(a_w_in[0].astype(BF16), "ag_a_w_in")
    sh_wout0 = a_w_out[0].astype(BF16)
    sh_wbin = b_w_in[0].astype(BF16)
    wqb_loc = jnp.pad(b_w_qb[0].reshape(QL, HL, NOPE + ROPE), ((0, 0), (0, 0), (0, QPAD - NOPE - ROPE)))
    sh_wqb = wqb_loc.reshape(QL, HL * QPAD).astype(BF16)
    sh_wbout = b_w_out[0].astype(BF16)
    kva_w = KL + LANES
    sh_wkva = jnp.pad(kv_w_a, ((0, 0), (0, kva_w - KL - ROPE))).astype(BF16)
    sh_wkvb = kv_w_b.astype(BF16)

    cl = C // N_DEV
    v0 = CONV_TOP
    small_loc = _pack_rows([(a_w_dw[0], 0), (a_b_dw, v0), (a_norm_g, v0 + 1), (a_norm_b, v0 + 2)], 40, cl, "pack_conv_small")
    small = _all_gather(small_loc, "ag_conv_small")
    small = jnp.transpose(small, (1, 0, 2)).reshape(40, C)
    w_dw, b_dw, g_cn, b_cn = small[:CONV_KERNEL], small[v0:v0 + 1], small[v0 + 1:v0 + 2], small[v0 + 2:v0 + 3]

    c_all = _all_gather(jnp.broadcast_to(c, (SUBLANES, D)), "ag_c")[:, 0, :]
    c_pad = jnp.concatenate([c_all, jnp.zeros((16 - N_DEV, D), F32)], axis=0)
    (sc,), _ = _rowwise(lambda r, v: ([_silu(r[0])], []), [(c_pad, D, 0, False)], [], [(BF16, D, D)], [],
                        n_rows=16, tr=16, name="silu_c")
    nl = 3 * D // N_DEV
    zsc = jnp.zeros_like(sc)
    sc2 = jnp.concatenate([jnp.concatenate([sc, zsc], axis=1), jnp.concatenate([zsc, sc], axis=1)], axis=0)
    mod_loc = _mm(sc2, w_ada.reshape(DEPTH * D, nl), "nn", F32, "mod")
    mod_all = jnp.transpose(_all_gather(mod_loc, "ag_mod"), (1, 0, 2)).reshape(DEPTH, 16, 3 * D)
    mods = []
    for l in range(DEPTH):
        row = lax.dynamic_slice(mod_all[l], (idx, 0), (1, 3 * D)) + b_ada[l][None, :]
        mods.append((row[:, :D], row[:, D:2 * D], row[:, 2 * D:]))
    (shift0, scale0, gate0), (shift1, scale1, gate1) = mods
    vec = lambda a, w=None: (a, a.shape[1] if w is None else w, 0, False)

    def f_mod0(r, v):
        xh, _ = _ln_stats(r[0])
        return [xh * (1.0 + v[0]) + v[1]], []
    (h0,), _ = _rowwise(f_mod0, [(x2, D, 0, False)], [vec(scale0), vec(shift0)], [(BF16, D, D)], [],
                        n_rows=S, tr=128, name="l0_modulate")
    u0, wbin = _mm(h0, win0, "nn", F32, "l0_in", b_slots=True, jobs=[_Gather(sh_wbin)])
    v2, wout0 = _conv_fwd(u0, w_dw, b_dw, S, C, "l0_conv", jobs=[_Gather(sh_wout0)])
    wout0 = wout0.reshape(C, D)

    def f_gate0(r, v):
        xh, _ = _ln_stats(r[0])
        return [_silu(xh * v[0] + v[1]) * _silu(r[1])], []
    (p0,), _ = _rowwise(f_gate0, [(v2, C, 0, False), (u0, C, 2, False)], [vec(g_cn), vec(b_cn)], [(BF16, C, C)], [],
                        n_rows=S, tr=128, name="l0_gate")
    o0, wkva, wkvb = _mm(p0, wout0, "nn", F32, "l0_out", jobs=[_Gather(sh_wkva), _Gather(sh_wkvb)])
    wkva = wkva.reshape(D, kva_w)

    def f_res0(r, v):
        gate, g0, b0, sc1, sh1 = v
        xh, _ = _ln_stats(ALPHA * r[0] + (1.0 + gate) * r[1])
        x1 = xh * g0 + b0
        xh1, _ = _ln_stats(x1)
        return [x1, x1, xh1 * (1.0 + sc1) + sh1], []
    (x1, x1b, h1), _ = _rowwise(f_res0, [(x2, D, 0, False), (o0, D, 0, False)],
                                [vec(gate0), vec(ln_g[0:1]), vec(ln_b[0:1]), vec(scale1), vec(shift1)],
                                [(F32, D, D), (BF16, D, D), (BF16, D, D)], [], n_rows=S, tr=128, name="l0_residual")

    kva = _mm(x1b, wkva, "nn", F32, "kv_a")
    g_kv = kv_norm_g[None, :]

    def f_kvn(r, v):
        xh, _ = _rms_stats(r[0][:, :KL])
        return [xh * v[0], _rope_fwd(r[0][:, KL:], r[1], r[2], r[3])], []
    (ckv, krp), _ = _rowwise(f_kvn, [(kva, kva_w, 0, False)] + rope_rows, [vec(g_kv)],
                             [(BF16, KL, KL), (BF16, LANES, LANES)], [], n_rows=S, tr=256, name="kv_norm_rope")
    kvh = _mm(ckv, wkvb, "nn", BF16, "kv_b", b_slots=True)

    u1, wqb = _mm(h1, wbin, "nn", F32, "l1_in", b_slots=True, jobs=[_Gather(sh_wqb)])
    g_q = b_q_norm_g[0:1]

    def f_qn(r, v):
        xh, _ = _rms_stats(r[0])
        return [xh * v[0]], []
    (cqn,), _ = _rowwise(f_qn, [(u1, QL, 0, False)], [vec(g_q)], [(BF16, QL, QL)], [], n_rows=S, tr=256, name="q_norm")
    qf = _mm(cqn, wqb, "nn", F32, "q_b", b_slots=True)

    rope_tab = jnp.concatenate([cos_t[:, :ROPE], sin_a[:, :ROPE // 2], sin_b[:, ROPE // 2:ROPE]], axis=1)
    oa, lse, qb, p1, wbout = _attn_fwd(qf, rope_tab, kvh, krp, u1, QL // VDIM, S, H, "attn_fwd", jobs=[_Gather(sh_wbout)])
    wbout = wbout.reshape(W, D)
    zc0 = QL // TW
    o1 = _mm(p1, wbout, "nn", F32, "l1_out")

    def f_head(r, v):
        gate, g1, b1 = v
        xh, rstd = _ln_stats(ALPHA * r[0] + (1.0 + gate) * r[1])
        err = xh * g1 + b1 - r[2]
        dy = err * (1.0 / D)
        dr = _ln_bwd(dy * g1, xh, rstd)
        return [dr, (1.0 + gate) * dr], [_colsum(err * err), _colsum(dy * xh), _colsum(dy), _colsum(dr * r[1])]
    (dr1, do1b), (loss_row, dlng1, dlnb1, dgate1) = _rowwise(
        f_head, [(x1, D, 0, False), (o1, D, 0, False), (tgt, D, 0, False)], [vec(gate1), vec(ln_g[1:2]), vec(ln_b[1:2])],
        [(F32, D, D), (BF16, D, D)], [D, D, D, D], n_rows=S, tr=64, name="head_loss")
    loss = lax.psum(0.5 / D * jnp.sum(loss_row), ("x", "y", "c"))

    dp1 = _mm(do1b, wbout, "nt", F32, "l1_out_dx")
    g_wbout = _mm(p1, do1b, "tn", BF16, "l1_out_dw").reshape(N_DEV, W // N_DEV, D)

    def f_dgate1(r, v):
        dp, o, z = r
        return [dp * _silu(z), dp * o * _dsilu(z)], []
    (doa, dz1), _, (got,) = _rowwise(f_dgate1, [(dp1, TW, 0, True), (oa, TW, 0, True), (u1, TW, zc0, True)], [],
                                     [(BF16, W, TW), (BF16, W, TW)], [], n_rows=S, tr=512 if S % 512 == 0 else S,
                                     name="l1_gate_bwd", nc=W // TW, jobs=[_PairExchange(g_wbout)])
    t_wbout = _pair_add(g_wbout, got, core, "rs_b_w_out_add")
    dqb, dkvh, dkrp, r_wbout = _attn_bwd(qb, rope_tab, kvh, krp, doa, oa, lse, S, H, "attn_bwd",
                                         jobs=[_ChipExchange(t_wbout)])
    g_wqb = _mm(cqn, dqb, "tn", BF16, "q_b_dw", out_slots=True)
    dcqn, got = _mm(dqb, wqb, "nt", F32, "q_b_dx", b_slots=True, jobs=[_PairExchange(g_wqb)])
    t_wqb = _pair_add(g_wqb, got, core, "rs_b_w_qb_add")

    def f_qn_bwd(r, v):
        xh, rr = _rms_stats(r[0])
        return [_rms_bwd(r[1] * v[0], xh, rr)], [_colsum(r[1] * xh)]
    (dcq,), (dg_q,) = _rowwise(f_qn_bwd, [(u1, QL, 0, False), (dcqn, QL, 0, False)], [vec(g_q)], [(BF16, QL, QL)], [QL],
                               n_rows=S, tr=256, name="q_norm_bwd")
    du1 = jnp.concatenate([dcq, dz1], axis=1)
    g_wbin, r_wqb = _mm(h1, du1, "tn", BF16, "l1_in_dw", out_slots=True, jobs=[_ChipExchange(t_wqb)])
    dh1, got = _mm(du1, wbin, "nt", F32, "l1_in_dx", b_slots=True, jobs=[_PairExchange(g_wbin)])
    t_wbin = _pair_add(g_wbin, got, core, "rs_b_w_in_add")

    g_wkvb = _mm(ckv, dkvh, "tn", BF16, "kv_b_dw", out_slots=True)
    dckv, got = _mm(dkvh, wkvb, "nt", F32, "kv_b_dx", b_slots=True, jobs=[_PairExchange(g_wkvb)])
    t_wkvb = _pair_add(g_wkvb, got, core, "rs_kv_w_b_add")

    def f_kvn_bwd(r, v):
        xh, rr = _rms_stats(r[0][:, :KL])
        dck = _rms_bwd(r[1] * v[0], xh, rr)
        dkr = _rope_bwd(r[2], r[3], r[4], r[5])
        return [jnp.concatenate([dck, dkr], axis=1)], [_colsum(r[1] * xh)]
    (dkva,), (dg_kv,) = _rowwise(f_kvn_bwd, [(kva, kva_w, 0, False), (dckv, KL, 0, False), (dkrp, LANES, 0, False)] + rope_rows,
                                 [vec(g_kv)], [(BF16, kva_w, kva_w)], [KL], n_rows=S, tr=256, name="kv_norm_rope_bwd")
    g_wkva = _mm(x1b, dkva, "tn", BF16, "kv_a_dw").reshape(N_DEV, D // N_DEV, kva_w)
    dx1_kv, got = _mm(dkva, wkva, "nt", F32, "kv_a_dx", jobs=[_PairExchange(g_wkva)])
    t_wkva = _pair_add(g_wkva, got, core, "rs_kv_w_a_add")

    def f_res0_bwd(r, v):
        dr1_, dxkv, dh, x1_, x_, o_ = r
        sc1, gate, g0 = v
        xh1, rstd1 = _ln_stats(x1_)
        dx1 = ALPHA * dr1_ + dxkv + _ln_bwd(dh * (1.0 + sc1), xh1, rstd1)
        xh0, rstd0 = _ln_stats(ALPHA * x_ + (1.0 + gate) * o_)
        dr0 = _ln_bwd(dx1 * g0, xh0, rstd0)
        return ([dr0, (1.0 + gate) * dr0],
                [_colsum(dh * xh1), _colsum(dh), _colsum(dx1 * xh0), _colsum(dx1), _colsum(dr0 * o_)])
    (dr0, do0b), (dscale1, dshift1, dlng0, dlnb0, dgate0) = _rowwise(
        f_res0_bwd, [(a, D, 0, False) for a in (dr1, dx1_kv, dh1, x1, x2, o0)], [vec(scale1), vec(gate0), vec(ln_g[0:1])],
        [(F32, D, D), (BF16, D, D)], [D] * 5, n_rows=S, tr=64, name="l0_residual_bwd")

    dp0, r_wkvb, r_wkva = _mm(do0b, wout0, "nt", F32, "l0_out_dx",
                              jobs=[_ChipExchange(t_wkvb), _ChipExchange(t_wkva)])
    g_wout0 = _mm(p0, do0b, "tn", BF16, "l0_out_dw").reshape(N_DEV, C // N_DEV, D)

    def f_gate0_bwd(r, v):
        dp, v2_, z = r
        g, b = v
        xh, rstd = _ln_stats(v2_)
        v3 = xh * g + b
        dv4 = dp * _silu(z)
        dz = dp * _silu(v3) * _dsilu(z)
        dv3 = dv4 * _dsilu(v3)
        dv2_ = _ln_bwd(dv3 * g, xh, rstd)
        return [dz, dv2_], [_colsum(dv3 * xh), _colsum(dv3), _colsum(dv2_)]
    (dz0, dv2), (dg_cn, db_cn, db_dw) = _rowwise(
        f_gate0_bwd, [(dp0, C, 0, False), (v2, C, 0, False), (u0, C, 2, False)], [vec(g_cn), vec(b_cn)],
        [(BF16, C, C), (F32, C, C)], [C, C, C], n_rows=S, tr=64, name="l0_gate_bwd")
    da0, dg0, dw_dw, r_wbin, got = _conv_bwd(u0, dv2, w_dw, S, C, "l0_conv_bwd",
                                             jobs=[_ChipExchange(t_wbin), _PairExchange(g_wout0)])
    t_wout0 = _pair_add(g_wout0, got, core, "rs_a_w_out_add")
    du0 = jnp.concatenate([da0, dg0, dz0], axis=1)
    g_win0_a, r_wout0 = _mm(h0, du0, "tn", BF16, "l0_in_dw_a", out_slots=True, m_rows=(0, D // 2),
                            jobs=[_ChipExchange(t_wout0)])
    g_win0_b, got = _mm(h0, du0, "tn", BF16, "l0_in_dw_b", out_slots=True, m_rows=(D // 2, D // 2),
                        jobs=[_PairExchange(g_win0_a)])
    t_win0_a = _pair_add(g_win0_a, got, core, "rs_a_w_in_add_a")
    dh0, r_win0_a, got = _mm(du0, win0, "nt", F32, "l0_in_dx", b_slots=True,
                             jobs=[_ChipExchange(t_win0_a), _PairExchange(g_win0_b)])
    t_win0_b = _pair_add(g_win0_b, got, core, "rs_a_w_in_add_b")

    def f_mod0_bwd(r, v):
        xh, rstd = _ln_stats(r[2])
        return [ALPHA * r[0] + _ln_bwd(r[1] * (1.0 + v[0]), xh, rstd)], [_colsum(r[1] * xh), _colsum(r[1])]
    (grad_x,), (dscale0, dshift0), (r_win0_b,) = _rowwise(
        f_mod0_bwd, [(dr0, D, 0, False), (dh0, D, 0, False), (x2, D, 0, False)], [vec(scale0)], [(F32, D, D)], [D, D],
        n_rows=S, tr=128, name="l0_modulate_bwd", jobs=[_ChipExchange(t_win0_b)])
    r_win0 = jnp.concatenate([r_win0_a, r_win0_b], axis=1)

    assert C == D and QL <= D and KL <= D
    n_small = 48
    singles = [dlng0, dlng1, dlnb0, dlnb1, dshift0, dscale0, dgate0, dshift1, dscale1, dgate1, db_dw, dg_cn, db_cn, dg_q, dg_kv]
    rows_loc = _pack_rows([(dw_dw, 0)] + [(a, v0 + i) for i, a in enumerate(singles)], n_small, D, "pack_small_grads")
    parts = _all_gather(rows_loc, "ag_small_grads")
    tot = _slot_sum(parts, "small_grads_sum")
    g_ln_g, g_ln_b = tot[v0:v0 + 2], tot[v0 + 2:v0 + 4]
    g_b_ada = tot[v0 + 4:v0 + 10].reshape(DEPTH, 3 * D)
    loc_cols = lambda a: lax.dynamic_slice(a, (0, idx * cl), (a.shape[0], cl))
    g_a_w_dw = loc_cols(tot[0:CONV_KERNEL])
    g_a_b_dw = loc_cols(tot[v0 + 10:v0 + 11])
    g_a_norm_g = loc_cols(tot[v0 + 11:v0 + 12])
    g_a_norm_b = loc_cols(tot[v0 + 12:v0 + 13])
    g_q_norm = tot[v0 + 13:v0 + 14, :QL]
    g_kv_norm = tot[v0 + 14:v0 + 15, :KL]

    dmods = []
    for l in range(DEPTH):
        dmod = parts[:, v0 + 4 + 3 * l:v0 + 7 + 3 * l, :].reshape(N_DEV, 3 * D)
        dmods += [lax.dynamic_slice(dmod, (0, idx * nl), (N_DEV, nl)), jnp.zeros((16 - N_DEV, nl), F32)]
    g_w_ada = _mm(sc2, jnp.concatenate(dmods, axis=0), "tn", F32, "mod_dw").reshape(DEPTH, D, nl)

    r_wqb = r_wqb.reshape(N_CHIP, QL, HL, QPAD)[..., :NOPE + ROPE].reshape(N_CHIP, QL, HL * (NOPE + ROPE))
    r_wkva = r_wkva[..., :KL + ROPE]

    def upd(name, w, gparts, m, v):
        shp = w.shape
        r2 = (1, shp[0]) if len(shp) == 1 else (math.prod(shp[:-1]), shp[-1])
        gp = gparts.reshape((gparts.shape[0],) + r2)
        res = _adamw(w.reshape(r2), gp, m.reshape(r2), v.reshape(r2), "adamw_" + name)
        return [a.reshape(shp) for a in res]

    one = lambda g: g[None]
    results = [
        upd("w_ada", w_ada, one(g_w_ada), m_w_ada, v_w_ada),
        upd("b_ada", b_ada, one(g_b_ada), m_b_ada, v_b_ada),
        upd("ln_g", ln_g, one(g_ln_g), m_ln_g, v_ln_g),
        upd("ln_b", ln_b, one(g_ln_b), m_ln_b, v_ln_b),
        upd("a_w_in", a_w_in, r_win0[:, None], m_a_w_in, v_a_w_in),
        upd("a_w_dw", a_w_dw, one(g_a_w_dw[None]), m_a_w_dw, v_a_w_dw),
        upd("a_b_dw", a_b_dw, one(g_a_b_dw), m_a_b_dw, v_a_b_dw),
        upd("a_norm_g", a_norm_g, one(g_a_norm_g), m_a_norm_g, v_a_norm_g),
        upd("a_norm_b", a_norm_b, one(g_a_norm_b), m_a_norm_b, v_a_norm_b),
        upd("a_w_out", a_w_out, r_wout0[:, None], m_a_w_out, v_a_w_out),
        upd("b_w_in", b_w_in, r_wbin[:, None], m_b_w_in, v_b_w_in),
        upd("b_q_norm_g", b_q_norm_g, one(g_q_norm), m_b_q_norm_g, v_b_q_norm_g),
        upd("b_w_qb", b_w_qb, r_wqb[:, None], m_b_w_qb, v_b_w_qb),
        upd("b_w_out", b_w_out, r_wbout[:, None], m_b_w_out, v_b_w_out),
        upd("kv_w_a", kv_w_a, r_wkva, m_kv_w_a, v_kv_w_a),
        upd("kv_norm_g", kv_norm_g, one(g_kv_norm[0]), m_kv_norm_g, v_kv_norm_g),
        upd("kv_w_b", kv_w_b, r_wkvb, m_kv_w_b, v_kv_w_b),
    ]
    grads, deltas, new_m, new_v = zip(*results)
    return (loss, grad_x[None], *grads, *deltas, *new_m, *new_v)
```

```python
import functools
import math

import jax
import jax.numpy as jnp
from jax import lax
from jax.experimental import pallas as pl
from jax.experimental.pallas import tpu as pltpu

F32 = jnp.float32
BF16 = jnp.bfloat16
MESH = pl.DeviceIdType.MESH
ANY = pl.BlockSpec(memory_space=pl.ANY)

N_DEV = 8
N_CHIP = 4
V7X_VMEM_BYTES = 64 * 1024 * 1024
VMEM_LIMIT = V7X_VMEM_BYTES - 8 * 1024 * 1024
LANES = 128
SUBLANES = 8

DEPTH = 2
CONV_KERNEL = 31
NOPE = 128
ROPE = 64
VDIM = 128
QPAD = NOPE + LANES
ROPE_BASE = 10000.0
LN_EPS = 1e-5
RMS_EPS = 1e-6
ALPHA = (2.0 * DEPTH) ** 0.25
ATTN_SCALE = (NOPE + ROPE) ** -0.5
ADAM_LR = 0.001
ADAM_B1 = 0.9
ADAM_B2 = 0.999
ADAM_EPS = 1e-08
ADAM_WD = 0.01
ADAM_STEP = 10

CONV_TOP = 32
CONV_ROWS = 64
ATTN_BLOCK = 512
ATTN_CHUNK = 32
MM_INPUT_BYTES = 32 * 1024 * 1024
ADAMW_TILE_ELEMS = 256 * 1024


def _pick(dim, prefs):
    for p in prefs:
        if dim % p == 0:
            return p
    return dim


def _params(sem, **kw):
    return pltpu.CompilerParams(dimension_semantics=sem, vmem_limit_bytes=VMEM_LIMIT, **kw)


_DIMS = {"nn": (((1,), (0,)), ((), ())), "nt": (((1,), (1,)), ((), ())), "tn": (((0,), (0,)), ((), ()))}


def _mm(a, b, mode, out_dtype, name, *, b_slots=False, out_slots=False, m_rows=None, jobs=()):
    m_first = 0
    if mode == "tn":
        K, M = a.shape
        if m_rows is not None:
            m_first, M = m_rows
    else:
        M, K = a.shape
    if b_slots:
        assert mode in ("nn", "nt")
        if mode == "nn":
            _, kb, n_loc = b.shape
            N, slot_w = N_DEV * n_loc, n_loc
        else:
            _, N, k_loc = b.shape
            kb, slot_w = N_DEV * k_loc, k_loc
    else:
        if mode == "nt":
            N, kb = b.shape
        else:
            kb, N = b.shape
        slot_w = N // N_DEV if out_slots else None
    assert kb == K, (name, a.shape, b.shape)
    tm = _pick(M, (1024, 512, 256, 128, 64, 32, 16))
    n_unit = slot_w if (mode == "nn" and b_slots) or out_slots else N
    k_unit = slot_w if (mode == "nt" and b_slots) else K
    tn = _pick(n_unit, (1024, 1152, 768, 640, 512, 384, 256, 128))
    step_bytes = 2 * (tm * a.dtype.itemsize + tn * b.dtype.itemsize)
    tk = _pick(k_unit, [t for t in (4096, 2048, 1536, 1152, 1024, 768, 640, 512, 384, 256, 128, 64, 32, 16)
                        if t * step_bytes <= MM_INPUT_BYTES])
    nk = K // tk

    if mode == "tn":
        assert m_first % tm == 0
        a_spec = pl.BlockSpec((tk, tm), lambda i, j, k: (k, i + m_first // tm))
    else:
        a_spec = pl.BlockSpec((tm, tk), lambda i, j, k: (i, k))
    if b_slots and mode == "nn":
        per = slot_w // tn
        b_spec = pl.BlockSpec((None, tk, tn), lambda i, j, k: (j // per, k, j % per))
    elif b_slots and mode == "nt":
        per = slot_w // tk
        b_spec = pl.BlockSpec((None, tn, tk), lambda i, j, k: (k // per, j, k % per))
    elif mode == "nt":
        b_spec = pl.BlockSpec((tn, tk), lambda i, j, k: (j, k))
    else:
        b_spec = pl.BlockSpec((tk, tn), lambda i, j, k: (k, j))
    if out_slots:
        per_o = slot_w // tn
        o_spec = pl.BlockSpec((None, tm, tn), lambda i, j, k: (j // per_o, i, j % per_o))
        o_shape = jax.ShapeDtypeStruct((N_DEV, M, slot_w), out_dtype)
    else:
        o_spec = pl.BlockSpec((tm, tn), lambda i, j, k: (i, j))
        o_shape = jax.ShapeDtypeStruct((M, N), out_dtype)
    dims = _DIMS[mode]

    def step(a_ref, b_ref):
        return lax.dot_general(a_ref[...].astype(BF16), b_ref[...].astype(BF16), dims, preferred_element_type=F32)

    if nk == 1:
        def body(a_ref, b_ref, o_ref):
            o_ref[...] = step(a_ref, b_ref).astype(out_dtype)
    else:
        def body(a_ref, b_ref, o_ref, acc_ref):
            k = pl.program_id(2)

            @pl.when(k == 0)
            def _():
                acc_ref[...] = jnp.zeros_like(acc_ref)

            acc_ref[...] += step(a_ref, b_ref)

            @pl.when(k == nk - 1)
            def _():
                o_ref[...] = acc_ref[...].astype(out_dtype)

    res = _grid_call(body, (a, b), name=name, grid=(M // tm, N // tn, nk), in_specs=[a_spec, b_spec],
                     out_specs=[o_spec], out_shape=[o_shape],
                     scratch_shapes=[] if nk == 1 else [pltpu.VMEM((tm, tn), F32)],
                     sem=("parallel", "parallel", "arbitrary"), jobs=jobs)
    return tuple(res) if jobs else res[0]


def _rowwise(fn, rows, vecs, outs, accs, *, n_rows, tr, name, nc=1, jobs=()):
    nr = n_rows // tr
    n_in, n_vec, n_out, n_acc = len(rows), len(vecs), len(outs), len(accs)

    def cmap(cb0, follow):
        if follow:
            return lambda j, i: (i, cb0 + j)
        return lambda j, i: (i, cb0)

    def vmap_(cb0, follow):
        if follow:
            return lambda j, i: (0, cb0 + j)
        return lambda j, i: (0, cb0)

    in_specs = [pl.BlockSpec((tr, w), cmap(cb0, fol)) for (_, w, cb0, fol) in rows]
    in_specs += [pl.BlockSpec((1, w), vmap_(cb0, fol)) for (_, w, cb0, fol) in vecs]
    out_specs = [pl.BlockSpec((tr, bw), lambda j, i: (i, j)) for (_, _, bw) in outs]
    out_specs += [pl.BlockSpec((1, w), lambda j, i: (0, j)) for w in accs]
    out_shape = [jax.ShapeDtypeStruct((n_rows, tw), dt) for (dt, tw, _) in outs]
    out_shape += [jax.ShapeDtypeStruct((1, w * nc), F32) for w in accs]

    def body(*refs):
        ins = [r[...] for r in refs[:n_in]]
        vs = [r[...] for r in refs[n_in:n_in + n_vec]]
        o_refs = refs[n_in + n_vec:n_in + n_vec + n_out]
        a_refs = refs[n_in + n_vec + n_out:]
        o_vals, a_vals = fn(ins, vs)
        for r, v in zip(o_refs, o_vals):
            r[...] = v.astype(r.dtype)
        if n_acc:
            i = pl.program_id(1)

            @pl.when(i == 0)
            def _():
                for r in a_refs:
                    r[...] = jnp.zeros_like(r)

            for r, v in zip(a_refs, a_vals):
                r[...] += v

    res = _grid_call(
        body, [r[0] for r in rows] + [v[0] for v in vecs], name=name, grid=(nc, nr), in_specs=in_specs,
        out_specs=out_specs, out_shape=out_shape, scratch_shapes=[], sem=("parallel", "arbitrary"), jobs=jobs)
    outs, accs_out, extra = res[:n_out], res[n_out:n_out + n_acc], res[n_out + n_acc:]
    return (outs, accs_out, extra) if jobs else (outs, accs_out)


def _colsum(x):
    return jnp.sum(x, axis=0, keepdims=True)


def _ln_stats(x):
    mu = jnp.mean(x, axis=-1, keepdims=True)
    xc = x - mu
    var = jnp.mean(xc * xc, axis=-1, keepdims=True)
    rstd = lax.rsqrt(var + LN_EPS)
    return xc * rstd, rstd


def _ln_bwd(dxhat, xhat, rstd):
    return rstd * (dxhat - jnp.mean(dxhat, axis=-1, keepdims=True)
                   - xhat * jnp.mean(dxhat * xhat, axis=-1, keepdims=True))


def _rms_stats(x):
    r = lax.rsqrt(jnp.mean(x * x, axis=-1, keepdims=True) + RMS_EPS)
    return x * r, r


def _rms_bwd(dxhat, xhat, r):
    return r * (dxhat - xhat * jnp.mean(dxhat * xhat, axis=-1, keepdims=True))


def _silu(x):
    return x * jax.nn.sigmoid(x)


def _dsilu(x):
    s = jax.nn.sigmoid(x)
    return s * (1.0 + x * (1.0 - s))


def _rope_fwd(x, cos_t, sin_a, sin_b):
    return x * cos_t + pltpu.roll(x, LANES - ROPE // 2, 1) * sin_a + pltpu.roll(x, ROPE // 2, 1) * sin_b


def _rope_bwd(dy, cos_t, sin_a, sin_b):
    return dy * cos_t + pltpu.roll(dy * sin_a, ROPE // 2, 1) + pltpu.roll(dy * sin_b, LANES - ROPE // 2, 1)


def _rope1(x, tab):
    lane = lax.broadcasted_iota(jnp.int32, x.shape, 1)
    x = jnp.where(lane < ROPE, x, 0.0)
    swapped = pltpu.roll(x, LANES - ROPE // 2, 1) + pltpu.roll(x, ROPE // 2, 1)
    return jnp.where(lane < ROPE, x * tab + swapped * pltpu.roll(tab, ROPE, 1), 0.0)


def _rope1_bwd(dy, tab):
    lane = lax.broadcasted_iota(jnp.int32, dy.shape, 1)
    dy = jnp.where(lane < ROPE, dy, 0.0)
    g = dy * pltpu.roll(tab, ROPE, 1)
    return jnp.where(lane < ROPE, dy * tab + pltpu.roll(g, LANES - ROPE // 2, 1) + pltpu.roll(g, ROPE // 2, 1), 0.0)


def _rope_tables(seq):
    half = ROPE // 2
    inv_freq = ROPE_BASE ** (-jnp.arange(half, dtype=F32) / half)
    ang = jnp.arange(seq, dtype=jnp.int32).astype(F32)[:, None] * inv_freq[None, :]
    cos, sin = jnp.cos(ang), jnp.sin(ang)
    z = jnp.zeros_like(cos)
    cos_t = jnp.concatenate([cos, cos, z, z], axis=1)
    sin_a = jnp.concatenate([-sin, z, z, z], axis=1)
    sin_b = jnp.concatenate([z, sin, z, z], axis=1)
    return cos_t, sin_a, sin_b


def _shifted(win, n):
    return [win] + [pltpu.roll(win, n - j, 0) for j in range(1, SUBLANES)]


def _conv_fwd(u0, w_dw, b_dw, seq, width, name, jobs=()):
    tc = LANES
    nct = width // tc
    rows = CONV_ROWS
    n_win = rows + CONV_TOP

    def body(a_ref, g_ref, w_ref, b_ref, o_ref, vpad):
        vpad[0:CONV_TOP, :] = jnp.zeros((CONV_TOP, tc), F32)
        vpad[CONV_TOP:, :] = a_ref[...] * jax.nn.sigmoid(g_ref[...])

        def chunk(i, carry):
            t0 = pl.multiple_of(i * rows, rows)
            rot = _shifted(vpad[pl.ds(t0, n_win), :], n_win)
            acc = jnp.zeros((rows, tc), F32) + b_ref[...]
            for k in range(CONV_KERNEL):
                s = k + CONV_TOP - (CONV_KERNEL - 1)
                q, j = divmod(s, SUBLANES)
                acc = acc + w_ref[k:k + 1, :] * rot[j][SUBLANES * q:SUBLANES * q + rows]
            o_ref[pl.ds(t0, rows), :] = acc
            return carry

        lax.fori_loop(0, seq // rows, chunk, 0)

    return _grid_call(
        body, (u0, u0, w_dw, b_dw), name=name, grid=(nct,),
        in_specs=[pl.BlockSpec((seq, tc), lambda j: (0, j)), pl.BlockSpec((seq, tc), lambda j: (0, nct + j)),
                  pl.BlockSpec((CONV_KERNEL, tc), lambda j: (0, j)), pl.BlockSpec((1, tc), lambda j: (0, j))],
        out_specs=[pl.BlockSpec((seq, tc), lambda j: (0, j))],
        out_shape=[jax.ShapeDtypeStruct((seq, width), F32)],
        scratch_shapes=[pltpu.VMEM((seq + CONV_TOP, tc), F32)],
        sem=("parallel",), jobs=jobs)


def _conv_bwd(u0, dv2, w_dw, seq, width, name, jobs=()):
    tc = LANES
    nct = width // tc
    rows = CONV_ROWS
    n_win = rows + CONV_TOP
    nsh = CONV_KERNEL * SUBLANES

    def body(a_ref, g_ref, d_ref, w_ref, da_ref, dg_ref, dw_ref, vpad, dpad, dwacc):
        vpad[0:CONV_TOP, :] = jnp.zeros((CONV_TOP, tc), F32)
        vpad[CONV_TOP:, :] = a_ref[...] * jax.nn.sigmoid(g_ref[...])
        dpad[0:seq, :] = d_ref[...]
        dpad[seq:, :] = jnp.zeros((CONV_TOP, tc), F32)
        dwacc[...] = jnp.zeros_like(dwacc)

        def chunk(i, carry):
            t0 = pl.multiple_of(i * rows, rows)
            vrot = _shifted(vpad[pl.ds(t0, n_win), :], n_win)
            drot = _shifted(dpad[pl.ds(t0, n_win), :], n_win)
            dcur = drot[0][0:rows]
            dv1 = jnp.zeros((rows, tc), F32)
            for k in range(CONV_KERNEL):
                q, j = divmod(CONV_KERNEL - 1 - k, SUBLANES)
                dv1 = dv1 + w_ref[k:k + 1, :] * drot[j][SUBLANES * q:SUBLANES * q + rows]
                q, j = divmod(k + CONV_TOP - (CONV_KERNEL - 1), SUBLANES)
                prod = dcur * vrot[j][SUBLANES * q:SUBLANES * q + rows]
                part = prod[0:SUBLANES]
                for r in range(1, rows // SUBLANES):
                    part = part + prod[SUBLANES * r:SUBLANES * (r + 1)]
                dwacc[SUBLANES * k:SUBLANES * (k + 1), :] += part
            a = a_ref[pl.ds(t0, rows), :]
            sg = jax.nn.sigmoid(g_ref[pl.ds(t0, rows), :])
            da_ref[pl.ds(t0, rows), :] = (dv1 * sg).astype(BF16)
            dg_ref[pl.ds(t0, rows), :] = (dv1 * a * sg * (1.0 - sg)).astype(BF16)
            return carry

        lax.fori_loop(0, seq // rows, chunk, 0)
        for k in range(CONV_KERNEL):
            dw_ref[k:k + 1, :] = jnp.sum(dwacc[SUBLANES * k:SUBLANES * (k + 1), :], axis=0, keepdims=True)

    return _grid_call(
        body, (u0, u0, dv2, w_dw), name=name, grid=(nct,),
        in_specs=[pl.BlockSpec((seq, tc), lambda j: (0, j)), pl.BlockSpec((seq, tc), lambda j: (0, nct + j)),
                  pl.BlockSpec((seq, tc), lambda j: (0, j)), pl.BlockSpec((CONV_KERNEL, tc), lambda j: (0, j))],
        out_specs=[pl.BlockSpec((seq, tc), lambda j: (0, j)), pl.BlockSpec((seq, tc), lambda j: (0, j)),
                   pl.BlockSpec((CONV_KERNEL, tc), lambda j: (0, j))],
        out_shape=[jax.ShapeDtypeStruct((seq, width), BF16), jax.ShapeDtypeStruct((seq, width), BF16),
                   jax.ShapeDtypeStruct((CONV_KERNEL, width), F32)],
        scratch_shapes=[pltpu.VMEM((seq + CONV_TOP, tc), F32), pltpu.VMEM((seq + CONV_TOP, tc), F32),
                        pltpu.VMEM((nsh, tc), F32)],
        sem=("parallel",), jobs=jobs)


def _lanes(x, n):
    return jnp.tile(x, (1, n // LANES))


def _diag_mask(r0, rows, t, transposed=False):
    r = lax.broadcasted_iota(jnp.int32, (rows, t), 0) + r0
    c = lax.broadcasted_iota(jnp.int32, (rows, t), 1)
    return (c >= r) if transposed else (r >= c)


def _attn_fwd(qf, tab, kv, kr, u1, z_block0, seq, heads, name, jobs=()):
    t = min(ATTN_BLOCK, seq)
    nb = seq // t
    ch = ATTN_CHUNK

    def body(qf_ref, tab_ref, kv_ref, kr_ref, z_ref, o_ref, lse_ref, qb_ref, g_ref, kf_s, q_s, sa_s, sb_s, p_s, m_s, l_s, c_s,
             acc_s):
        qi = pl.program_id(1)

        @pl.when(qi == 0)
        def _():
            kf_s[:, 0:NOPE] = kv_ref[:, 0:NOPE]
            kf_s[:, NOPE:] = kr_ref[...]

        q_s[:, 0:NOPE] = (qf_ref[:, 0:NOPE] * ATTN_SCALE).astype(BF16)
        q_s[:, NOPE:] = (_rope1(qf_ref[:, NOPE:], tab_ref[...]) * ATTN_SCALE).astype(BF16)
        qb_ref[...] = q_s[...]
        m_s[...] = jnp.full_like(m_s, -jnp.inf)
        l_s[...] = jnp.zeros_like(l_s)
        acc_s[...] = jnp.zeros_like(acc_s)

        def scores(ki, s_s):
            k0 = pl.multiple_of(ki * t, t)
            s_s[...] = lax.dot_general(q_s[...], kf_s[pl.ds(k0, t), :], _DIMS["nt"], preferred_element_type=F32)

        def consume(ki, s_s, masked):
            k0 = pl.multiple_of(ki * t, t)

            def chunk(c, carry):
                r0 = pl.multiple_of(c * ch, ch)
                rows = pl.ds(r0, ch)
                s_c = s_s[rows, :]
                if masked:
                    s_c = jnp.where(_diag_mask(r0, ch, t), s_c, -jnp.inf)
                m_prev = m_s[rows, :]
                m_new = jnp.maximum(m_prev, jnp.max(s_c, axis=1, keepdims=True))
                p = jnp.exp(s_c - _lanes(m_new, t))
                corr = jnp.exp(m_prev - m_new)
                l_s[rows, :] = corr * l_s[rows, :] + jnp.sum(p, axis=1, keepdims=True)
                m_s[rows, :] = m_new
                c_s[rows, :] = corr
                p_s[rows, :] = p.astype(BF16)
                return carry

            lax.fori_loop(0, t // ch, chunk, 0, unroll=True)
            acc_s[...] = c_s[...] * acc_s[...] + jnp.dot(p_s[...], kv_ref[pl.ds(k0, t), pl.ds(NOPE, VDIM)],
                                                         preferred_element_type=F32)

        def pair(j, carry):
            scores(2 * j + 1, sb_s)
            consume(2 * j, sa_s, False)
            scores(2 * j + 2, sa_s)
            consume(2 * j + 1, sb_s, False)
            return carry

        scores(0, sa_s)
        lax.fori_loop(0, qi // 2, pair, 0)

        @pl.when(qi % 2 == 0)
        def _():
            consume(qi, sa_s, True)

        @pl.when(qi % 2 == 1)
        def _():
            scores(qi, sb_s)
            consume(qi - 1, sa_s, False)
            consume(qi, sb_s, True)

        o = acc_s[...] / l_s[...]
        o_ref[...] = o
        g_ref[...] = (o * _silu(z_ref[...])).astype(BF16)
        lse_ref[...] = m_s[...] + jnp.log(l_s[...])

    return _grid_call(
        body, (qf, tab, kv, kr, u1), name=name, grid=(heads, nb),
        in_specs=[pl.BlockSpec((t, QPAD), lambda h, i: (i, h)),
                  pl.BlockSpec((t, LANES), lambda h, i: (i, 0)),
                  pl.BlockSpec((seq, QPAD), lambda h, i: (0, h)),
                  pl.BlockSpec((seq, LANES), lambda h, i: (0, 0)),
                  pl.BlockSpec((t, VDIM), lambda h, i: (i, z_block0 + h))],
        out_specs=[pl.BlockSpec((t, VDIM), lambda h, i: (i, h)),
                   pl.BlockSpec((t, LANES), lambda h, i: (i, h)),
                   pl.BlockSpec((t, QPAD), lambda h, i: (i, h)),
                   pl.BlockSpec((t, VDIM), lambda h, i: (i, h))],
        out_shape=[jax.ShapeDtypeStruct((seq, heads * VDIM), F32), jax.ShapeDtypeStruct((seq, heads * LANES), F32),
                   jax.ShapeDtypeStruct((seq, heads * QPAD), BF16), jax.ShapeDtypeStruct((seq, heads * VDIM), BF16)],
        scratch_shapes=[pltpu.VMEM((seq, QPAD), BF16), pltpu.VMEM((t, QPAD), BF16), pltpu.VMEM((t, t), F32),
                        pltpu.VMEM((t, t), F32), pltpu.VMEM((t, t), BF16), pltpu.VMEM((t, LANES), F32),
                        pltpu.VMEM((t, LANES), F32), pltpu.VMEM((t, LANES), F32), pltpu.VMEM((t, VDIM), F32)],
        sem=("parallel", "arbitrary"), jobs=jobs)


def _attn_bwd(q, tab, kv, kr, do, o, lse, seq, heads, name, jobs=()):
    t = min(ATTN_BLOCK, seq)
    nb = seq // t
    ch = ATTN_CHUNK

    def body(q_ref, tab_ref, kv_ref, kr_ref, do_ref, o_ref, lse_ref, dqb_ref, dkv_ref, dkr_ref,
             dq_ref, kf_s, qt_s, dot_s, dl_s, s_s, dp_s, p_s, ds_s, dkt_s, dvt_s):
        @pl.when(pl.program_id(0) == 0)
        def _():
            dkr_ref[...] = jnp.zeros_like(dkr_ref)

        kf_s[:, 0:NOPE] = kv_ref[:, 0:NOPE]
        kf_s[:, NOPE:] = kr_ref[...]
        dq_ref[...] = jnp.zeros_like(dq_ref)
        for b in range(nb):
            rows = pl.ds(b * t, t)
            dob = do_ref[rows, :].astype(F32)
            qt_s[b] = jnp.transpose(q_ref[rows, :].astype(F32)).astype(BF16)
            dot_s[b] = jnp.transpose(dob).astype(BF16)
            dl_s[rows, :] = jnp.broadcast_to(jnp.sum(dob * o_ref[rows, :], axis=1, keepdims=True), (t, LANES))

        def key_block(ki, carry):
            k0 = pl.multiple_of(ki * t, t)
            kfb = kf_s[pl.ds(k0, t), :]
            dkt_s[...] = jnp.zeros_like(dkt_s)
            dvt_s[...] = jnp.zeros_like(dvt_s)

            def scores(qi, s_s, dp_s):
                q0 = pl.multiple_of(qi * t, t)
                s_s[...] = lax.dot_general(q_ref[pl.ds(q0, t), :], kfb, _DIMS["nt"], preferred_element_type=F32)
                dp_s[...] = lax.dot_general(do_ref[pl.ds(q0, t), :], kv_ref[pl.ds(k0, t), pl.ds(NOPE, VDIM)],
                                            _DIMS["nt"], preferred_element_type=F32)

            def consume(qi, s_s, dp_s, masked):
                q0 = pl.multiple_of(qi * t, t)

                def chunk(c, carry):
                    r0 = pl.multiple_of(c * ch, ch)
                    rows = pl.ds(r0, ch)
                    grows = pl.ds(pl.multiple_of(q0 + r0, ch), ch)
                    p = jnp.exp(s_s[rows, :] - _lanes(lse_ref[grows, :], t))
                    if masked:
                        p = jnp.where(_diag_mask(r0, ch, t), p, 0.0)
                    p_s[rows, :] = p.astype(BF16)
                    ds_s[rows, :] = (p * (dp_s[rows, :] - _lanes(dl_s[grows, :], t))).astype(BF16)
                    return carry

                lax.fori_loop(0, t // ch, chunk, 0, unroll=True)
                dq_ref[pl.ds(q0, t), :] += jnp.dot(ds_s[...], kfb, preferred_element_type=F32)
                dvt_s[...] += jnp.dot(dot_s[qi], p_s[...], preferred_element_type=F32)
                dkt_s[...] += jnp.dot(qt_s[qi], ds_s[...], preferred_element_type=F32)

            def block(qi, masked):
                scores(qi, s_s, dp_s)
                consume(qi, s_s, dp_s, masked)

            def above(qi, carry):
                block(qi, False)
                return carry

            block(ki, True)
            lax.fori_loop(ki + 1, nb, above, 0)
            dk = jnp.transpose(dkt_s[...])
            dkv_ref[pl.ds(k0, t), :] = jnp.concatenate([dk[:, :NOPE], jnp.transpose(dvt_s[...])], axis=1).astype(BF16)
            dkr_ref[pl.ds(k0, t), :] += dk[:, NOPE:]
            return carry

        lax.fori_loop(0, nb, key_block, 0)
        for b in range(nb):
            rows = pl.ds(b * t, t)
            dqb_ref[rows, 0:NOPE] = (dq_ref[rows, 0:NOPE] * ATTN_SCALE).astype(BF16)
            dqb_ref[rows, NOPE:] = (_rope1_bwd(dq_ref[rows, NOPE:], tab_ref[rows, :]) * ATTN_SCALE).astype(BF16)

    head = lambda w: pl.BlockSpec((seq, w), lambda h: (0, h))
    shared = pl.BlockSpec((seq, LANES), lambda h: (0, 0))
    return _grid_call(
        body, (q, tab, kv, kr, do, o, lse), name=name, grid=(heads,),
        in_specs=[head(QPAD), shared, head(QPAD), shared, head(VDIM), head(VDIM), head(LANES)],
        out_specs=[head(QPAD), head(QPAD), shared],
        out_shape=[jax.ShapeDtypeStruct((seq, heads * QPAD), BF16), jax.ShapeDtypeStruct((seq, heads * QPAD), BF16),
                   jax.ShapeDtypeStruct((seq, LANES), F32)],
        scratch_shapes=[pltpu.VMEM((seq, QPAD), F32),
                        pltpu.VMEM((seq, QPAD), BF16), pltpu.VMEM((nb, QPAD, t), BF16), pltpu.VMEM((nb, VDIM, t), BF16),
                        pltpu.VMEM((seq, LANES), F32), pltpu.VMEM((t, t), F32), pltpu.VMEM((t, t), F32),
                        pltpu.VMEM((t, t), BF16), pltpu.VMEM((t, t), BF16), pltpu.VMEM((QPAD, t), F32),
                        pltpu.VMEM((VDIM, t), F32)],
        sem=("arbitrary",), jobs=jobs)


def _place():
    x, y, c = lax.axis_index("x"), lax.axis_index("y"), lax.axis_index("c")
    return x, y, c


class _Gather:
    def __init__(self, shard):
        r, n = shard.shape
        self.ins = [shard]
        self.outs = [jax.ShapeDtypeStruct((N_DEV, r, n), shard.dtype)]
        self.sems = [pltpu.SemaphoreType.DMA((7,)), pltpu.SemaphoreType.DMA((7,)), pltpu.SemaphoreType.DMA]

    def bind(self, ins, outs, sems):
        (self.x_ref,), (self.out_ref,) = ins, outs
        self.send, self.recv, self.local = sems

    def _plan(self):
        x, y, c = _place()
        chips = [(1 - x, y), (x, 1 - y), (1 - x, 1 - y)]

        def copy(k, block, to, from_shard=False):
            px, py, pc = block
            slot = self.out_ref.at[4 * px + 2 * py + pc]
            return pltpu.make_async_remote_copy(
                src_ref=self.x_ref if from_shard else slot, dst_ref=slot, send_sem=self.send.at[k],
                recv_sem=self.recv.at[k], device_id=to, device_id_type=MESH)

        me, sibling = (x, y, c), (x, y, 1 - c)
        mine = pltpu.make_async_copy(self.x_ref, self.out_ref.at[4 * x + 2 * y + c], self.local)
        first = [copy(0, me, sibling, True)] + [copy(1 + j, me, (*chip, c), True) for j, chip in enumerate(chips)]
        landed = [copy(1 + j, (*chip, c), me) for j, chip in enumerate(chips)]
        passed = [copy(4 + j, (*chip, c), sibling) for j, chip in enumerate(chips)]
        from_sibling = [copy(0, sibling, me)] + [copy(4 + j, (*chip, 1 - c), me) for j, chip in enumerate(chips)]
        return mine, first, landed, passed, from_sibling

    def start(self):
        mine, first, _, _, _ = self._plan()
        mine.start()
        for cp in first:
            cp.start()

    def mid(self):
        _, _, landed, passed, _ = self._plan()
        for got, fwd in zip(landed, passed):
            got.wait_recv()
            fwd.start()

    def finish(self):
        mine, first, _, passed, from_sibling = self._plan()
        for cp in from_sibling:
            cp.wait_recv()
        for cp in first + passed:
            cp.wait_send()
        mine.wait()


class _PairExchange:
    def __init__(self, g8):
        _, r, n = g8.shape
        self.ins = [g8]
        self.outs = [jax.ShapeDtypeStruct((N_CHIP, r, n), g8.dtype)]
        self.sems = [pltpu.SemaphoreType.DMA((N_CHIP,)), pltpu.SemaphoreType.DMA((N_CHIP,))]

    def bind(self, ins, outs, sems):
        (self.g_ref,), (self.out_ref,) = ins, outs
        self.send, self.recv = sems

    def _plan(self):
        x, y, c = _place()
        return [pltpu.make_async_remote_copy(
            src_ref=self.g_ref.at[2 * p + (1 - c)], dst_ref=self.out_ref.at[p], send_sem=self.send.at[p],
            recv_sem=self.recv.at[p], device_id=(x, y, 1 - c), device_id_type=MESH) for p in range(N_CHIP)]

    def start(self):
        for cp in self._plan():
            cp.start()

    def mid(self):
        pass

    def finish(self):
        copies = self._plan()
        for cp in copies:
            cp.wait_recv()
        for cp in copies:
            cp.wait_send()


class _ChipExchange:
    def __init__(self, t4):
        _, r, n = t4.shape
        self.ins = [t4]
        self.outs = [jax.ShapeDtypeStruct((N_CHIP, r, n), t4.dtype)]
        self.sems = [pltpu.SemaphoreType.DMA((3,)), pltpu.SemaphoreType.DMA((3,)), pltpu.SemaphoreType.DMA]

    def bind(self, ins, outs, sems):
        (self.t_ref,), (self.out_ref,) = ins, outs
        self.send, self.recv, self.local_sem = sems

    def _plan(self):
        x, y, c = _place()
        mine = 2 * x + y
        chips = [(1 - x, y), (x, 1 - y), (1 - x, 1 - y)]
        local = pltpu.make_async_copy(self.t_ref.at[mine], self.out_ref.at[mine], self.local_sem)
        sends = [pltpu.make_async_remote_copy(
            src_ref=self.t_ref.at[2 * px + py], dst_ref=self.out_ref.at[mine], send_sem=self.send.at[k],
            recv_sem=self.recv.at[k], device_id=(px, py, c), device_id_type=MESH) for k, (px, py) in enumerate(chips)]
        arrivals = [pltpu.make_async_remote_copy(
            src_ref=self.t_ref.at[mine], dst_ref=self.out_ref.at[2 * px + py], send_sem=self.send.at[k],
            recv_sem=self.recv.at[k], device_id=(px, py, c), device_id_type=MESH) for k, (px, py) in enumerate(chips)]
        return local, sends, arrivals

    def start(self):
        local, sends, _ = self._plan()
        local.start()
        for cp in sends:
            cp.start()

    def mid(self):
        pass

    def finish(self):
        local, sends, arrivals = self._plan()
        for cp in arrivals:
            cp.wait_recv()
        for cp in sends:
            cp.wait_send()
        local.wait()


def _job_specs(jobs):
    ins = [a for j in jobs for a in j.ins]
    outs = [o for j in jobs for o in j.outs]
    sems = [s for j in jobs for s in j.sems]
    return ins, outs, sems


def _bind_jobs(jobs, ins, outs, sems):
    for j in jobs:
        ni, no, ns = len(j.ins), len(j.outs), len(j.sems)
        j.bind(ins[:ni], outs[:no], sems[:ns])
        ins, outs, sems = ins[ni:], outs[no:], sems[ns:]


def _carry(body, jobs, n_in, n_out, grid):
    if not jobs:
        return body
    j_in, j_out, j_sem = (len(v) for v in _job_specs(jobs))
    total = math.prod(grid)
    mid_step = (3 * total) // 4

    def wrapped(*refs):
        ins, rest = refs[:n_in], refs[n_in:]
        jin, rest = rest[:j_in], rest[j_in:]
        outs, rest = rest[:n_out], rest[n_out:]
        jout, rest = rest[:j_out], rest[j_out:]
        scratch, jsem = rest[:len(rest) - j_sem], rest[len(rest) - j_sem:]
        _bind_jobs(jobs, jin, jout, jsem)
        lin = 0
        for d, size in enumerate(grid):
            lin = lin * size + pl.program_id(d)

        @pl.when(lin == 0)
        def _():
            for j in jobs:
                j.start()

        body(*ins, *outs, *scratch)

        @pl.when(lin == mid_step)
        def _():
            for j in jobs:
                j.mid()

        @pl.when(lin == total - 1)
        def _():
            for j in jobs:
                j.finish()

    return wrapped


def _carry_call(jobs):
    ins, outs, sems = _job_specs(jobs)
    return ins, [ANY] * len(ins), [ANY] * len(outs), outs, sems


def _grid_call(body, operands, *, name, grid, in_specs, out_specs, out_shape, scratch_shapes, sem, jobs=()):
    j_ins, j_in_specs, j_out_specs, j_outs, j_sems = _carry_call(jobs)
    return pl.pallas_call(
        _carry(body, jobs, len(in_specs), len(out_specs), grid), name=name, grid=grid,
        in_specs=list(in_specs) + j_in_specs, out_specs=list(out_specs) + j_out_specs,
        out_shape=list(out_shape) + j_outs, scratch_shapes=list(scratch_shapes) + j_sems,
        compiler_params=_params(("arbitrary",) * len(grid) if jobs else sem, has_side_effects=bool(jobs)),
    )(*operands, *j_ins)


def _comm(jobs, name):
    ins, outs, sems = _job_specs(jobs)

    def body(*refs):
        _bind_jobs(jobs, refs[:len(ins)], refs[len(ins):len(ins) + len(outs)], refs[len(ins) + len(outs):])
        for j in jobs:
            j.start()
        for j in jobs:
            j.mid()
        for j in jobs:
            j.finish()

    return pl.pallas_call(
        body, name=name, in_specs=[ANY] * len(ins), out_specs=[ANY] * len(outs), out_shape=outs,
        scratch_shapes=sems, compiler_params=pltpu.CompilerParams(has_side_effects=True),
    )(*ins)


def _all_gather(shard, name):
    return _comm([_Gather(shard)], name)[0]


def _pair_add(g8, got, core, name):
    _, r, n = g8.shape
    tr = _pick(r, (256, 128, 64, 32, 16, 8))

    def body(c_ref, g_ref, r_ref, o_ref):
        o_ref[...] = (g_ref[...].astype(F32) + r_ref[...].astype(F32)).astype(o_ref.dtype)

    grid_spec = pltpu.PrefetchScalarGridSpec(
        num_scalar_prefetch=1, grid=(N_CHIP, r // tr),
        in_specs=[pl.BlockSpec((None, tr, n), lambda p, i, c: (2 * p + c[0], i, 0)),
                  pl.BlockSpec((None, tr, n), lambda p, i, c: (p, i, 0))],
        out_specs=pl.BlockSpec((None, tr, n), lambda p, i, c: (p, i, 0)))
    return pl.pallas_call(
        body, name=name, grid_spec=grid_spec, out_shape=jax.ShapeDtypeStruct((N_CHIP, r, n), g8.dtype),
        compiler_params=_params(("parallel", "parallel")),
    )(core, g8, got)


def _slot_sum(g, name):
    k, r, n = g.shape

    def body(g_ref, o_ref):
        acc = g_ref[0].astype(F32)
        for s in range(1, k):
            acc = acc + g_ref[s].astype(F32)
        o_ref[...] = acc

    return pl.pallas_call(body, name=name, out_shape=jax.ShapeDtypeStruct((r, n), F32),
                          compiler_params=pltpu.CompilerParams(vmem_limit_bytes=VMEM_LIMIT))(g)


def _pack_rows(parts, n_rows, width, name):
    starts = [r0 for _, r0 in parts]

    def body(*refs):
        o_ref = refs[-1]
        o_ref[...] = jnp.zeros_like(o_ref)
        for ref, r0 in zip(refs[:-1], starts):
            k, w = ref.shape
            o_ref[r0:r0 + k, 0:w] = ref[...]

    return pl.pallas_call(body, name=name, out_shape=jax.ShapeDtypeStruct((n_rows, width), F32),
                          compiler_params=pltpu.CompilerParams(vmem_limit_bytes=VMEM_LIMIT))(*[a for a, _ in parts])


def _adamw(w, gparts, m, v, name):
    r, n = w.shape
    k = gparts.shape[0]
    tr = _pick(r, [t for t in (512, 256, 128, 64, 32, 16) if t * n <= ADAMW_TILE_ELEMS])
    c1 = 1.0 - ADAM_B1 ** ADAM_STEP
    c2 = 1.0 - ADAM_B2 ** ADAM_STEP

    def body(w_ref, g_ref, m_ref, v_ref, go_ref, d_ref, mo_ref, vo_ref):
        g = g_ref[0].astype(F32)
        for s in range(1, k):
            g = g + g_ref[s].astype(F32)
        mn = ADAM_B1 * m_ref[...] + (1.0 - ADAM_B1) * g
        vn = ADAM_B2 * v_ref[...] + (1.0 - ADAM_B2) * (g * g)
        go_ref[...] = g
        mo_ref[...] = mn
        vo_ref[...] = vn
        d_ref[...] = -ADAM_LR * ((mn / c1) / (jnp.sqrt(vn / c2) + ADAM_EPS) + ADAM_WD * w_ref[...])

    spec = pl.BlockSpec((tr, n), lambda i: (i, 0))
    return pl.pallas_call(
        body, name=name, grid=(r // tr,),
        in_specs=[spec, pl.BlockSpec((k, tr, n), lambda i: (0, i, 0)), spec, spec],
        out_specs=[spec] * 4, out_shape=[jax.ShapeDtypeStruct((r, n), F32)] * 4,
        compiler_params=_params(("parallel",)),
    )(w, gparts, m, v)


def kernel(x, c, w_ada, b_ada, ln_g, ln_b, a_w_in, a_w_dw, a_b_dw, a_norm_g, a_norm_b, a_w_out, b_w_in, b_q_norm_g, b_w_qb, b_w_out, kv_w_a, kv_norm_g, kv_w_b, loss_target, m_w_ada, m_b_ada, m_ln_g, m_ln_b, m_a_w_in, m_a_w_dw, m_a_b_dw, m_a_norm_g, m_a_norm_b, m_a_w_out, m_b_w_in, m_b_q_norm_g, m_b_w_qb, m_b_w_out, m_kv_w_a, m_kv_norm_g, m_kv_w_b, v_w_ada, v_b_ada, v_ln_g, v_ln_b, v_a_w_in, v_a_w_dw, v_a_b_dw, v_a_norm_g, v_a_norm_b, v_a_w_out, v_b_w_in, v_b_q_norm_g, v_b_w_qb, v_b_w_out, v_kv_w_a, v_kv_norm_g, v_kv_w_b):
    S, D = x.shape[1], x.shape[2]
    C = a_w_out.shape[1] * N_DEV
    QL = b_q_norm_g.shape[1]
    KL = kv_norm_g.shape[0]
    H = kv_w_b.shape[1] * N_DEV // (NOPE + VDIM)
    HL = H // N_DEV
    W = H * VDIM
    assert a_w_dw.shape[1] == CONV_KERNEL and kv_w_a.shape[1] == KL + ROPE and b_w_qb.shape[2] == HL * (NOPE + ROPE)
    assert b_w_in.shape[2] * N_DEV == QL + W and D % (N_DEV * LANES) == 0 and S % CONV_ROWS == 0
    TW = 1024 if W % 1024 == 0 and QL % 1024 == 0 else LANES

    xi, yi, ci = _place()
    idx = 4 * xi + 2 * yi + ci
    core = jnp.reshape(ci, (1,)).astype(jnp.int32)
    x2, tgt = x[0], loss_target[0]
    cos_t, sin_a, sin_b = _rope_tables(S)
    rope_rows = [(cos_t, LANES, 0, False), (sin_a, LANES, 0, False), (sin_b, LANES, 0, False)]

    win0 = _all_gather(a_w_in[0].astype(BF16), "ag_a_w_in")
    sh_wout0 = a_w_out[0].astype(BF16)
    sh_wbin = b_w_in[0].astype(BF16)
    wqb_loc = jnp.pad(b_w_qb[0].reshape(QL, HL, NOPE + ROPE), ((0, 0), (0, 0), (0, QPAD - NOPE - ROPE)))
    sh_wqb = wqb_loc.reshape(QL, HL * QPAD).astype(BF16)
    sh_wbout = b_w_out[0].astype(BF16)
    kva_w = KL + LANES
    sh_wkva = jnp.pad(kv_w_a, ((0, 0), (0, kva_w - KL - ROPE))).astype(BF16)
    sh_wkvb = kv_w_b.astype(BF16)

    cl = C // N_DEV
    v0 = CONV_TOP
    small_loc = _pack_rows([(a_w_dw[0], 0), (a_b_dw, v0), (a_norm_g, v0 + 1), (a_norm_b, v0 + 2)], 40, cl, "pack_conv_small")
    small = _all_gather(small_loc, "ag_conv_small")
    small = jnp.transpose(small, (1, 0, 2)).reshape(40, C)
    w_dw, b_dw, g_cn, b_cn = small[:CONV_KERNEL], small[v0:v0 + 1], small[v0 + 1:v0 + 2], small[v0 + 2:v0 + 3]

    c_all = _all_gather(jnp.broadcast_to(c, (SUBLANES, D)), "ag_c")[:, 0, :]
    c_pad = jnp.concatenate([c_all, jnp.zeros((16 - N_DEV, D), F32)], axis=0)
    (sc,), _ = _rowwise(lambda r, v: ([_silu(r[0])], []), [(c_pad, D, 0, False)], [], [(BF16, D, D)], [],
                        n_rows=16, tr=16, name="silu_c")
    nl = 3 * D // N_DEV
    zsc = jnp.zeros_like(sc)
    sc2 = jnp.concatenate([jnp.concatenate([sc, zsc], axis=1), jnp.concatenate([zsc, sc], axis=1)], axis=0)
    mod_loc = _mm(sc2, w_ada.reshape(DEPTH * D, nl), "nn", F32, "mod")
    mod_all = jnp.transpose(_all_gather(mod_loc, "ag_mod"), (1, 0, 2)).reshape(DEPTH, 16, 3 * D)
    mods = []
    for l in range(DEPTH):
        row = lax.dynamic_slice(mod_all[l], (idx, 0), (1, 3 * D)) + b_ada[l][None, :]
        mods.append((row[:, :D], row[:, D:2 * D], row[:, 2 * D:]))
    (shift0, scale0, gate0), (shift1, scale1, gate1) = mods
    vec = lambda a, w=None: (a, a.shape[1] if w is None else w, 0, False)

    def f_mod0(r, v):
        xh, _ = _ln_stats(r[0])
        return [xh * (1.0 + v[0]) + v[1]], []
    (h0,), _ = _rowwise(f_mod0, [(x2, D, 0, False)], [vec(scale0), vec(shift0)], [(BF16, D, D)], [],
                        n_rows=S, tr=128, name="l0_modulate")
    u0, wbin = _mm(h0, win0, "nn", F32, "l0_in", b_slots=True, jobs=[_Gather(sh_wbin)])
    v2, wout0 = _conv_fwd(u0, w_dw, b_dw, S, C, "l0_conv", jobs=[_Gather(sh_wout0)])
    wout0 = wout0.reshape(C, D)

    def f_gate0(r, v):
        xh, _ = _ln_stats(r[0])
        return [_silu(xh * v[0] + v[1]) * _silu(r[1])], []
    (p0,), _ = _rowwise(f_gate0, [(v2, C, 0, False), (u0, C, 2, False)], [vec(g_cn), vec(b_cn)], [(BF16, C, C)], [],
                        n_rows=S, tr=128, name="l0_gate")
    o0, wkva, wkvb = _mm(p0, wout0, "nn", F32, "l0_out", jobs=[_Gather(sh_wkva), _Gather(sh_wkvb)])
    wkva = wkva.reshape(D, kva_w)

    def f_res0(r, v):
        gate, g0, b0, sc1, sh1 = v
        xh, _ = _ln_stats(ALPHA * r[0] + (1.0 + gate) * r[1])
        x1 = xh * g0 + b0
        xh1, _ = _ln_stats(x1)
        return [x1, x1, xh1 * (1.0 + sc1) + sh1], []
    (x1, x1b, h1), _ = _rowwise(f_res0, [(x2, D, 0, False), (o0, D, 0, False)],
                                [vec(gate0), vec(ln_g[0:1]), vec(ln_b[0:1]), vec(scale1), vec(shift1)],
                                [(F32, D, D), (BF16, D, D), (BF16, D, D)], [], n_rows=S, tr=128, name="l0_residual")

    kva = _mm(x1b, wkva, "nn", F32, "kv_a")
    g_kv = kv_norm_g[None, :]

    def f_kvn(r, v):
        xh, _ = _rms_stats(r[0][:, :KL])
        return [xh * v[0], _rope_fwd(r[0][:, KL:], r[1], r[2], r[3])], []
    (ckv, krp), _ = _rowwise(f_kvn, [(kva, kva_w, 0, False)] + rope_rows, [vec(g_kv)],
                             [(BF16, KL, KL), (BF16, LANES, LANES)], [], n_rows=S, tr=256, name="kv_norm_rope")
    kvh = _mm(ckv, wkvb, "nn", BF16, "kv_b", b_slots=True)

    u1, wqb = _mm(h1, wbin, "nn", F32, "l1_in", b_slots=True, jobs=[_Gather(sh_wqb)])
    g_q = b_q_norm_g[0:1]

    def f_qn(r, v):
        xh, _ = _rms_stats(r[0])
        return [xh * v[0]], []
    (cqn,), _ = _rowwise(f_qn, [(u1, QL, 0, False)], [vec(g_q)], [(BF16, QL, QL)], [], n_rows=S, tr=256, name="q_norm")
    qf = _mm(cqn, wqb, "nn", F32, "q_b", b_slots=True)

    rope_tab = jnp.concatenate([cos_t[:, :ROPE], sin_a[:, :ROPE // 2], sin_b[:, ROPE // 2:ROPE]], axis=1)
    oa, lse, qb, p1, wbout = _attn_fwd(qf, rope_tab, kvh, krp, u1, QL // VDIM, S, H, "attn_fwd", jobs=[_Gather(sh_wbout)])
    wbout = wbout.reshape(W, D)
    zc0 = QL // TW
    o1 = _mm(p1, wbout, "nn", F32, "l1_out")

    def f_head(r, v):
        gate, g1, b1 = v
        xh, rstd = _ln_stats(ALPHA * r[0] + (1.0 + gate) * r[1])
        err = xh * g1 + b1 - r[2]
        dy = err * (1.0 / D)
        dr = _ln_bwd(dy * g1, xh, rstd)
        return [dr, (1.0 + gate) * dr], [_colsum(err * err), _colsum(dy * xh), _colsum(dy), _colsum(dr * r[1])]
    (dr1, do1b), (loss_row, dlng1, dlnb1, dgate1) = _rowwise(
        f_head, [(x1, D, 0, False), (o1, D, 0, False), (tgt, D, 0, False)], [vec(gate1), vec(ln_g[1:2]), vec(ln_b[1:2])],
        [(F32, D, D), (BF16, D, D)], [D, D, D, D], n_rows=S, tr=64, name="head_loss")
    loss = lax.psum(0.5 / D * jnp.sum(loss_row), ("x", "y", "c"))

    dp1 = _mm(do1b, wbout, "nt", F32, "l1_out_dx")
    g_wbout = _mm(p1, do1b, "tn", BF16, "l1_out_dw").reshape(N_DEV, W // N_DEV, D)

    def f_dgate1(r, v):
        dp, o, z = r
        return [dp * _silu(z), dp * o * _dsilu(z)], []
    (doa, dz1), _, (got,) = _rowwise(f_dgate1, [(dp1, TW, 0, True), (oa, TW, 0, True), (u1, TW, zc0, True)], [],
                                     [(BF16, W, TW), (BF16, W, TW)], [], n_rows=S, tr=512 if S % 512 == 0 else S,
                                     name="l1_gate_bwd", nc=W // TW, jobs=[_PairExchange(g_wbout)])
    t_wbout = _pair_add(g_wbout, got, core, "rs_b_w_out_add")
    dqb, dkvh, dkrp, r_wbout = _attn_bwd(qb, rope_tab, kvh, krp, doa, oa, lse, S, H, "attn_bwd",
                                         jobs=[_ChipExchange(t_wbout)])
    g_wqb = _mm(cqn, dqb, "tn", BF16, "q_b_dw", out_slots=True)
    dcqn, got = _mm(dqb, wqb, "nt", F32, "q_b_dx", b_slots=True, jobs=[_PairExchange(g_wqb)])
    t_wqb = _pair_add(g_wqb, got, core, "rs_b_w_qb_add")

    def f_qn_bwd(r, v):
        xh, rr = _rms_stats(r[0])
        return [_rms_bwd(r[1] * v[0], xh, rr)], [_colsum(r[1] * xh)]
    (dcq,), (dg_q,) = _rowwise(f_qn_bwd, [(u1, QL, 0, False), (dcqn, QL, 0, False)], [vec(g_q)], [(BF16, QL, QL)], [QL],
                               n_rows=S, tr=256, name="q_norm_bwd")
    du1 = jnp.concatenate([dcq, dz1], axis=1)
    g_wbin, r_wqb = _mm(h1, du1, "tn", BF16, "l1_in_dw", out_slots=True, jobs=[_ChipExchange(t_wqb)])
    dh1, got = _mm(du1, wbin, "nt", F32, "l1_in_dx", b_slots=True, jobs=[_PairExchange(g_wbin)])
    t_wbin = _pair_add(g_wbin, got, core, "rs_b_w_in_add")

    g_wkvb = _mm(ckv, dkvh, "tn", BF16, "kv_b_dw", out_slots=True)
    dckv, got = _mm(dkvh, wkvb, "nt", F32, "kv_b_dx", b_slots=True, jobs=[_PairExchange(g_wkvb)])
    t_wkvb = _pair_add(g_wkvb, got, core, "rs_kv_w_b_add")

    def f_kvn_bwd(r, v):
        xh, rr = _rms_stats(r[0][:, :KL])
        dck = _rms_bwd(r[1] * v[0], xh, rr)
        dkr = _rope_bwd(r[2], r[3], r[4], r[5])
        return [jnp.concatenate([dck, dkr], axis=1)], [_colsum(r[1] * xh)]
    (dkva,), (dg_kv,) = _rowwise(f_kvn_bwd, [(kva, kva_w, 0, False), (dckv, KL, 0, False), (dkrp, LANES, 0, False)] + rope_rows,
                                 [vec(g_kv)], [(BF16, kva_w, kva_w)], [KL], n_rows=S, tr=256, name="kv_norm_rope_bwd")
    g_wkva = _mm(x1b, dkva, "tn", BF16, "kv_a_dw").reshape(N_DEV, D // N_DEV, kva_w)
    dx1_kv, got = _mm(dkva, wkva, "nt", F32, "kv_a_dx", jobs=[_PairExchange(g_wkva)])
    t_wkva = _pair_add(g_wkva, got, core, "rs_kv_w_a_add")

    def f_res0_bwd(r, v):
        dr1_, dxkv, dh, x1_, x_, o_ = r
        sc1, gate, g0 = v
        xh1, rstd1 = _ln_stats(x1_)
        dx1 = ALPHA * dr1_ + dxkv + _ln_bwd(dh * (1.0 + sc1), xh1, rstd1)
        xh0, rstd0 = _ln_stats(ALPHA * x_ + (1.0 + gate) * o_)
        dr0 = _ln_bwd(dx1 * g0, xh0, rstd0)
        return ([dr0, (1.0 + gate) * dr0],
                [_colsum(dh * xh1), _colsum(dh), _colsum(dx1 * xh0), _colsum(dx1), _colsum(dr0 * o_)])
    (dr0, do0b), (dscale1, dshift1, dlng0, dlnb0, dgate0) = _rowwise(
        f_res0_bwd, [(a, D, 0, False) for a in (dr1, dx1_kv, dh1, x1, x2, o0)], [vec(scale1), vec(gate0), vec(ln_g[0:1])],
        [(F32, D, D), (BF16, D, D)], [D] * 5, n_rows=S, tr=64, name="l0_residual_bwd")

    dp0, r_wkvb, r_wkva = _mm(do0b, wout0, "nt", F32, "l0_out_dx",
                              jobs=[_ChipExchange(t_wkvb), _ChipExchange(t_wkva)])
    g_wout0 = _mm(p0, do0b, "tn", BF16, "l0_out_dw").reshape(N_DEV, C // N_DEV, D)

    def f_gate0_bwd(r, v):
        dp, v2_, z = r
        g, b = v
        xh, rstd = _ln_stats(v2_)
        v3 = xh * g + b
        dv4 = dp * _silu(z)
        dz = dp * _silu(v3) * _dsilu(z)
        dv3 = dv4 * _dsilu(v3)
        dv2_ = _ln_bwd(dv3 * g, xh, rstd)
        return [dz, dv2_], [_colsum(dv3 * xh), _colsum(dv3), _colsum(dv2_)]
    (dz0, dv2), (dg_cn, db_cn, db_dw) = _rowwise(
        f_gate0_bwd, [(dp0, C, 0, False), (v2, C, 0, False), (u0, C, 2, False)], [vec(g_cn), vec(b_cn)],
        [(BF16, C, C), (F32, C, C)], [C, C, C], n_rows=S, tr=64, name="l0_gate_bwd")
    da0, dg0, dw_dw, r_wbin, got = _conv_bwd(u0, dv2, w_dw, S, C, "l0_conv_bwd",
                                             jobs=[_ChipExchange(t_wbin), _PairExchange(g_wout0)])
    t_wout0 = _pair_add(g_wout0, got, core, "rs_a_w_out_add")
    du0 = jnp.concatenate([da0, dg0, dz0], axis=1)
    nq4 = 4
    qrows = D // nq4
    g_q, t_q, r_q = [], [], []
    r_wout0 = None
    for i in range(nq4):
        jobs = [_ChipExchange(t_wout0)] if i == 0 else []
        if i >= 2:
            jobs.append(_ChipExchange(t_q[i - 2]))
        if i >= 1:
            jobs.append(_PairExchange(g_q[i - 1]))
        res = list(_mm(h0, du0, "tn", BF16, f"l0_in_dw_{i}", out_slots=True, m_rows=(i * qrows, qrows), jobs=jobs))
        g_q.append(res.pop(0))
        if i == 0:
            r_wout0 = res.pop(0)
        if i >= 2:
            r_q.append(res.pop(0))
        if i >= 1:
            t_q.append(_pair_add(g_q[i - 1], res.pop(0), core, f"rs_a_w_in_add_{i - 1}"))
    dh0, r2, got = _mm(du0, win0, "nt", F32, "l0_in_dx", b_slots=True,
                       jobs=[_ChipExchange(t_q[nq4 - 2]), _PairExchange(g_q[nq4 - 1])])
    r_q.append(r2)
    t_q.append(_pair_add(g_q[nq4 - 1], got, core, f"rs_a_w_in_add_{nq4 - 1}"))

    def f_mod0_bwd(r, v):
        xh, rstd = _ln_stats(r[2])
        return [ALPHA * r[0] + _ln_bwd(r[1] * (1.0 + v[0]), xh, rstd)], [_colsum(r[1] * xh), _colsum(r[1])]
    (grad_x,), (dscale0, dshift0), (r3,) = _rowwise(
        f_mod0_bwd, [(dr0, D, 0, False), (dh0, D, 0, False), (x2, D, 0, False)], [vec(scale0)], [(F32, D, D)], [D, D],
        n_rows=S, tr=128, name="l0_modulate_bwd", jobs=[_ChipExchange(t_q[nq4 - 1])])
    r_q.append(r3)
    r_win0 = jnp.concatenate(r_q, axis=1)

    assert C == D and QL <= D and KL <= D
    n_small = 48
    singles = [dlng0, dlng1, dlnb0, dlnb1, dshift0, dscale0, dgate0, dshift1, dscale1, dgate1, db_dw, dg_cn, db_cn, dg_q, dg_kv]
    rows_loc = _pack_rows([(dw_dw, 0)] + [(a, v0 + i) for i, a in enumerate(singles)], n_small, D, "pack_small_grads")
    parts = _all_gather(rows_loc, "ag_small_grads")
    tot = _slot_sum(parts, "small_grads_sum")
    g_ln_g, g_ln_b = tot[v0:v0 + 2], tot[v0 + 2:v0 + 4]
    g_b_ada = tot[v0 + 4:v0 + 10].reshape(DEPTH, 3 * D)
    loc_cols = lambda a: lax.dynamic_slice(a, (0, idx * cl), (a.shape[0], cl))
    g_a_w_dw = loc_cols(tot[0:CONV_KERNEL])
    g_a_b_dw = loc_cols(tot[v0 + 10:v0 + 11])
    g_a_norm_g = loc_cols(tot[v0 + 11:v0 + 12])
    g_a_norm_b = loc_cols(tot[v0 + 12:v0 + 13])
    g_q_norm = tot[v0 + 13:v0 + 14, :QL]
    g_kv_norm = tot[v0 + 14:v0 + 15, :KL]

    dmods = []
    for l in range(DEPTH):
        dmod = parts[:, v0 + 4 + 3 * l:v0 + 7 + 3 * l, :].reshape(N_DEV, 3 * D)
        dmods += [lax.dynamic_slice(dmod, (0, idx * nl), (N_DEV, nl)), jnp.zeros((16 - N_DEV, nl), F32)]
    g_w_ada = _mm(sc2, jnp.concatenate(dmods, axis=0), "tn", F32, "mod_dw").reshape(DEPTH, D, nl)

    r_wqb = r_wqb.reshape(N_CHIP, QL, HL, QPAD)[..., :NOPE + ROPE].reshape(N_CHIP, QL, HL * (NOPE + ROPE))
    r_wkva = r_wkva[..., :KL + ROPE]

    def upd(name, w, gparts, m, v):
        shp = w.shape
        r2 = (1, shp[0]) if len(shp) == 1 else (math.prod(shp[:-1]), shp[-1])
        gp = gparts.reshape((gparts.shape[0],) + r2)
        res = _adamw(w.reshape(r2), gp, m.reshape(r2), v.reshape(r2), "adamw_" + name)
        return [a.reshape(shp) for a in res]

    one = lambda g: g[None]
    results = [
        upd("w_ada", w_ada, one(g_w_ada), m_w_ada, v_w_ada),
        upd("b_ada", b_ada, one(g_b_ada), m_b_ada, v_b_ada),
        upd("ln_g", ln_g, one(g_ln_g), m_ln_g, v_ln_g),
        upd("ln_b", ln_b, one(g_ln_b), m_ln_b, v_ln_b),
        upd("a_w_in", a_w_in, r_win0[:, None], m_a_w_in, v_a_w_in),
        upd("a_w_dw", a_w_dw, one(g_a_w_dw[None]), m_a_w_dw, v_a_w_dw),
        upd("a_b_dw", a_b_dw, one(g_a_b_dw), m_a_b_dw, v_a_b_dw),
        upd("a_norm_g", a_norm_g, one(g_a_norm_g), m_a_norm_g, v_a_norm_g),
        upd("a_norm_b", a_norm_b, one(g_a_norm_b), m_a_norm_b, v_a_norm_b),
        upd("a_w_out", a_w_out, r_wout0[:, None], m_a_w_out, v_a_w_out),
        upd("b_w_in", b_w_in, r_wbin[:, None], m_b_w_in, v_b_w_in),
        upd("b_q_norm_g", b_q_norm_g, one(g_q_norm), m_b_q_norm_g, v_b_q_norm_g),
        upd("b_w_qb", b_w_qb, r_wqb[:, None], m_b_w_qb, v_b_w_qb),
        upd("b_w_out", b_w_out, r_wbout[:, None], m_b_w_out, v_b_w_out),
        upd("kv_w_a", kv_w_a, r_wkva, m_kv_w_a, v_kv_w_a),
        upd("kv_norm_g", kv_norm_g, one(g_kv_norm[0]), m_kv_norm_g, v_kv_norm_g),
        upd("kv_w_b", kv_w_b, r_wkvb, m_kv_w_b, v_kv_w_b),
    ]
    grads, deltas, new_m, new_v = zip(*results)
    return (loss, grad_x[None], *grads, *deltas, *new_m, *new_v)
```

```python
import functools
import math

import jax
import jax.numpy as jnp
from jax import lax
from jax.experimental import pallas as pl
from jax.experimental.pallas import tpu as pltpu

F32 = jnp.float32
BF16 = jnp.bfloat16
MESH = pl.DeviceIdType.MESH
ANY = pl.BlockSpec(memory_space=pl.ANY)

N_DEV = 8
N_CHIP = 4
V7X_VMEM_BYTES = 64 * 1024 * 1024
VMEM_LIMIT = V7X_VMEM_BYTES - 8 * 1024 * 1024
LANES = 128
SUBLANES = 8

DEPTH = 2
CONV_KERNEL = 31
NOPE = 128
ROPE = 64
VDIM = 128
QPAD = NOPE + LANES
ROPE_BASE = 10000.0
LN_EPS = 1e-5
RMS_EPS = 1e-6
ALPHA = (2.0 * DEPTH) ** 0.25
ATTN_SCALE = (NOPE + ROPE) ** -0.5
ADAM_LR = 0.001
ADAM_B1 = 0.9
ADAM_B2 = 0.999
ADAM_EPS = 1e-08
ADAM_WD = 0.01
ADAM_STEP = 10

CONV_TOP = 32
CONV_ROWS = 64
ATTN_BLOCK = 512
ATTN_CHUNK = 32
MM_INPUT_BYTES = 32 * 1024 * 1024
ADAMW_TILE_ELEMS = 256 * 1024


def _pick(dim, prefs):
    for p in prefs:
        if dim % p == 0:
            return p
    return dim


def _params(sem, **kw):
    return pltpu.CompilerParams(dimension_semantics=sem, vmem_limit_bytes=VMEM_LIMIT, **kw)


_DIMS = {"nn": (((1,), (0,)), ((), ())), "nt": (((1,), (1,)), ((), ())), "tn": (((0,), (0,)), ((), ()))}


def _mm(a, b, mode, out_dtype, name, *, b_slots=False, out_slots=False, m_rows=None, jobs=()):
    m_first = 0
    if mode == "tn":
        K, M = a.shape
        if m_rows is not None:
            m_first, M = m_rows
    else:
        M, K = a.shape
    if b_slots:
        assert mode in ("nn", "nt")
        if mode == "nn":
            _, kb, n_loc = b.shape
            N, slot_w = N_DEV * n_loc, n_loc
        else:
            _, N, k_loc = b.shape
            kb, slot_w = N_DEV * k_loc, k_loc
    else:
        if mode == "nt":
            N, kb = b.shape
        else:
            kb, N = b.shape
        slot_w = N // N_DEV if out_slots else None
    assert kb == K, (name, a.shape, b.shape)
    tm = _pick(M, (1024, 512, 256, 128, 64, 32, 16))
    n_unit = slot_w if (mode == "nn" and b_slots) or out_slots else N
    k_unit = slot_w if (mode == "nt" and b_slots) else K
    tn = _pick(n_unit, (1024, 1152, 768, 640, 512, 384, 256, 128))
    step_bytes = 2 * (tm * a.dtype.itemsize + tn * b.dtype.itemsize)
    tk = _pick(k_unit, [t for t in (4096, 2048, 1536, 1152, 1024, 768, 640, 512, 384, 256, 128, 64, 32, 16)
                        if t * step_bytes <= MM_INPUT_BYTES])
    nk = K // tk

    if mode == "tn":
        assert m_first % tm == 0
        a_spec = pl.BlockSpec((tk, tm), lambda i, j, k: (k, i + m_first // tm))
    else:
        a_spec = pl.BlockSpec((tm, tk), lambda i, j, k: (i, k))
    if b_slots and mode == "nn":
        per = slot_w // tn
        b_spec = pl.BlockSpec((None, tk, tn), lambda i, j, k: (j // per, k, j % per))
    elif b_slots and mode == "nt":
        per = slot_w // tk
        b_spec = pl.BlockSpec((None, tn, tk), lambda i, j, k: (k // per, j, k % per))
    elif mode == "nt":
        b_spec = pl.BlockSpec((tn, tk), lambda i, j, k: (j, k))
    else:
        b_spec = pl.BlockSpec((tk, tn), lambda i, j, k: (k, j))
    if out_slots:
        per_o = slot_w // tn
        o_spec = pl.BlockSpec((None, tm, tn), lambda i, j, k: (j // per_o, i, j % per_o))
        o_shape = jax.ShapeDtypeStruct((N_DEV, M, slot_w), out_dtype)
    else:
        o_spec = pl.BlockSpec((tm, tn), lambda i, j, k: (i, j))
        o_shape = jax.ShapeDtypeStruct((M, N), out_dtype)
    dims = _DIMS[mode]

    def step(a_ref, b_ref):
        return lax.dot_general(a_ref[...].astype(BF16), b_ref[...].astype(BF16), dims, preferred_element_type=F32)

    if nk == 1:
        def body(a_ref, b_ref, o_ref):
            o_ref[...] = step(a_ref, b_ref).astype(out_dtype)
    else:
        def body(a_ref, b_ref, o_ref, acc_ref):
            k = pl.program_id(2)

            @pl.when(k == 0)
            def _():
                acc_ref[...] = jnp.zeros_like(acc_ref)

            acc_ref[...] += step(a_ref, b_ref)

            @pl.when(k == nk - 1)
            def _():
                o_ref[...] = acc_ref[...].astype(out_dtype)

    res = _grid_call(body, (a, b), name=name, grid=(M // tm, N // tn, nk), in_specs=[a_spec, b_spec],
                     out_specs=[o_spec], out_shape=[o_shape],
                     scratch_shapes=[] if nk == 1 else [pltpu.VMEM((tm, tn), F32)],
                     sem=("parallel", "parallel", "arbitrary"), jobs=jobs)
    return tuple(res) if jobs else res[0]


def _rowwise(fn, rows, vecs, outs, accs, *, n_rows, tr, name, nc=1, jobs=()):
    nr = n_rows // tr
    n_in, n_vec, n_out, n_acc = len(rows), len(vecs), len(outs), len(accs)

    def cmap(cb0, follow):
        if follow:
            return lambda j, i: (i, cb0 + j)
        return lambda j, i: (i, cb0)

    def vmap_(cb0, follow):
        if follow:
            return lambda j, i: (0, cb0 + j)
        return lambda j, i: (0, cb0)

    in_specs = [pl.BlockSpec((tr, w), cmap(cb0, fol)) for (_, w, cb0, fol) in rows]
    in_specs += [pl.BlockSpec((1, w), vmap_(cb0, fol)) for (_, w, cb0, fol) in vecs]
    out_specs = [pl.BlockSpec((tr, bw), lambda j, i: (i, j)) for (_, _, bw) in outs]
    out_specs += [pl.BlockSpec((1, w), lambda j, i: (0, j)) for w in accs]
    out_shape = [jax.ShapeDtypeStruct((n_rows, tw), dt) for (dt, tw, _) in outs]
    out_shape += [jax.ShapeDtypeStruct((1, w * nc), F32) for w in accs]

    def body(*refs):
        ins = [r[...] for r in refs[:n_in]]
        vs = [r[...] for r in refs[n_in:n_in + n_vec]]
        o_refs = refs[n_in + n_vec:n_in + n_vec + n_out]
        a_refs = refs[n_in + n_vec + n_out:]
        o_vals, a_vals = fn(ins, vs)
        for r, v in zip(o_refs, o_vals):
            r[...] = v.astype(r.dtype)
        if n_acc:
            i = pl.program_id(1)

            @pl.when(i == 0)
            def _():
                for r in a_refs:
                    r[...] = jnp.zeros_like(r)

            for r, v in zip(a_refs, a_vals):
                r[...] += v

    res = _grid_call(
        body, [r[0] for r in rows] + [v[0] for v in vecs], name=name, grid=(nc, nr), in_specs=in_specs,
        out_specs=out_specs, out_shape=out_shape, scratch_shapes=[], sem=("parallel", "arbitrary"), jobs=jobs)
    outs, accs_out, extra = res[:n_out], res[n_out:n_out + n_acc], res[n_out + n_acc:]
    return (outs, accs_out, extra) if jobs else (outs, accs_out)


def _colsum(x):
    return jnp.sum(x, axis=0, keepdims=True)


def _ln_stats(x):
    mu = jnp.mean(x, axis=-1, keepdims=True)
    xc = x - mu
    var = jnp.mean(xc * xc, axis=-1, keepdims=True)
    rstd = lax.rsqrt(var + LN_EPS)
    return xc * rstd, rstd


def _ln_bwd(dxhat, xhat, rstd):
    return rstd * (dxhat - jnp.mean(dxhat, axis=-1, keepdims=True)
                   - xhat * jnp.mean(dxhat * xhat, axis=-1, keepdims=True))


def _rms_stats(x):
    r = lax.rsqrt(jnp.mean(x * x, axis=-1, keepdims=True) + RMS_EPS)
    return x * r, r


def _rms_bwd(dxhat, xhat, r):
    return r * (dxhat - xhat * jnp.mean(dxhat * xhat, axis=-1, keepdims=True))


def _silu(x):
    return x * jax.nn.sigmoid(x)


def _dsilu(x):
    s = jax.nn.sigmoid(x)
    return s * (1.0 + x * (1.0 - s))


def _rope_fwd(x, cos_t, sin_a, sin_b):
    return x * cos_t + pltpu.roll(x, LANES - ROPE // 2, 1) * sin_a + pltpu.roll(x, ROPE // 2, 1) * sin_b


def _rope_bwd(dy, cos_t, sin_a, sin_b):
    return dy * cos_t + pltpu.roll(dy * sin_a, ROPE // 2, 1) + pltpu.roll(dy * sin_b, LANES - ROPE // 2, 1)


def _rope1(x, tab):
    lane = lax.broadcasted_iota(jnp.int32, x.shape, 1)
    x = jnp.where(lane < ROPE, x, 0.0)
    swapped = pltpu.roll(x, LANES - ROPE // 2, 1) + pltpu.roll(x, ROPE // 2, 1)
    return jnp.where(lane < ROPE, x * tab + swapped * pltpu.roll(tab, ROPE, 1), 0.0)


def _rope1_bwd(dy, tab):
    lane = lax.broadcasted_iota(jnp.int32, dy.shape, 1)
    dy = jnp.where(lane < ROPE, dy, 0.0)
    g = dy * pltpu.roll(tab, ROPE, 1)
    return jnp.where(lane < ROPE, dy * tab + pltpu.roll(g, LANES - ROPE // 2, 1) + pltpu.roll(g, ROPE // 2, 1), 0.0)


def _rope_tables(seq):
    half = ROPE // 2
    inv_freq = ROPE_BASE ** (-jnp.arange(half, dtype=F32) / half)
    ang = jnp.arange(seq, dtype=jnp.int32).astype(F32)[:, None] * inv_freq[None, :]
    cos, sin = jnp.cos(ang), jnp.sin(ang)
    z = jnp.zeros_like(cos)
    cos_t = jnp.concatenate([cos, cos, z, z], axis=1)
    sin_a = jnp.concatenate([-sin, z, z, z], axis=1)
    sin_b = jnp.concatenate([z, sin, z, z], axis=1)
    return cos_t, sin_a, sin_b


def _shifted(win, n):
    return [win] + [pltpu.roll(win, n - j, 0) for j in range(1, SUBLANES)]


def _conv_fwd(u0, w_dw, b_dw, seq, width, name, jobs=()):
    tc = LANES
    nct = width // tc
    rows = CONV_ROWS
    n_win = rows + CONV_TOP

    def body(a_ref, g_ref, w_ref, b_ref, o_ref, vpad):
        vpad[0:CONV_TOP, :] = jnp.zeros((CONV_TOP, tc), F32)
        vpad[CONV_TOP:, :] = a_ref[...] * jax.nn.sigmoid(g_ref[...])

        def chunk(i, carry):
            t0 = pl.multiple_of(i * rows, rows)
            rot = _shifted(vpad[pl.ds(t0, n_win), :], n_win)
            acc = jnp.zeros((rows, tc), F32) + b_ref[...]
            for k in range(CONV_KERNEL):
                s = k + CONV_TOP - (CONV_KERNEL - 1)
                q, j = divmod(s, SUBLANES)
                acc = acc + w_ref[k:k + 1, :] * rot[j][SUBLANES * q:SUBLANES * q + rows]
            o_ref[pl.ds(t0, rows), :] = acc
            return carry

        lax.fori_loop(0, seq // rows, chunk, 0)

    return _grid_call(
        body, (u0, u0, w_dw, b_dw), name=name, grid=(nct,),
        in_specs=[pl.BlockSpec((seq, tc), lambda j: (0, j)), pl.BlockSpec((seq, tc), lambda j: (0, nct + j)),
                  pl.BlockSpec((CONV_KERNEL, tc), lambda j: (0, j)), pl.BlockSpec((1, tc), lambda j: (0, j))],
        out_specs=[pl.BlockSpec((seq, tc), lambda j: (0, j))],
        out_shape=[jax.ShapeDtypeStruct((seq, width), F32)],
        scratch_shapes=[pltpu.VMEM((seq + CONV_TOP, tc), F32)],
        sem=("parallel",), jobs=jobs)


def _conv_bwd(u0, dv2, w_dw, seq, width, name, jobs=()):
    tc = LANES
    nct = width // tc
    rows = CONV_ROWS
    n_win = rows + CONV_TOP
    nsh = CONV_KERNEL * SUBLANES

    def body(a_ref, g_ref, d_ref, w_ref, da_ref, dg_ref, dw_ref, vpad, dpad, dwacc):
        vpad[0:CONV_TOP, :] = jnp.zeros((CONV_TOP, tc), F32)
        vpad[CONV_TOP:, :] = a_ref[...] * jax.nn.sigmoid(g_ref[...])
        dpad[0:seq, :] = d_ref[...]
        dpad[seq:, :] = jnp.zeros((CONV_TOP, tc), F32)
        dwacc[...] = jnp.zeros_like(dwacc)

        def chunk(i, carry):
            t0 = pl.multiple_of(i * rows, rows)
            vrot = _shifted(vpad[pl.ds(t0, n_win), :], n_win)
            drot = _shifted(dpad[pl.ds(t0, n_win), :], n_win)
            dcur = drot[0][0:rows]
            dv1 = jnp.zeros((rows, tc), F32)
            for k in range(CONV_KERNEL):
                q, j = divmod(CONV_KERNEL - 1 - k, SUBLANES)
                dv1 = dv1 + w_ref[k:k + 1, :] * drot[j][SUBLANES * q:SUBLANES * q + rows]
                q, j = divmod(k + CONV_TOP - (CONV_KERNEL - 1), SUBLANES)
                prod = dcur * vrot[j][SUBLANES * q:SUBLANES * q + rows]
                part = prod[0:SUBLANES]
                for r in range(1, rows // SUBLANES):
                    part = part + prod[SUBLANES * r:SUBLANES * (r + 1)]
                dwacc[SUBLANES * k:SUBLANES * (k + 1), :] += part
            a = a_ref[pl.ds(t0, rows), :]
            sg = jax.nn.sigmoid(g_ref[pl.ds(t0, rows), :])
            da_ref[pl.ds(t0, rows), :] = (dv1 * sg).astype(BF16)
            dg_ref[pl.ds(t0, rows), :] = (dv1 * a * sg * (1.0 - sg)).astype(BF16)
            return carry

        lax.fori_loop(0, seq // rows, chunk, 0)
        for k in range(CONV_KERNEL):
            dw_ref[k:k + 1, :] = jnp.sum(dwacc[SUBLANES * k:SUBLANES * (k + 1), :], axis=0, keepdims=True)

    return _grid_call(
        body, (u0, u0, dv2, w_dw), name=name, grid=(nct,),
        in_specs=[pl.BlockSpec((seq, tc), lambda j: (0, j)), pl.BlockSpec((seq, tc), lambda j: (0, nct + j)),
                  pl.BlockSpec((seq, tc), lambda j: (0, j)), pl.BlockSpec((CONV_KERNEL, tc), lambda j: (0, j))],
        out_specs=[pl.BlockSpec((seq, tc), lambda j: (0, j)), pl.BlockSpec((seq, tc), lambda j: (0, j)),
                   pl.BlockSpec((CONV_KERNEL, tc), lambda j: (0, j))],
        out_shape=[jax.ShapeDtypeStruct((seq, width), BF16), jax.ShapeDtypeStruct((seq, width), BF16),
                   jax.ShapeDtypeStruct((CONV_KERNEL, width), F32)],
        scratch_shapes=[pltpu.VMEM((seq + CONV_TOP, tc), F32), pltpu.VMEM((seq + CONV_TOP, tc), F32),
                        pltpu.VMEM((nsh, tc), F32)],
        sem=("parallel",), jobs=jobs)


def _lanes(x, n):
    return jnp.tile(x, (1, n // LANES))


def _diag_mask(r0, rows, t, transposed=False):
    r = lax.broadcasted_iota(jnp.int32, (rows, t), 0) + r0
    c = lax.broadcasted_iota(jnp.int32, (rows, t), 1)
    return (c >= r) if transposed else (r >= c)


def _attn_fwd(qf, tab, kv, kr, u1, z_block0, seq, heads, name, jobs=()):
    t = min(ATTN_BLOCK, seq)
    nb = seq // t
    ch = ATTN_CHUNK

    def body(qf_ref, tab_ref, kv_ref, kr_ref, z_ref, o_ref, lse_ref, qb_ref, g_ref, kf_s, q_s, sa_s, sb_s, p_s, m_s, l_s, c_s,
             acc_s):
        qi = pl.program_id(1)

        @pl.when(qi == 0)
        def _():
            kf_s[:, 0:NOPE] = kv_ref[:, 0:NOPE]
            kf_s[:, NOPE:] = kr_ref[...]

        q_s[:, 0:NOPE] = (qf_ref[:, 0:NOPE] * ATTN_SCALE).astype(BF16)
        q_s[:, NOPE:] = (_rope1(qf_ref[:, NOPE:], tab_ref[...]) * ATTN_SCALE).astype(BF16)
        qb_ref[...] = q_s[...]
        m_s[...] = jnp.full_like(m_s, -jnp.inf)
        l_s[...] = jnp.zeros_like(l_s)
        acc_s[...] = jnp.zeros_like(acc_s)

        def scores(ki, s_s):
            k0 = pl.multiple_of(ki * t, t)
            s_s[...] = lax.dot_general(q_s[...], kf_s[pl.ds(k0, t), :], _DIMS["nt"], preferred_element_type=F32)

        def consume(ki, s_s, masked):
            k0 = pl.multiple_of(ki * t, t)

            def chunk(c, carry):
                r0 = pl.multiple_of(c * ch, ch)
                rows = pl.ds(r0, ch)
                s_c = s_s[rows, :]
                if masked:
                    s_c = jnp.where(_diag_mask(r0, ch, t), s_c, -jnp.inf)
                m_prev = m_s[rows, :]
                m_new = jnp.maximum(m_prev, jnp.max(s_c, axis=1, keepdims=True))
                p = jnp.exp(s_c - _lanes(m_new, t))
                corr = jnp.exp(m_prev - m_new)
                l_s[rows, :] = corr * l_s[rows, :] + jnp.sum(p, axis=1, keepdims=True)
                m_s[rows, :] = m_new
                c_s[rows, :] = corr
                p_s[rows, :] = p.astype(BF16)
                return carry

            lax.fori_loop(0, t // ch, chunk, 0, unroll=True)
            acc_s[...] = c_s[...] * acc_s[...] + jnp.dot(p_s[...], kv_ref[pl.ds(k0, t), pl.ds(NOPE, VDIM)],
                                                         preferred_element_type=F32)

        def pair(j, carry):
            scores(2 * j + 1, sb_s)
            consume(2 * j, sa_s, False)
            scores(2 * j + 2, sa_s)
            consume(2 * j + 1, sb_s, False)
            return carry

        scores(0, sa_s)
        lax.fori_loop(0, qi // 2, pair, 0)

        @pl.when(qi % 2 == 0)
        def _():
            consume(qi, sa_s, True)

        @pl.when(qi % 2 == 1)
        def _():
            scores(qi, sb_s)
            consume(qi - 1, sa_s, False)
            consume(qi, sb_s, True)

        o = acc_s[...] / l_s[...]
        o_ref[...] = o
        g_ref[...] = (o * _silu(z_ref[...])).astype(BF16)
        lse_ref[...] = m_s[...] + jnp.log(l_s[...])

    return _grid_call(
        body, (qf, tab, kv, kr, u1), name=name, grid=(heads, nb),
        in_specs=[pl.BlockSpec((t, QPAD), lambda h, i: (i, h)),
                  pl.BlockSpec((t, LANES), lambda h, i: (i, 0)),
                  pl.BlockSpec((seq, QPAD), lambda h, i: (0, h)),
                  pl.BlockSpec((seq, LANES), lambda h, i: (0, 0)),
                  pl.BlockSpec((t, VDIM), lambda h, i: (i, z_block0 + h))],
        out_specs=[pl.BlockSpec((t, VDIM), lambda h, i: (i, h)),
                   pl.BlockSpec((t, LANES), lambda h, i: (i, h)),
                   pl.BlockSpec((t, QPAD), lambda h, i: (i, h)),
                   pl.BlockSpec((t, VDIM), lambda h, i: (i, h))],
        out_shape=[jax.ShapeDtypeStruct((seq, heads * VDIM), F32), jax.ShapeDtypeStruct((seq, heads * LANES), F32),
                   jax.ShapeDtypeStruct((seq, heads * QPAD), BF16), jax.ShapeDtypeStruct((seq, heads * VDIM), BF16)],
        scratch_shapes=[pltpu.VMEM((seq, QPAD), BF16), pltpu.VMEM((t, QPAD), BF16), pltpu.VMEM((t, t), F32),
                        pltpu.VMEM((t, t), F32), pltpu.VMEM((t, t), BF16), pltpu.VMEM((t, LANES), F32),
                        pltpu.VMEM((t, LANES), F32), pltpu.VMEM((t, LANES), F32), pltpu.VMEM((t, VDIM), F32)],
        sem=("parallel", "arbitrary"), jobs=jobs)


def _attn_bwd(q, tab, kv, kr, do, o, lse, seq, heads, name, jobs=()):
    t = min(ATTN_BLOCK, seq)
    nb = seq // t
    ch = ATTN_CHUNK

    def body(q_ref, tab_ref, kv_ref, kr_ref, do_ref, o_ref, lse_ref, dqb_ref, dkv_ref, dkr_ref,
             dq_ref, kf_s, qt_s, dot_s, dl_s, s_s, dp_s, p_s, ds_s, dkt_s, dvt_s):
        @pl.when(pl.program_id(0) == 0)
        def _():
            dkr_ref[...] = jnp.zeros_like(dkr_ref)

        kf_s[:, 0:NOPE] = kv_ref[:, 0:NOPE]
        kf_s[:, NOPE:] = kr_ref[...]
        dq_ref[...] = jnp.zeros_like(dq_ref)
        for b in range(nb):
            rows = pl.ds(b * t, t)
            dob = do_ref[rows, :].astype(F32)
            qt_s[b] = jnp.transpose(q_ref[rows, :].astype(F32)).astype(BF16)
            dot_s[b] = jnp.transpose(dob).astype(BF16)
            dl_s[rows, :] = jnp.broadcast_to(jnp.sum(dob * o_ref[rows, :], axis=1, keepdims=True), (t, LANES))

        def key_block(ki, carry):
            k0 = pl.multiple_of(ki * t, t)
            kfb = kf_s[pl.ds(k0, t), :]
            dkt_s[...] = jnp.zeros_like(dkt_s)
            dvt_s[...] = jnp.zeros_like(dvt_s)

            def scores(qi, s_s, dp_s):
                q0 = pl.multiple_of(qi * t, t)
                s_s[...] = lax.dot_general(q_ref[pl.ds(q0, t), :], kfb, _DIMS["nt"], preferred_element_type=F32)
                dp_s[...] = lax.dot_general(do_ref[pl.ds(q0, t), :], kv_ref[pl.ds(k0, t), pl.ds(NOPE, VDIM)],
                                            _DIMS["nt"], preferred_element_type=F32)

            def consume(qi, s_s, dp_s, masked):
                q0 = pl.multiple_of(qi * t, t)

                def chunk(c, carry):
                    r0 = pl.multiple_of(c * ch, ch)
                    rows = pl.ds(r0, ch)
                    grows = pl.ds(pl.multiple_of(q0 + r0, ch), ch)
                    p = jnp.exp(s_s[rows, :] - _lanes(lse_ref[grows, :], t))
                    if masked:
                        p = jnp.where(_diag_mask(r0, ch, t), p, 0.0)
                    p_s[rows, :] = p.astype(BF16)
                    ds_s[rows, :] = (p * (dp_s[rows, :] - _lanes(dl_s[grows, :], t))).astype(BF16)
                    return carry

                lax.fori_loop(0, t // ch, chunk, 0, unroll=True)
                dq_ref[pl.ds(q0, t), :] += jnp.dot(ds_s[...], kfb, preferred_element_type=F32)
                dvt_s[...] += jnp.dot(dot_s[qi], p_s[...], preferred_element_type=F32)
                dkt_s[...] += jnp.dot(qt_s[qi], ds_s[...], preferred_element_type=F32)

            def block(qi, masked):
                scores(qi, s_s, dp_s)
                consume(qi, s_s, dp_s, masked)

            def above(qi, carry):
                block(qi, False)
                return carry

            block(ki, True)
            lax.fori_loop(ki + 1, nb, above, 0)
            dk = jnp.transpose(dkt_s[...])
            dkv_ref[pl.ds(k0, t), :] = jnp.concatenate([dk[:, :NOPE], jnp.transpose(dvt_s[...])], axis=1).astype(BF16)
            dkr_ref[pl.ds(k0, t), :] += dk[:, NOPE:]
            return carry

        lax.fori_loop(0, nb, key_block, 0)
        for b in range(nb):
            rows = pl.ds(b * t, t)
            dqb_ref[rows, 0:NOPE] = (dq_ref[rows, 0:NOPE] * ATTN_SCALE).astype(BF16)
            dqb_ref[rows, NOPE:] = (_rope1_bwd(dq_ref[rows, NOPE:], tab_ref[rows, :]) * ATTN_SCALE).astype(BF16)

    head = lambda w: pl.BlockSpec((seq, w), lambda h: (0, h))
    shared = pl.BlockSpec((seq, LANES), lambda h: (0, 0))
    return _grid_call(
        body, (q, tab, kv, kr, do, o, lse), name=name, grid=(heads,),
        in_specs=[head(QPAD), shared, head(QPAD), shared, head(VDIM), head(VDIM), head(LANES)],
        out_specs=[head(QPAD), head(QPAD), shared],
        out_shape=[jax.ShapeDtypeStruct((seq, heads * QPAD), BF16), jax.ShapeDtypeStruct((seq, heads * QPAD), BF16),
                   jax.ShapeDtypeStruct((seq, LANES), F32)],
        scratch_shapes=[pltpu.VMEM((seq, QPAD), F32),
                        pltpu.VMEM((seq, QPAD), BF16), pltpu.VMEM((nb, QPAD, t), BF16), pltpu.VMEM((nb, VDIM, t), BF16),
                        pltpu.VMEM((seq, LANES), F32), pltpu.VMEM((t, t), F32), pltpu.VMEM((t, t), F32),
                        pltpu.VMEM((t, t), BF16), pltpu.VMEM((t, t), BF16), pltpu.VMEM((QPAD, t), F32),
                        pltpu.VMEM((VDIM, t), F32)],
        sem=("arbitrary",), jobs=jobs)


def _place():
    x, y, c = lax.axis_index("x"), lax.axis_index("y"), lax.axis_index("c")
    return x, y, c


class _Gather:
    def __init__(self, shard):
        r, n = shard.shape
        self.ins = [shard]
        self.outs = [jax.ShapeDtypeStruct((N_DEV, r, n), shard.dtype)]
        self.sems = [pltpu.SemaphoreType.DMA((7,)), pltpu.SemaphoreType.DMA((7,)), pltpu.SemaphoreType.DMA]

    def bind(self, ins, outs, sems):
        (self.x_ref,), (self.out_ref,) = ins, outs
        self.send, self.recv, self.local = sems

    def _plan(self):
        x, y, c = _place()
        chips = [(1 - x, y), (x, 1 - y), (1 - x, 1 - y)]

        def copy(k, block, to, from_shard=False):
            px, py, pc = block
            slot = self.out_ref.at[4 * px + 2 * py + pc]
            return pltpu.make_async_remote_copy(
                src_ref=self.x_ref if from_shard else slot, dst_ref=slot, send_sem=self.send.at[k],
                recv_sem=self.recv.at[k], device_id=to, device_id_type=MESH)

        me, sibling = (x, y, c), (x, y, 1 - c)
        mine = pltpu.make_async_copy(self.x_ref, self.out_ref.at[4 * x + 2 * y + c], self.local)
        first = [copy(0, me, sibling, True)] + [copy(1 + j, me, (*chip, c), True) for j, chip in enumerate(chips)]
        landed = [copy(1 + j, (*chip, c), me) for j, chip in enumerate(chips)]
        passed = [copy(4 + j, (*chip, c), sibling) for j, chip in enumerate(chips)]
        from_sibling = [copy(0, sibling, me)] + [copy(4 + j, (*chip, 1 - c), me) for j, chip in enumerate(chips)]
        return mine, first, landed, passed, from_sibling

    def start(self):
        mine, first, _, _, _ = self._plan()
        mine.start()
        for cp in first:
            cp.start()

    def mid(self):
        _, _, landed, passed, _ = self._plan()
        for got, fwd in zip(landed, passed):
            got.wait_recv()
            fwd.start()

    def finish(self):
        mine, first, _, passed, from_sibling = self._plan()
        for cp in from_sibling:
            cp.wait_recv()
        for cp in first + passed:
            cp.wait_send()
        mine.wait()


class _PairExchange:
    def __init__(self, g8):
        _, r, n = g8.shape
        self.ins = [g8]
        self.outs = [jax.ShapeDtypeStruct((N_CHIP, r, n), g8.dtype)]
        self.sems = [pltpu.SemaphoreType.DMA((N_CHIP,)), pltpu.SemaphoreType.DMA((N_CHIP,))]

    def bind(self, ins, outs, sems):
        (self.g_ref,), (self.out_ref,) = ins, outs
        self.send, self.recv = sems

    def _plan(self):
        x, y, c = _place()
        return [pltpu.make_async_remote_copy(
            src_ref=self.g_ref.at[2 * p + (1 - c)], dst_ref=self.out_ref.at[p], send_sem=self.send.at[p],
            recv_sem=self.recv.at[p], device_id=(x, y, 1 - c), device_id_type=MESH) for p in range(N_CHIP)]

    def start(self):
        for cp in self._plan():
            cp.start()

    def mid(self):
        pass

    def finish(self):
        copies = self._plan()
        for cp in copies:
            cp.wait_recv()
        for cp in copies:
            cp.wait_send()


class _ChipExchange:
    def __init__(self, t4):
        _, r, n = t4.shape
        self.ins = [t4]
        self.outs = [jax.ShapeDtypeStruct((N_CHIP, r, n), t4.dtype)]
        self.sems = [pltpu.SemaphoreType.DMA((3,)), pltpu.SemaphoreType.DMA((3,)), pltpu.SemaphoreType.DMA]

    def bind(self, ins, outs, sems):
        (self.t_ref,), (self.out_ref,) = ins, outs
        self.send, self.recv, self.local_sem = sems

    def _plan(self):
        x, y, c = _place()
        mine = 2 * x + y
        chips = [(1 - x, y), (x, 1 - y), (1 - x, 1 - y)]
        local = pltpu.make_async_copy(self.t_ref.at[mine], self.out_ref.at[mine], self.local_sem)
        sends = [pltpu.make_async_remote_copy(
            src_ref=self.t_ref.at[2 * px + py], dst_ref=self.out_ref.at[mine], send_sem=self.send.at[k],
            recv_sem=self.recv.at[k], device_id=(px, py, c), device_id_type=MESH) for k, (px, py) in enumerate(chips)]
        arrivals = [pltpu.make_async_remote_copy(
            src_ref=self.t_ref.at[mine], dst_ref=self.out_ref.at[2 * px + py], send_sem=self.send.at[k],
            recv_sem=self.recv.at[k], device_id=(px, py, c), device_id_type=MESH) for k, (px, py) in enumerate(chips)]
        return local, sends, arrivals

    def start(self):
        local, sends, _ = self._plan()
        local.start()
        for cp in sends:
            cp.start()

    def mid(self):
        pass

    def finish(self):
        local, sends, arrivals = self._plan()
        for cp in arrivals:
            cp.wait_recv()
        for cp in sends:
            cp.wait_send()
        local.wait()


def _job_specs(jobs):
    ins = [a for j in jobs for a in j.ins]
    outs = [o for j in jobs for o in j.outs]
    sems = [s for j in jobs for s in j.sems]
    return ins, outs, sems


def _bind_jobs(jobs, ins, outs, sems):
    for j in jobs:
        ni, no, ns = len(j.ins), len(j.outs), len(j.sems)
        j.bind(ins[:ni], outs[:no], sems[:ns])
        ins, outs, sems = ins[ni:], outs[no:], sems[ns:]


def _carry(body, jobs, n_in, n_out, grid):
    if not jobs:
        return body
    j_in, j_out, j_sem = (len(v) for v in _job_specs(jobs))
    total = math.prod(grid)
    mid_step = (3 * total) // 4

    def wrapped(*refs):
        ins, rest = refs[:n_in], refs[n_in:]
        jin, rest = rest[:j_in], rest[j_in:]
        outs, rest = rest[:n_out], rest[n_out:]
        jout, rest = rest[:j_out], rest[j_out:]
        scratch, jsem = rest[:len(rest) - j_sem], rest[len(rest) - j_sem:]
        _bind_jobs(jobs, jin, jout, jsem)
        lin = 0
        for d, size in enumerate(grid):
            lin = lin * size + pl.program_id(d)

        @pl.when(lin == 0)
        def _():
            for j in jobs:
                j.start()

        body(*ins, *outs, *scratch)

        @pl.when(lin == mid_step)
        def _():
            for j in jobs:
                j.mid()

        @pl.when(lin == total - 1)
        def _():
            for j in jobs:
                j.finish()

    return wrapped


def _carry_call(jobs):
    ins, outs, sems = _job_specs(jobs)
    return ins, [ANY] * len(ins), [ANY] * len(outs), outs, sems


def _grid_call(body, operands, *, name, grid, in_specs, out_specs, out_shape, scratch_shapes, sem, jobs=()):
    j_ins, j_in_specs, j_out_specs, j_outs, j_sems = _carry_call(jobs)
    return pl.pallas_call(
        _carry(body, jobs, len(in_specs), len(out_specs), grid), name=name, grid=grid,
        in_specs=list(in_specs) + j_in_specs, out_specs=list(out_specs) + j_out_specs,
        out_shape=list(out_shape) + j_outs, scratch_shapes=list(scratch_shapes) + j_sems,
        compiler_params=_params(("arbitrary",) * len(grid) if jobs else sem, has_side_effects=bool(jobs)),
    )(*operands, *j_ins)


def _comm(jobs, name):
    ins, outs, sems = _job_specs(jobs)

    def body(*refs):
        _bind_jobs(jobs, refs[:len(ins)], refs[len(ins):len(ins) + len(outs)], refs[len(ins) + len(outs):])
        for j in jobs:
            j.start()
        for j in jobs:
            j.mid()
        for j in jobs:
            j.finish()

    return pl.pallas_call(
        body, name=name, in_specs=[ANY] * len(ins), out_specs=[ANY] * len(outs), out_shape=outs,
        scratch_shapes=sems, compiler_params=pltpu.CompilerParams(has_side_effects=True),
    )(*ins)


def _all_gather(shard, name):
    return _comm([_Gather(shard)], name)[0]


def _pair_add(g8, got, core, name):
    _, r, n = g8.shape
    tr = _pick(r, (256, 128, 64, 32, 16, 8))

    def body(c_ref, g_ref, r_ref, o_ref):
        o_ref[...] = (g_ref[...].astype(F32) + r_ref[...].astype(F32)).astype(o_ref.dtype)

    grid_spec = pltpu.PrefetchScalarGridSpec(
        num_scalar_prefetch=1, grid=(N_CHIP, r // tr),
        in_specs=[pl.BlockSpec((None, tr, n), lambda p, i, c: (2 * p + c[0], i, 0)),
                  pl.BlockSpec((None, tr, n), lambda p, i, c: (p, i, 0))],
        out_specs=pl.BlockSpec((None, tr, n), lambda p, i, c: (p, i, 0)))
    return pl.pallas_call(
        body, name=name, grid_spec=grid_spec, out_shape=jax.ShapeDtypeStruct((N_CHIP, r, n), g8.dtype),
        compiler_params=_params(("parallel", "parallel")),
    )(core, g8, got)


def _slot_sum(g, name):
    k, r, n = g.shape

    def body(g_ref, o_ref):
        acc = g_ref[0].astype(F32)
        for s in range(1, k):
            acc = acc + g_ref[s].astype(F32)
        o_ref[...] = acc

    return pl.pallas_call(body, name=name, out_shape=jax.ShapeDtypeStruct((r, n), F32),
                          compiler_params=pltpu.CompilerParams(vmem_limit_bytes=VMEM_LIMIT))(g)


def _pack_rows(parts, n_rows, width, name):
    starts = [r0 for _, r0 in parts]

    def body(*refs):
        o_ref = refs[-1]
        o_ref[...] = jnp.zeros_like(o_ref)
        for ref, r0 in zip(refs[:-1], starts):
            k, w = ref.shape
            o_ref[r0:r0 + k, 0:w] = ref[...]

    return pl.pallas_call(body, name=name, out_shape=jax.ShapeDtypeStruct((n_rows, width), F32),
                          compiler_params=pltpu.CompilerParams(vmem_limit_bytes=VMEM_LIMIT))(*[a for a, _ in parts])


def _adamw(w, gparts, m, v, name):
    r, n = w.shape
    k = gparts.shape[0]
    tr = _pick(r, [t for t in (512, 256, 128, 64, 32, 16) if t * n <= ADAMW_TILE_ELEMS])
    c1 = 1.0 - ADAM_B1 ** ADAM_STEP
    c2 = 1.0 - ADAM_B2 ** ADAM_STEP

    def body(w_ref, g_ref, m_ref, v_ref, go_ref, d_ref, mo_ref, vo_ref):
        g = g_ref[0].astype(F32)
        for s in range(1, k):
            g = g + g_ref[s].astype(F32)
        mn = ADAM_B1 * m_ref[...] + (1.0 - ADAM_B1) * g
        vn = ADAM_B2 * v_ref[...] + (1.0 - ADAM_B2) * (g * g)
        go_ref[...] = g
        mo_ref[...] = mn
        vo_ref[...] = vn
        d_ref[...] = -ADAM_LR * ((mn / c1) / (jnp.sqrt(vn / c2) + ADAM_EPS) + ADAM_WD * w_ref[...])

    spec = pl.BlockSpec((tr, n), lambda i: (i, 0))
    return pl.pallas_call(
        body, name=name, grid=(r // tr,),
        in_specs=[spec, pl.BlockSpec((k, tr, n), lambda i: (0, i, 0)), spec, spec],
        out_specs=[spec] * 4, out_shape=[jax.ShapeDtypeStruct((r, n), F32)] * 4,
        compiler_params=_params(("parallel",)),
    )(w, gparts, m, v)


def kernel(x, c, w_ada, b_ada, ln_g, ln_b, a_w_in, a_w_dw, a_b_dw, a_norm_g, a_norm_b, a_w_out, b_w_in, b_q_norm_g, b_w_qb, b_w_out, kv_w_a, kv_norm_g, kv_w_b, loss_target, m_w_ada, m_b_ada, m_ln_g, m_ln_b, m_a_w_in, m_a_w_dw, m_a_b_dw, m_a_norm_g, m_a_norm_b, m_a_w_out, m_b_w_in, m_b_q_norm_g, m_b_w_qb, m_b_w_out, m_kv_w_a, m_kv_norm_g, m_kv_w_b, v_w_ada, v_b_ada, v_ln_g, v_ln_b, v_a_w_in, v_a_w_dw, v_a_b_dw, v_a_norm_g, v_a_norm_b, v_a_w_out, v_b_w_in, v_b_q_norm_g, v_b_w_qb, v_b_w_out, v_kv_w_a, v_kv_norm_g, v_kv_w_b):
    S, D = x.shape[1], x.shape[2]
    C = a_w_out.shape[1] * N_DEV
    QL = b_q_norm_g.shape[1]
    KL = kv_norm_g.shape[0]
    H = kv_w_b.shape[1] * N_DEV // (NOPE + VDIM)
    HL = H // N_DEV
    W = H * VDIM
    assert a_w_dw.shape[1] == CONV_KERNEL and kv_w_a.shape[1] == KL + ROPE and b_w_qb.shape[2] == HL * (NOPE + ROPE)
    assert b_w_in.shape[2] * N_DEV == QL + W and D % (N_DEV * LANES) == 0 and S % CONV_ROWS == 0
    TW = 1024 if W % 1024 == 0 and QL % 1024 == 0 else LANES

    xi, yi, ci = _place()
    idx = 4 * xi + 2 * yi + ci
    core = jnp.reshape(ci, (1,)).astype(jnp.int32)
    x2, tgt = x[0], loss_target[0]
    cos_t, sin_a, sin_b = _rope_tables(S)
    rope_rows = [(cos_t, LANES, 0, False), (sin_a, LANES, 0, False), (sin_b, LANES, 0, False)]

    win0 = _all_gather(a_w_in[0].astype(BF16), "ag_a_w_in")
    sh_wout0 = a_w_out[0].astype(BF16)
    sh_wbin = b_w_in[0].astype(BF16)
    wqb_loc = jnp.pad(b_w_qb[0].reshape(QL, HL, NOPE + ROPE), ((0, 0), (0, 0), (0, QPAD - NOPE - ROPE)))
    sh_wqb = wqb_loc.reshape(QL, HL * QPAD).astype(BF16)
    sh_wbout = b_w_out[0].astype(BF16)
    kva_w = KL + LANES
    sh_wkva = jnp.pad(kv_w_a, ((0, 0), (0, kva_w - KL - ROPE))).astype(BF16)
    sh_wkvb = kv_w_b.astype(BF16)

    cl = C // N_DEV
    v0 = CONV_TOP
    small_loc = _pack_rows([(a_w_dw[0], 0), (a_b_dw, v0), (a_norm_g, v0 + 1), (a_norm_b, v0 + 2)], 40, cl, "pack_conv_small")
    small = _all_gather(small_loc, "ag_conv_small")
    small = jnp.transpose(small, (1, 0, 2)).reshape(40, C)
    w_dw, b_dw, g_cn, b_cn = small[:CONV_KERNEL], small[v0:v0 + 1], small[v0 + 1:v0 + 2], small[v0 + 2:v0 + 3]

    c_all = _all_gather(jnp.broadcast_to(c, (SUBLANES, D)), "ag_c")[:, 0, :]
    c_pad = jnp.concatenate([c_all, jnp.zeros((16 - N_DEV, D), F32)], axis=0)
    (sc,), _ = _rowwise(lambda r, v: ([_silu(r[0])], []), [(c_pad, D, 0, False)], [], [(BF16, D, D)], [],
                        n_rows=16, tr=16, name="silu_c")
    nl = 3 * D // N_DEV
    zsc = jnp.zeros_like(sc)
    sc2 = jnp.concatenate([jnp.concatenate([sc, zsc], axis=1), jnp.concatenate([zsc, sc], axis=1)], axis=0)
    mod_loc = _mm(sc2, w_ada.reshape(DEPTH * D, nl), "nn", F32, "mod")
    mod_all = jnp.transpose(_all_gather(mod_loc, "ag_mod"), (1, 0, 2)).reshape(DEPTH, 16, 3 * D)
    mods = []
    for l in range(DEPTH):
        row = lax.dynamic_slice(mod_all[l], (idx, 0), (1, 3 * D)) + b_ada[l][None, :]
        mods.append((row[:, :D], row[:, D:2 * D], row[:, 2 * D:]))
    (shift0, scale0, gate0), (shift1, scale1, gate1) = mods
    vec = lambda a, w=None: (a, a.shape[1] if w is None else w, 0, False)

    def f_mod0(r, v):
        xh, _ = _ln_stats(r[0])
        return [xh * (1.0 + v[0]) + v[1]], []
    (h0,), _ = _rowwise(f_mod0, [(x2, D, 0, False)], [vec(scale0), vec(shift0)], [(BF16, D, D)], [],
                        n_rows=S, tr=128, name="l0_modulate")
    u0, wbin = _mm(h0, win0, "nn", F32, "l0_in", b_slots=True, jobs=[_Gather(sh_wbin)])
    v2, wout0 = _conv_fwd(u0, w_dw, b_dw, S, C, "l0_conv", jobs=[_Gather(sh_wout0)])
    wout0 = wout0.reshape(C, D)

    def f_gate0(r, v):
        xh, _ = _ln_stats(r[0])
        return [_silu(xh * v[0] + v[1]) * _silu(r[1])], []
    (p0,), _ = _rowwise(f_gate0, [(v2, C, 0, False), (u0, C, 2, False)], [vec(g_cn), vec(b_cn)], [(BF16, C, C)], [],
                        n_rows=S, tr=128, name="l0_gate")
    o0, wkva, wkvb = _mm(p0, wout0, "nn", F32, "l0_out", jobs=[_Gather(sh_wkva), _Gather(sh_wkvb)])
    wkva = wkva.reshape(D, kva_w)

    def f_res0(r, v):
        gate, g0, b0, sc1, sh1 = v
        xh, _ = _ln_stats(ALPHA * r[0] + (1.0 + gate) * r[1])
        x1 = xh * g0 + b0
        xh1, _ = _ln_stats(x1)
        return [x1, x1, xh1 * (1.0 + sc1) + sh1], []
    (x1, x1b, h1), _ = _rowwise(f_res0, [(x2, D, 0, False), (o0, D, 0, False)],
                                [vec(gate0), vec(ln_g[0:1]), vec(ln_b[0:1]), vec(scale1), vec(shift1)],
                                [(F32, D, D), (BF16, D, D), (BF16, D, D)], [], n_rows=S, tr=128, name="l0_residual")

    kva = _mm(x1b, wkva, "nn", F32, "kv_a")
    g_kv = kv_norm_g[None, :]

    def f_kvn(r, v):
        xh, _ = _rms_stats(r[0][:, :KL])
        return [xh * v[0], _rope_fwd(r[0][:, KL:], r[1], r[2], r[3])], []
    (ckv, krp), _ = _rowwise(f_kvn, [(kva, kva_w, 0, False)] + rope_rows, [vec(g_kv)],
                             [(BF16, KL, KL), (BF16, LANES, LANES)], [], n_rows=S, tr=256, name="kv_norm_rope")
    kvh = _mm(ckv, wkvb, "nn", BF16, "kv_b", b_slots=True)

    u1, wqb = _mm(h1, wbin, "nn", F32, "l1_in", b_slots=True, jobs=[_Gather(sh_wqb)])
    g_q = b_q_norm_g[0:1]

    def f_qn(r, v):
        xh, _ = _rms_stats(r[0])
        return [xh * v[0]], []
    (cqn,), _ = _rowwise(f_qn, [(u1, QL, 0, False)], [vec(g_q)], [(BF16, QL, QL)], [], n_rows=S, tr=256, name="q_norm")
    qf = _mm(cqn, wqb, "nn", F32, "q_b", b_slots=True)

    rope_tab = jnp.concatenate([cos_t[:, :ROPE], sin_a[:, :ROPE // 2], sin_b[:, ROPE // 2:ROPE]], axis=1)
    oa, lse, qb, p1, wbout = _attn_fwd(qf, rope_tab, kvh, krp, u1, QL // VDIM, S, H, "attn_fwd", jobs=[_Gather(sh_wbout)])
    wbout = wbout.reshape(W, D)
    zc0 = QL // TW
    o1 = _mm(p1, wbout, "nn", F32, "l1_out")

    def f_head(r, v):
        gate, g1, b1 = v
        xh, rstd = _ln_stats(ALPHA * r[0] + (1.0 + gate) * r[1])
        err = xh * g1 + b1 - r[2]
        dy = err * (1.0 / D)
        dr = _ln_bwd(dy * g1, xh, rstd)
        return [dr, (1.0 + gate) * dr], [_colsum(err * err), _colsum(dy * xh), _colsum(dy), _colsum(dr * r[1])]
    (dr1, do1b), (loss_row, dlng1, dlnb1, dgate1) = _rowwise(
        f_head, [(x1, D, 0, False), (o1, D, 0, False), (tgt, D, 0, False)], [vec(gate1), vec(ln_g[1:2]), vec(ln_b[1:2])],
        [(F32, D, D), (BF16, D, D)], [D, D, D, D], n_rows=S, tr=64, name="head_loss")
    loss = lax.psum(0.5 / D * jnp.sum(loss_row), ("x", "y", "c"))

    dp1 = _mm(do1b, wbout, "nt", F32, "l1_out_dx")
    g_wbout = _mm(p1, do1b, "tn", BF16, "l1_out_dw").reshape(N_DEV, W // N_DEV, D)

    def f_dgate1(r, v):
        dp, o, z = r
        return [dp * _silu(z), dp * o * _dsilu(z)], []
    (doa, dz1), _, (got,) = _rowwise(f_dgate1, [(dp1, TW, 0, True), (oa, TW, 0, True), (u1, TW, zc0, True)], [],
                                     [(BF16, W, TW), (BF16, W, TW)], [], n_rows=S, tr=512 if S % 512 == 0 else S,
                                     name="l1_gate_bwd", nc=W // TW, jobs=[_PairExchange(g_wbout)])
    t_wbout = _pair_add(g_wbout, got, core, "rs_b_w_out_add")
    dqb, dkvh, dkrp, r_wbout = _attn_bwd(qb, rope_tab, kvh, krp, doa, oa, lse, S, H, "attn_bwd",
                                         jobs=[_ChipExchange(t_wbout)])
    g_wqb = _mm(cqn, dqb, "tn", BF16, "q_b_dw", out_slots=True)
    dcqn, got = _mm(dqb, wqb, "nt", F32, "q_b_dx", b_slots=True, jobs=[_PairExchange(g_wqb)])
    t_wqb = _pair_add(g_wqb, got, core, "rs_b_w_qb_add")

    def f_qn_bwd(r, v):
        xh, rr = _rms_stats(r[0])
        return [_rms_bwd(r[1] * v[0], xh, rr)], [_colsum(r[1] * xh)]
    (dcq,), (dg_q,) = _rowwise(f_qn_bwd, [(u1, QL, 0, False), (dcqn, QL, 0, False)], [vec(g_q)], [(BF16, QL, QL)], [QL],
                               n_rows=S, tr=256, name="q_norm_bwd")
    du1 = jnp.concatenate([dcq, dz1], axis=1)
    g_wbin, r_wqb = _mm(h1, du1, "tn", BF16, "l1_in_dw", out_slots=True, jobs=[_ChipExchange(t_wqb)])
    dh1, got = _mm(du1, wbin, "nt", F32, "l1_in_dx", b_slots=True, jobs=[_PairExchange(g_wbin)])
    t_wbin = _pair_add(g_wbin, got, core, "rs_b_w_in_add")

    g_wkvb = _mm(ckv, dkvh, "tn", BF16, "kv_b_dw", out_slots=True)
    dckv, got = _mm(dkvh, wkvb, "nt", F32, "kv_b_dx", b_slots=True, jobs=[_PairExchange(g_wkvb)])
    t_wkvb = _pair_add(g_wkvb, got, core, "rs_kv_w_b_add")

    def f_kvn_bwd(r, v):
        xh, rr = _rms_stats(r[0][:, :KL])
        dck = _rms_bwd(r[1] * v[0], xh, rr)
        dkr = _rope_bwd(r[2], r[3], r[4], r[5])
        return [jnp.concatenate([dck, dkr], axis=1)], [_colsum(r[1] * xh)]
    (dkva,), (dg_kv,) = _rowwise(f_kvn_bwd, [(kva, kva_w, 0, False), (dckv, KL, 0, False), (dkrp, LANES, 0, False)] + rope_rows,
                                 [vec(g_kv)], [(BF16, kva_w, kva_w)], [KL], n_rows=S, tr=256, name="kv_norm_rope_bwd")
    g_wkva = _mm(x1b, dkva, "tn", BF16, "kv_a_dw").reshape(N_DEV, D // N_DEV, kva_w)
    dx1_kv, got = _mm(dkva, wkva, "nt", F32, "kv_a_dx", jobs=[_PairExchange(g_wkva)])
    t_wkva = _pair_add(g_wkva, got, core, "rs_kv_w_a_add")

    def f_res0_bwd(r, v):
        dr1_, dxkv, dh, x1_, x_, o_ = r
        sc1, gate, g0 = v
        xh1, rstd1 = _ln_stats(x1_)
        dx1 = ALPHA * dr1_ + dxkv + _ln_bwd(dh * (1.0 + sc1), xh1, rstd1)
        xh0, rstd0 = _ln_stats(ALPHA * x_ + (1.0 + gate) * o_)
        dr0 = _ln_bwd(dx1 * g0, xh0, rstd0)
        return ([dr0, (1.0 + gate) * dr0],
                [_colsum(dh * xh1), _colsum(dh), _colsum(dx1 * xh0), _colsum(dx1), _colsum(dr0 * o_)])
    (dr0, do0b), (dscale1, dshift1, dlng0, dlnb0, dgate0) = _rowwise(
        f_res0_bwd, [(a, D, 0, False) for a in (dr1, dx1_kv, dh1, x1, x2, o0)], [vec(scale1), vec(gate0), vec(ln_g[0:1])],
        [(F32, D, D), (BF16, D, D)], [D] * 5, n_rows=S, tr=64, name="l0_residual_bwd")

    dp0, r_wkvb, r_wkva = _mm(do0b, wout0, "nt", F32, "l0_out_dx",
                              jobs=[_ChipExchange(t_wkvb), _ChipExchange(t_wkva)])
    g_wout0 = _mm(p0, do0b, "tn", BF16, "l0_out_dw").reshape(N_DEV, C // N_DEV, D)

    def f_gate0_bwd(r, v):
        dp, v2_, z = r
        g, b = v
        xh, rstd = _ln_stats(v2_)
        v3 = xh * g + b
        dv4 = dp * _silu(z)
        dz = dp * _silu(v3) * _dsilu(z)
        dv3 = dv4 * _dsilu(v3)
        dv2_ = _ln_bwd(dv3 * g, xh, rstd)
        return [dz, dv2_], [_colsum(dv3 * xh), _colsum(dv3), _colsum(dv2_)]
    (dz0, dv2), (dg_cn, db_cn, db_dw) = _rowwise(
        f_gate0_bwd, [(dp0, C, 0, False), (v2, C, 0, False), (u0, C, 2, False)], [vec(g_cn), vec(b_cn)],
        [(BF16, C, C), (F32, C, C)], [C, C, C], n_rows=S, tr=64, name="l0_gate_bwd")
    da0, dg0, dw_dw, r_wbin, got = _conv_bwd(u0, dv2, w_dw, S, C, "l0_conv_bwd",
                                             jobs=[_ChipExchange(t_wbin), _PairExchange(g_wout0)])
    t_wout0 = _pair_add(g_wout0, got, core, "rs_a_w_out_add")
    du0 = jnp.concatenate([da0, dg0, dz0], axis=1)
    half, quarter = D // 2, D // 4
    g0, r_wout0 = _mm(h0, du0, "tn", BF16, "l0_in_dw_0", out_slots=True, m_rows=(0, half), jobs=[_ChipExchange(t_wout0)])
    g1, got = _mm(h0, du0, "tn", BF16, "l0_in_dw_1", out_slots=True, m_rows=(half, quarter), jobs=[_PairExchange(g0)])
    t0 = _pair_add(g0, got, core, "rs_a_w_in_add_0")
    g2, got = _mm(h0, du0, "tn", BF16, "l0_in_dw_2", out_slots=True, m_rows=(half + quarter, quarter),
                  jobs=[_PairExchange(g1)])
    t1 = _pair_add(g1, got, core, "rs_a_w_in_add_1")
    dh0, r0, r1, got = _mm(du0, win0, "nt", F32, "l0_in_dx", b_slots=True,
                           jobs=[_ChipExchange(t0), _ChipExchange(t1), _PairExchange(g2)])
    t2 = _pair_add(g2, got, core, "rs_a_w_in_add_2")

    def f_mod0_bwd(r, v):
        xh, rstd = _ln_stats(r[2])
        return [ALPHA * r[0] + _ln_bwd(r[1] * (1.0 + v[0]), xh, rstd)], [_colsum(r[1] * xh), _colsum(r[1])]
    (grad_x,), (dscale0, dshift0), (r2,) = _rowwise(
        f_mod0_bwd, [(dr0, D, 0, False), (dh0, D, 0, False), (x2, D, 0, False)], [vec(scale0)], [(F32, D, D)], [D, D],
        n_rows=S, tr=128, name="l0_modulate_bwd", jobs=[_ChipExchange(t2)])
    r_win0 = jnp.concatenate([r0, r1, r2], axis=1)

    assert C == D and QL <= D and KL <= D
    n_small = 48
    singles = [dlng0, dlng1, dlnb0, dlnb1, dshift0, dscale0, dgate0, dshift1, dscale1, dgate1, db_dw, dg_cn, db_cn, dg_q, dg_kv]
    rows_loc = _pack_rows([(dw_dw, 0)] + [(a, v0 + i) for i, a in enumerate(singles)], n_small, D, "pack_small_grads")
    parts = _all_gather(rows_loc, "ag_small_grads")
    tot = _slot_sum(parts, "small_grads_sum")
    g_ln_g, g_ln_b = tot[v0:v0 + 2], tot[v0 + 2:v0 + 4]
    g_b_ada = tot[v0 + 4:v0 + 10].reshape(DEPTH, 3 * D)
    loc_cols = lambda a: lax.dynamic_slice(a, (0, idx * cl), (a.shape[0], cl))
    g_a_w_dw = loc_cols(tot[0:CONV_KERNEL])
    g_a_b_dw = loc_cols(tot[v0 + 10:v0 + 11])
    g_a_norm_g = loc_cols(tot[v0 + 11:v0 + 12])
    g_a_norm_b = loc_cols(tot[v0 + 12:v0 + 13])
    g_q_norm = tot[v0 + 13:v0 + 14, :QL]
    g_kv_norm = tot[v0 + 14:v0 + 15, :KL]

    dmods = []
    for l in range(DEPTH):
        dmod = parts[:, v0 + 4 + 3 * l:v0 + 7 + 3 * l, :].reshape(N_DEV, 3 * D)
        dmods += [lax.dynamic_slice(dmod, (0, idx * nl), (N_DEV, nl)), jnp.zeros((16 - N_DEV, nl), F32)]
    g_w_ada = _mm(sc2, jnp.concatenate(dmods, axis=0), "tn", F32, "mod_dw").reshape(DEPTH, D, nl)

    r_wqb = r_wqb.reshape(N_CHIP, QL, HL, QPAD)[..., :NOPE + ROPE].reshape(N_CHIP, QL, HL * (NOPE + ROPE))
    r_wkva = r_wkva[..., :KL + ROPE]

    def upd(name, w, gparts, m, v):
        shp = w.shape
        r2 = (1, shp[0]) if len(shp) == 1 else (math.prod(shp[:-1]), shp[-1])
        gp = gparts.reshape((gparts.shape[0],) + r2)
        res = _adamw(w.reshape(r2), gp, m.reshape(r2), v.reshape(r2), "adamw_" + name)
        return [a.reshape(shp) for a in res]

    one = lambda g: g[None]
    results = [
        upd("w_ada", w_ada, one(g_w_ada), m_w_ada, v_w_ada),
        upd("b_ada", b_ada, one(g_b_ada), m_b_ada, v_b_ada),
        upd("ln_g", ln_g, one(g_ln_g), m_ln_g, v_ln_g),
        upd("ln_b", ln_b, one(g_ln_b), m_ln_b, v_ln_b),
        upd("a_w_in", a_w_in, r_win0[:, None], m_a_w_in, v_a_w_in),
        upd("a_w_dw", a_w_dw, one(g_a_w_dw[None]), m_a_w_dw, v_a_w_dw),
        upd("a_b_dw", a_b_dw, one(g_a_b_dw), m_a_b_dw, v_a_b_dw),
        upd("a_norm_g", a_norm_g, one(g_a_norm_g), m_a_norm_g, v_a_norm_g),
        upd("a_norm_b", a_norm_b, one(g_a_norm_b), m_a_norm_b, v_a_norm_b),
        upd("a_w_out", a_w_out, r_wout0[:, None], m_a_w_out, v_a_w_out),
        upd("b_w_in", b_w_in, r_wbin[:, None], m_b_w_in, v_b_w_in),
        upd("b_q_norm_g", b_q_norm_g, one(g_q_norm), m_b_q_norm_g, v_b_q_norm_g),
        upd("b_w_qb", b_w_qb, r_wqb[:, None], m_b_w_qb, v_b_w_qb),
        upd("b_w_out", b_w_out, r_wbout[:, None], m_b_w_out, v_b_w_out),
        upd("kv_w_a", kv_w_a, r_wkva, m_kv_w_a, v_kv_w_a),
        upd("kv_norm_g", kv_norm_g, one(g_kv_norm[0]), m_kv_norm_g, v_kv_norm_g),
        upd("kv_w_b", kv_w_b, r_wkvb, m_kv_w_b, v_kv_w_b),
    ]
    grads, deltas, new_m, new_v = zip(*results)
    return (loss, grad_x[None], *grads, *deltas, *new_m, *new_v)
```
